```python
import jax, jax.numpy as jnp
from jax import lax
import numpy as np

D_MODEL = 1024
BATCH = 2
SEQ = 8192
DEPTH = 4

CHUNK = 64
EPS = 1e-6
N_BRANCH = 2
CONV_DIM = D_MODEL
CONV_WIDTH = 3
SSM_EXPAND = 2
D_SSM = SSM_EXPAND * D_MODEL
SSM_HEAD_DIM = 64
SSM_HEADS = D_SSM // SSM_HEAD_DIM
SSM_GROUPS = 8
SSM_STATE = 128
SSM_CONV_WIDTH = 4
SSM_CONV_DIM = D_SSM + 2 * SSM_GROUPS * SSM_STATE
DT_MIN = 1e-3
DT_MAX = 1e-1
D_FF = 4 * D_MODEL
N_MOD = 6

kernel_name = "hybrid_shortconv_ssd_gated_trunk"


def proj_sizes():
    return (N_BRANCH * D_MODEL,
            CONV_DIM, CONV_DIM, CONV_DIM,
            D_SSM,
            SSM_CONV_DIM,
            SSM_HEADS)


def rms_norm(x, w):
    xf = x.astype(jnp.float32)
    y = xf * lax.rsqrt(jnp.mean(xf * xf, axis=-1, keepdims=True) + EPS)
    return (y * w.astype(jnp.float32)).astype(x.dtype)


def causal_depthwise_conv(x, w):
    k = w.shape[0]
    return lax.conv_general_dilated(
        x, w[:, None, :].astype(x.dtype), window_strides=(1,), padding=((k - 1, 0),),
        dimension_numbers=('NWC', 'WIO', 'NWC'), feature_group_count=x.shape[-1])


def ssd_scan(x, a, b, c):
    bsz, seqlen, h, p = x.shape
    g, n = b.shape[-2:]
    r = h // g
    nc = seqlen // CHUNK

    def to_chunks(t):
        return jnp.moveaxis(t.reshape(bsz, nc, CHUNK, *t.shape[2:]), 1, 0)

    xs = (to_chunks(x.reshape(bsz, seqlen, g, r, p)),
          to_chunks(a.reshape(bsz, seqlen, g, r)),
          to_chunks(b), to_chunks(c))
    tril = jnp.tril(jnp.ones((CHUNK, CHUNK), dtype=bool))[None, :, :, None, None]

    def step(state, inp):
        xq, aq, bq, cq = inp
        a_cum = jnp.cumsum(aq, axis=1)
        seg = a_cum[:, :, None] - a_cum[:, None, :]
        decay = jnp.exp(jnp.where(tril, seg, -jnp.inf))
        scores = jnp.einsum('btgn,bsgn->btsg', cq, bq)
        y_diag = jnp.einsum('btsg,btsgr,bsgrp->btgrp', scores, decay, xq)
        y_off = jnp.einsum('btgn,bgrpn->btgrp', cq, state) * jnp.exp(a_cum)[..., None]
        to_end = jnp.exp(a_cum[:, -1:] - a_cum)
        new_state = (state * jnp.exp(a_cum[:, -1])[..., None, None]
                     + jnp.einsum('bsgn,bsgr,bsgrp->bgrpn', bq, to_end, xq))
        return new_state, y_diag + y_off

    state0 = jnp.zeros((bsz, g, r, p, n), jnp.float32)
    _, y = lax.scan(step, state0, xs)
    return jnp.moveaxis(y, 0, 1).reshape(bsz, seqlen, h, p)


def mixer_sublayer(u, w_in, conv_w, ssm_conv_w, ssm_conv_b, dt_bias, a_log, d_skip,
                   ssm_norm_w, w_conv_out, w_ssm_out, w_o):
    bsz, seqlen, _ = u.shape
    proj = u @ w_in
    split_at = [int(v) for v in np.cumsum(proj_sizes())[:-1]]
    gl, cb, cc, cx, z, xbc, dt = jnp.split(proj, split_at, axis=-1)

    y_conv = cb * causal_depthwise_conv(cc * cx, conv_w)
    p_conv = y_conv @ w_conv_out

    xbc = jax.nn.silu(causal_depthwise_conv(xbc, ssm_conv_w) + ssm_conv_b.astype(xbc.dtype))
    xbc = xbc.astype(jnp.float32)
    xs, bs, cs = jnp.split(xbc, [D_SSM, D_SSM + SSM_GROUPS * SSM_STATE], axis=-1)
    xs = xs.reshape(bsz, seqlen, SSM_HEADS, SSM_HEAD_DIM)
    bs = bs.reshape(bsz, seqlen, SSM_GROUPS, SSM_STATE)
    cs = cs.reshape(bsz, seqlen, SSM_GROUPS, SSM_STATE)
    dt = jax.nn.softplus(dt.astype(jnp.float32) + dt_bias.astype(jnp.float32))
    a = -jnp.exp(a_log.astype(jnp.float32))
    y = ssd_scan(xs * dt[..., None], dt * a, bs, cs)
    y = y + d_skip.astype(jnp.float32)[:, None] * xs
    y = y.reshape(bsz, seqlen, D_SSM) * jax.nn.silu(z.astype(jnp.float32))
    yg = y.reshape(bsz, seqlen, SSM_GROUPS, D_SSM // SSM_GROUPS)
    yg = yg * lax.rsqrt(jnp.mean(yg * yg, axis=-1, keepdims=True) + EPS)
    y = (yg.reshape(bsz, seqlen, D_SSM) * ssm_norm_w.astype(jnp.float32)).astype(u.dtype)
    p_ssm = y @ w_ssm_out

    g_conv, g_ssm = jnp.split(jax.nn.sigmoid(gl), 2, axis=-1)
    merged = g_conv * p_conv + g_ssm * p_ssm
    return merged @ w_o


def setup_inputs(seed: int = 0) -> dict:
    key = jax.random.key(seed)
    ks = jax.random.split(key, 24)
    d_proj = sum(proj_sizes())
    f32 = jnp.float32

    def nrm(k, shape, scale):
        return jax.random.normal(k, shape, f32) * scale

    dt0 = jnp.exp(jax.random.uniform(ks[10], (DEPTH, SSM_HEADS), f32,
                                     np.log(DT_MIN), np.log(DT_MAX)))
    dt_bias = dt0 + jnp.log(-jnp.expm1(-dt0))
    return {
        "x": nrm(ks[0], (BATCH, SEQ, D_MODEL), 1.0),
        "c": nrm(ks[1], (BATCH, D_MODEL), 1.0),
        "w_ada": nrm(ks[2], (DEPTH, D_MODEL, N_MOD * D_MODEL), 0.5 * D_MODEL ** -0.5),
        "b_ada": nrm(ks[3], (DEPTH, N_MOD * D_MODEL), 0.02),
        "ln1": 1.0 + nrm(ks[4], (DEPTH, D_MODEL), 0.1),
        "ln2": 1.0 + nrm(ks[5], (DEPTH, D_MODEL), 0.1),
        "w_in": nrm(ks[6], (DEPTH, D_MODEL, d_proj), D_MODEL ** -0.5),
        "conv_w": nrm(ks[7], (DEPTH, CONV_WIDTH, CONV_DIM), CONV_WIDTH ** -0.5),
        "ssm_conv_w": nrm(ks[8], (DEPTH, SSM_CONV_WIDTH, SSM_CONV_DIM), SSM_CONV_WIDTH ** -0.5),
        "ssm_conv_b": nrm(ks[9], (DEPTH, SSM_CONV_DIM), 0.01),
        "dt_bias": dt_bias,
        "a_log": jnp.log(jax.random.uniform(ks[11], (DEPTH, SSM_HEADS), f32, 1.0, 16.0)),
        "d_skip": 1.0 + nrm(ks[12], (DEPTH, SSM_HEADS), 0.1),
        "ssm_norm_w": 1.0 + nrm(ks[13], (DEPTH, D_SSM), 0.1),
        "w_conv_out": nrm(ks[14], (DEPTH, CONV_DIM, D_MODEL), CONV_DIM ** -0.5),
        "w_ssm_out": nrm(ks[15], (DEPTH, D_SSM, D_MODEL), D_SSM ** -0.5),
        "w_o": nrm(ks[16], (DEPTH, D_MODEL, D_MODEL), D_MODEL ** -0.5),
        "w_up": nrm(ks[17], (DEPTH, D_MODEL, D_FF), D_MODEL ** -0.5),
        "w_down": nrm(ks[18], (DEPTH, D_FF, D_MODEL), D_FF ** -0.5),
        "final_norm": 1.0 + nrm(ks[19], (D_MODEL,), 0.1),
    }


def reference(x, c, w_ada, b_ada, ln1, ln2, w_in, conv_w, ssm_conv_w, ssm_conv_b,
              dt_bias, a_log, d_skip, ssm_norm_w, w_conv_out, w_ssm_out, w_o,
              w_up, w_down, final_norm):
    bsz = x.shape[0]
    c_act = jax.nn.silu(c)
    for i in range(DEPTH):
        mod = (c_act @ w_ada[i] + b_ada[i]).reshape(bsz, N_MOD, D_MODEL)[:, :, None, :]
        shift1, scale1, gate1 = mod[:, 0], mod[:, 1], mod[:, 2]
        shift2, scale2, gate2 = mod[:, 3], mod[:, 4], mod[:, 5]

        u = rms_norm(x, ln1[i]) * (1.0 + scale1) + shift1
        mix = mixer_sublayer(u, w_in[i], conv_w[i], ssm_conv_w[i], ssm_conv_b[i], dt_bias[i],
                             a_log[i], d_skip[i], ssm_norm_w[i], w_conv_out[i],
                             w_ssm_out[i], w_o[i])
        x = x + gate1 * mix

        u2 = rms_norm(x, ln2[i]) * (1.0 + scale2) + shift2
        hid = jnp.square(jax.nn.relu(u2 @ w_up[i]))
        x = x + gate2 * (hid @ w_down[i])
    return rms_norm(x, final_norm)
```

```python
import functools

import numpy as np
import jax
import jax.numpy as jnp
from jax import lax
from jax.experimental import pallas as pl
from jax.experimental.pallas import tpu as pltpu

D_MODEL = 1024
DEPTH = 4
EPS = 1e-6
N_MOD = 6
MOD_ROWS = 8
CONV_DIM = D_MODEL
CONV_WIDTH = 3
D_SSM = 2 * D_MODEL
HEAD_DIM = 64
HEADS = D_SSM // HEAD_DIM
GROUPS = 8
HEADS_PER_GROUP = HEADS // GROUPS
GROUP_X = HEADS_PER_GROUP * HEAD_DIM
STATE = 128
SSM_CONV_WIDTH = 4
D_FF = 4 * D_MODEL
LANES = 128
SUBLANES = 8

OFF_GATE = 0
OFF_CB = 2 * D_MODEL
OFF_CC = OFF_CB + CONV_DIM
OFF_CX = OFF_CC + CONV_DIM
OFF_Z = OFF_CX + CONV_DIM
OFF_XS = OFF_Z + D_SSM
OFF_BS = OFF_XS + D_SSM
OFF_CS = OFF_BS + GROUPS * STATE
N_MAIN = OFF_CS + GROUPS * STATE

SCAN_CHUNK = 128
GROUP_CONV = GROUP_X + 2 * STATE
SHORT_PER_GROUP = CONV_DIM // GROUPS

VMEM_LIMIT = 48 * 1024 * 1024


def _cparams(n_axes):
    return pltpu.CompilerParams(
        dimension_semantics=("arbitrary",) * n_axes,
        vmem_limit_bytes=VMEM_LIMIT)


def _dot(a, b):
    return jnp.dot(a, b, preferred_element_type=jnp.float32)


def _split3(v):
    hi = v.astype(jnp.bfloat16)
    r1 = v - hi.astype(jnp.float32)
    mid = r1.astype(jnp.bfloat16)
    lo = (r1 - mid.astype(jnp.float32)).astype(jnp.bfloat16)
    return hi, mid, lo


def _silu(v):
    return v / (1.0 + jnp.exp(-v))


def _ada_kernel(c_ref, w_ref, b_ref, o_ref):
    c = c_ref[...]
    o_ref[0] = _dot(_silu(c), w_ref[0]) + b_ref[0]


def _ada_call(c_pad, w_ada, b_ada):
    n_out = N_MOD * D_MODEL
    tn = D_MODEL
    return pl.pallas_call(
        _ada_kernel,
        grid=(DEPTH, n_out // tn),
        in_specs=[
            pl.BlockSpec((SUBLANES, D_MODEL), lambda l, j: (0, 0)),
            pl.BlockSpec((1, D_MODEL, tn), lambda l, j: (l, 0, j)),
            pl.BlockSpec((1, 1, tn), lambda l, j: (l, 0, j)),
        ],
        out_specs=pl.BlockSpec((1, SUBLANES, tn), lambda l, j: (l, 0, j)),
        out_shape=jax.ShapeDtypeStruct((DEPTH, SUBLANES, n_out), jnp.float32),
        compiler_params=_cparams(2),
        name="ada_mod",
    )(c_pad, w_ada, b_ada.reshape(DEPTH, 1, n_out))


def _modnorm_to(x_ref, ln_ref, scale, shift, u_ref, rows, chunk=256):
    w = ln_ref[...] * (1.0 + scale)

    def body(r, carry):
        r0 = pl.multiple_of(r * chunk, chunk)
        xf = x_ref[pl.ds(r0, chunk), :]
        ms = jnp.mean(xf * xf, axis=-1, keepdims=True)
        y = xf * lax.rsqrt(ms + EPS)
        u_ref[pl.ds(r0, chunk), :] = (y * w + shift).astype(u_ref.dtype)
        return carry

    lax.fori_loop(0, rows // chunk, body, 0)


def _inproj_kernel(x_ref, mod_ref, ln_ref, w_ref, wdt_ref, proj_ref, dt_ref, u_ref, *, tm):
    j = pl.program_id(1)

    @pl.when(j == 0)
    def _():
        shift = mod_ref[0, 0:1, :]
        scale = mod_ref[0, 1:2, :]
        _modnorm_to(x_ref, ln_ref, scale, shift, u_ref, tm)
        dt_ref[...] = _dot(u_ref[...], wdt_ref[...])

    proj_ref[...] = _dot(u_ref[...], w_ref[...]).astype(proj_ref.dtype)


def _inproj_call(x2, mod_l, ln, w_main, w_dt, seq, tm=1024, tn=1024):
    t_total = x2.shape[0]
    tiles_per_batch = seq // tm
    return pl.pallas_call(
        functools.partial(_inproj_kernel, tm=tm),
        grid=(t_total // tm, N_MAIN // tn),
        in_specs=[
            pl.BlockSpec((tm, D_MODEL), lambda i, j: (i, 0)),
            pl.BlockSpec((1, MOD_ROWS, D_MODEL), lambda i, j: (i // tiles_per_batch, 0, 0)),
            pl.BlockSpec((1, D_MODEL), lambda i, j: (0, 0)),
            pl.BlockSpec((D_MODEL, tn), lambda i, j: (0, j)),
            pl.BlockSpec((D_MODEL, LANES), lambda i, j: (0, 0)),
        ],
        out_specs=[
            pl.BlockSpec((tm, tn), lambda i, j: (i, j)),
            pl.BlockSpec((tm, LANES), lambda i, j: (i, 0)),
        ],
        out_shape=[
            jax.ShapeDtypeStruct((t_total, N_MAIN), jnp.bfloat16),
            jax.ShapeDtypeStruct((t_total, LANES), jnp.float32),
        ],
        scratch_shapes=[pltpu.VMEM((tm, D_MODEL), jnp.bfloat16)],
        compiler_params=_cparams(2),
        name="in_proj",
    )(x2, mod_l, ln, w_main, w_dt)


def _mixer_kernel(xs_ref, bs_ref, cs_ref, z_ref, cb_ref, cc_ref, cx_ref, dt_ref,
                  cw_ref, cbias_ref, scw_ref, hp_ref, dskip_ref, nw_ref,
                  e64_ref, e128_ref, esel_ref, tril_ref,
                  yssm_ref, yconv_ref,
                  raw_ref, sraw_ref, state_ref, *, tb):
    t = pl.program_id(2)
    q = SCAN_CHUNK
    halo = SUBLANES

    @pl.when(t == 0)
    def _():
        raw_ref[0:halo, :] = jnp.zeros((halo, GROUP_CONV), jnp.float32)
        sraw_ref[0:halo, :] = jnp.zeros((halo, SHORT_PER_GROUP), jnp.float32)
        state_ref[...] = jnp.zeros_like(state_ref)

    @pl.when(t != 0)
    def _():
        raw_ref[0:halo, :] = raw_ref[tb:tb + halo, :]
        sraw_ref[0:halo, :] = sraw_ref[tb:tb + halo, :]

    raw_ref[halo:halo + tb, 0:GROUP_X] = xs_ref[...].astype(jnp.float32)
    raw_ref[halo:halo + tb, GROUP_X:GROUP_X + STATE] = bs_ref[...].astype(jnp.float32)
    raw_ref[halo:halo + tb, GROUP_X + STATE:GROUP_CONV] = cs_ref[...].astype(jnp.float32)
    sraw_ref[halo:halo + tb, :] = (cc_ref[...].astype(jnp.float32)
                                   * cx_ref[...].astype(jnp.float32))

    cw = cw_ref[0]
    cbias = cbias_ref[0]
    scw = scw_ref[0]
    dt_bias = hp_ref[0:1, :]
    a_neg = -jnp.exp(hp_ref[1:2, :])
    dskip = dskip_ref[0]
    nw = nw_ref[0]
    e64 = e64_ref[0]
    e128 = e128_ref[0]
    esel = esel_ref[0]
    tril_b = tril_ref[...]

    row = lax.broadcasted_iota(jnp.int32, (q, q), 0)
    col = lax.broadcasted_iota(jnp.int32, (q, q), 1)
    causal = row >= col
    lane_head = lax.broadcasted_iota(jnp.int32, (1, GROUP_X), 1) // HEAD_DIM

    def expand(parts, e):
        return _dot(parts[0], e) + _dot(parts[1], e) + _dot(parts[2], e)

    for c in range(tb // q):
        r0 = c * q
        sconv = scw[0:1, :] * sraw_ref[halo + r0 - 2:halo + r0 - 2 + q, :]
        sconv += scw[1:2, :] * sraw_ref[halo + r0 - 1:halo + r0 - 1 + q, :]
        sconv += scw[2:3, :] * sraw_ref[halo + r0:halo + r0 + q, :]
        yconv_ref[r0:r0 + q, :] = (cb_ref[r0:r0 + q, :].astype(jnp.float32)
                                   * sconv).astype(yconv_ref.dtype)

        acc = cbias + cw[0:1, :] * raw_ref[halo + r0 - 3:halo + r0 - 3 + q, :]
        acc += cw[1:2, :] * raw_ref[halo + r0 - 2:halo + r0 - 2 + q, :]
        acc += cw[2:3, :] * raw_ref[halo + r0 - 1:halo + r0 - 1 + q, :]
        acc += cw[3:4, :] * raw_ref[halo + r0:halo + r0 + q, :]
        xbc = _silu(acc)
        xs = xbc[:, 0:GROUP_X]
        b_b = xbc[:, GROUP_X:GROUP_X + STATE].astype(jnp.bfloat16)
        c_b = xbc[:, GROUP_X + STATE:GROUP_CONV].astype(jnp.bfloat16)

        dtr = dt_ref[r0:r0 + q, :] + dt_bias
        dt_all = jnp.maximum(dtr, 0.0) + jnp.log1p(jnp.exp(-jnp.abs(dtr)))
        a_all = dt_all * a_neg
        a_parts = _split3(a_all)
        acum_all = (_dot(tril_b, a_parts[0]) + _dot(tril_b, a_parts[1])
                    + _dot(tril_b, a_parts[2]))
        acum_parts = _split3(acum_all)
        dt_e = expand(_split3(dt_all), e64)
        acum_e = expand(acum_parts, e64)
        acum_l = expand(acum_parts, e128)
        nt = (((1,), (1,)), ((), ()))
        acum_t = (lax.dot_general(esel, acum_parts[0], nt, preferred_element_type=jnp.float32)
                  + lax.dot_general(esel, acum_parts[1], nt, preferred_element_type=jnp.float32)
                  + lax.dot_general(esel, acum_parts[2], nt, preferred_element_type=jnp.float32))

        xdt = xs * dt_e
        acum_last = acum_e[q - 1:q, :]
        xw_b = (xdt * jnp.exp(acum_last - acum_e)).astype(jnp.bfloat16)

        scores = lax.dot_general(c_b, b_b, nt, preferred_element_type=jnp.float32)
        state = state_ref[...]
        y = _dot(c_b, state.astype(jnp.bfloat16)) * jnp.exp(acum_e)
        for r in range(HEADS_PER_GROUP):
            seg = acum_l[:, r * q:(r + 1) * q] - acum_t[r:r + 1, :]
            decay = jnp.exp(jnp.where(causal, seg, -jnp.inf))
            w_r = (scores * decay).astype(jnp.bfloat16)
            x_r = jnp.where(lane_head == r, xdt, 0.0).astype(jnp.bfloat16)
            y = y + _dot(w_r, x_r)
        tn_dims = (((0,), (0,)), ((), ()))
        state_ref[...] = (state * jnp.exp(acum_last)
                          + lax.dot_general(b_b, xw_b, tn_dims,
                                            preferred_element_type=jnp.float32))

        y = y + dskip * xs
        y = y * _silu(z_ref[r0:r0 + q, :].astype(jnp.float32))
        y = y * lax.rsqrt(jnp.mean(y * y, axis=-1, keepdims=True) + EPS)
        yssm_ref[r0:r0 + q, :] = (y * nw).astype(yssm_ref.dtype)


def _mixer_call(proj, dt_raw, cw, cbias, scw, hp, dskip_e, nw, e64, e128, esel, tril,
                bsz, seq, tb=512):
    t_total = proj.shape[0]
    nt = seq // tb
    rowmap = lambda b, g, t: b * nt + t

    def colspec(width, off):
        base = off // width
        return pl.BlockSpec((tb, width), lambda b, g, t: (rowmap(b, g, t), base + g))

    def gspec(shape):
        return pl.BlockSpec((1,) + shape, lambda b, g, t: (g, 0, 0))

    return pl.pallas_call(
        functools.partial(_mixer_kernel, tb=tb),
        grid=(bsz, GROUPS, nt),
        in_specs=[
            colspec(GROUP_X, OFF_XS),
            colspec(STATE, OFF_BS),
            colspec(STATE, OFF_CS),
            colspec(GROUP_X, OFF_Z),
            colspec(SHORT_PER_GROUP, OFF_CB),
            colspec(SHORT_PER_GROUP, OFF_CC),
            colspec(SHORT_PER_GROUP, OFF_CX),
            pl.BlockSpec((tb, LANES), lambda b, g, t: (rowmap(b, g, t), 0)),
            gspec((SSM_CONV_WIDTH, GROUP_CONV)),
            gspec((1, GROUP_CONV)),
            gspec((CONV_WIDTH, SHORT_PER_GROUP)),
            pl.BlockSpec((SUBLANES, LANES), lambda b, g, t: (0, 0)),
            gspec((1, GROUP_X)),
            gspec((1, GROUP_X)),
            gspec((LANES, GROUP_X)),
            gspec((LANES, HEADS_PER_GROUP * SCAN_CHUNK)),
            gspec((SUBLANES, LANES)),
            pl.BlockSpec((SCAN_CHUNK, SCAN_CHUNK), lambda b, g, t: (0, 0)),
        ],
        out_specs=[
            pl.BlockSpec((tb, GROUP_X), lambda b, g, t: (rowmap(b, g, t), g)),
            pl.BlockSpec((tb, SHORT_PER_GROUP), lambda b, g, t: (rowmap(b, g, t), g)),
        ],
        out_shape=[
            jax.ShapeDtypeStruct((t_total, D_SSM), jnp.bfloat16),
            jax.ShapeDtypeStruct((t_total, CONV_DIM), jnp.bfloat16),
        ],
        scratch_shapes=[
            pltpu.VMEM((tb + 2 * SUBLANES, GROUP_CONV), jnp.float32),
            pltpu.VMEM((tb + 2 * SUBLANES, SHORT_PER_GROUP), jnp.float32),
            pltpu.VMEM((STATE, GROUP_X), jnp.float32),
        ],
        compiler_params=_cparams(3),
        name="mixer",
    )(proj, proj, proj, proj, proj, proj, proj, dt_raw,
      cw, cbias, scw, hp, dskip_e, nw, e64, e128, esel, tril)


def _out_kernel(yc_ref, ys_ref, gl_ref, x_ref, mod_ref, wc_ref, ws_ref, wo_ref, o_ref):
    p_conv = _dot(yc_ref[...], wc_ref[...])
    p_ssm = _dot(ys_ref[...], ws_ref[...])
    gl = gl_ref[...].astype(jnp.float32)
    g = 1.0 / (1.0 + jnp.exp(-gl))
    merged = g[:, :D_MODEL] * p_conv + g[:, D_MODEL:] * p_ssm
    mix = _dot(merged.astype(jnp.bfloat16), wo_ref[...])
    gate1 = mod_ref[0, 2:3, :]
    o_ref[...] = x_ref[...] + gate1 * mix


def _out_call(y_conv, y_ssm, proj, x2, mod_l, wc, ws, wo, seq, tm=512):
    t_total = x2.shape[0]
    tiles_per_batch = seq // tm
    return pl.pallas_call(
        _out_kernel,
        grid=(t_total // tm,),
        in_specs=[
            pl.BlockSpec((tm, CONV_DIM), lambda i: (i, 0)),
            pl.BlockSpec((tm, D_SSM), lambda i: (i, 0)),
            pl.BlockSpec((tm, 2 * D_MODEL), lambda i: (i, 0)),
            pl.BlockSpec((tm, D_MODEL), lambda i: (i, 0)),
            pl.BlockSpec((1, MOD_ROWS, D_MODEL), lambda i: (i // tiles_per_batch, 0, 0)),
            pl.BlockSpec((CONV_DIM, D_MODEL), lambda i: (0, 0)),
            pl.BlockSpec((D_SSM, D_MODEL), lambda i: (0, 0)),
            pl.BlockSpec((D_MODEL, D_MODEL), lambda i: (0, 0)),
        ],
        out_specs=pl.BlockSpec((tm, D_MODEL), lambda i: (i, 0)),
        out_shape=jax.ShapeDtypeStruct((t_total, D_MODEL), jnp.float32),
        compiler_params=_cparams(1),
        name="mix_out",
    )(y_conv, y_ssm, proj, x2, mod_l, wc, ws, wo)


def _mlp_kernel(x_ref, mod_ref, ln_ref, wup_ref, wdn_ref, fn_ref, o_ref, u_ref, acc_ref,
                *, tm, final):
    j = pl.program_id(1)

    @pl.when(j == 0)
    def _():
        shift = mod_ref[0, 3:4, :]
        scale = mod_ref[0, 4:5, :]
        _modnorm_to(x_ref, ln_ref, scale, shift, u_ref, tm)
        acc_ref[...] = jnp.zeros_like(acc_ref)

    h = jnp.maximum(_dot(u_ref[...], wup_ref[...]), 0.0)
    acc_ref[...] += _dot((h * h).astype(jnp.bfloat16), wdn_ref[...])

    @pl.when(j == pl.num_programs(1) - 1)
    def _():
        gate2 = mod_ref[0, 5:6, :]
        xn = x_ref[...] + gate2 * acc_ref[...]
        if final:
            ms = jnp.mean(xn * xn, axis=-1, keepdims=True)
            xn = xn * lax.rsqrt(ms + EPS) * fn_ref[...]
        o_ref[...] = xn


def _mlp_call(x2, mod_l, ln, w_up, w_dn, fnorm, seq, final, tm=1024, tf=1024):
    t_total = x2.shape[0]
    tiles_per_batch = seq // tm
    return pl.pallas_call(
        functools.partial(_mlp_kernel, tm=tm, final=final),
        grid=(t_total // tm, D_FF // tf),
        in_specs=[
            pl.BlockSpec((tm, D_MODEL), lambda i, j: (i, 0)),
            pl.BlockSpec((1, MOD_ROWS, D_MODEL), lambda i, j: (i // tiles_per_batch, 0, 0)),
            pl.BlockSpec((1, D_MODEL), lambda i, j: (0, 0)),
            pl.BlockSpec((D_MODEL, tf), lambda i, j: (0, j)),
            pl.BlockSpec((tf, D_MODEL), lambda i, j: (j, 0)),
            pl.BlockSpec((1, D_MODEL), lambda i, j: (0, 0)),
        ],
        out_specs=pl.BlockSpec((tm, D_MODEL), lambda i, j: (i, 0)),
        out_shape=jax.ShapeDtypeStruct((t_total, D_MODEL), jnp.float32),
        scratch_shapes=[pltpu.VMEM((tm, D_MODEL), jnp.bfloat16),
                        pltpu.VMEM((tm, D_MODEL), jnp.float32)],
        compiler_params=_cparams(2),
        name="mlp",
    )(x2, mod_l, ln, w_up, w_dn, fnorm)


def _head_constants():
    e64 = np.zeros((GROUPS, LANES, GROUP_X), np.float32)
    e128 = np.zeros((GROUPS, LANES, HEADS_PER_GROUP * SCAN_CHUNK), np.float32)
    esel = np.zeros((GROUPS, SUBLANES, LANES), np.float32)
    for g in range(GROUPS):
        for r in range(HEADS_PER_GROUP):
            h = g * HEADS_PER_GROUP + r
            e64[g, h, r * HEAD_DIM:(r + 1) * HEAD_DIM] = 1.0
            e128[g, h, r * SCAN_CHUNK:(r + 1) * SCAN_CHUNK] = 1.0
            esel[g, r, h] = 1.0
    tril = np.tril(np.ones((SCAN_CHUNK, SCAN_CHUNK), np.float32))
    bf = jnp.bfloat16
    return (jnp.asarray(e64, bf), jnp.asarray(e128, bf), jnp.asarray(esel, bf),
            jnp.asarray(tril, bf))


def _group_cols(w_l):
    k = w_l.shape[0]
    xs = w_l[:, :D_SSM].reshape(k, GROUPS, GROUP_X)
    bs = w_l[:, D_SSM:D_SSM + GROUPS * STATE].reshape(k, GROUPS, STATE)
    cs = w_l[:, D_SSM + GROUPS * STATE:].reshape(k, GROUPS, STATE)
    return jnp.transpose(jnp.concatenate([xs, bs, cs], axis=-1), (1, 0, 2))


def kernel(x, c, w_ada, b_ada, ln1, ln2, w_in, conv_w, ssm_conv_w, ssm_conv_b, dt_bias, a_log,
           d_skip, ssm_norm_w, w_conv_out, w_ssm_out, w_o, w_up, w_down, final_norm):
    bsz, seq, _ = x.shape
    t_total = bsz * seq
    bf = jnp.bfloat16
    f32 = jnp.float32

    c_pad = jnp.zeros((SUBLANES, D_MODEL), f32).at[:bsz].set(c)
    mod_all = _ada_call(c_pad, w_ada, b_ada)
    mod_all = mod_all[:, :bsz].reshape(DEPTH, bsz, N_MOD, D_MODEL)
    mod_all = jnp.pad(mod_all, ((0, 0), (0, 0), (0, MOD_ROWS - N_MOD), (0, 0)))

    e64, e128, esel, tril = _head_constants()
    fnorm = final_norm.reshape(1, D_MODEL)
    x2 = x.reshape(t_total, D_MODEL)

    for l in range(DEPTH):
        mod_l = mod_all[l]
        w_main = w_in[l, :, :N_MAIN].astype(bf)
        w_dt = jnp.pad(w_in[l, :, N_MAIN:], ((0, 0), (0, LANES - HEADS))).astype(bf)
        proj, dt_raw = _inproj_call(x2, mod_l, ln1[l].reshape(1, D_MODEL), w_main, w_dt, seq)

        cw = _group_cols(ssm_conv_w[l])
        cbias = _group_cols(ssm_conv_b[l].reshape(1, -1))
        scw = jnp.transpose(conv_w[l].reshape(CONV_WIDTH, GROUPS, SHORT_PER_GROUP), (1, 0, 2))
        hp = jnp.zeros((SUBLANES, LANES), f32)
        hp = hp.at[0, :HEADS].set(dt_bias[l]).at[1, :HEADS].set(a_log[l])
        dskip_e = jnp.repeat(d_skip[l], HEAD_DIM).reshape(GROUPS, 1, GROUP_X)
        nw = ssm_norm_w[l].reshape(GROUPS, 1, GROUP_X)
        y_ssm, y_conv = _mixer_call(proj, dt_raw, cw, cbias, scw, hp, dskip_e, nw,
                                    e64, e128, esel, tril, bsz, seq)

        x2 = _out_call(y_conv, y_ssm, proj, x2, mod_l, w_conv_out[l].astype(bf),
                       w_ssm_out[l].astype(bf), w_o[l].astype(bf), seq)
        x2 = _mlp_call(x2, mod_l, ln2[l].reshape(1, D_MODEL), w_up[l].astype(bf),
                       w_down[l].astype(bf), fnorm, seq, final=(l == DEPTH - 1))
    return x2.reshape(bsz, seq, D_MODEL)
```

```python
import functools

import numpy as np
import jax
import jax.numpy as jnp
from jax import lax
from jax.experimental import pallas as pl
from jax.experimental.pallas import tpu as pltpu

D_MODEL = 1024
DEPTH = 4
EPS = 1e-6
N_MOD = 6
MOD_ROWS = 8
CONV_DIM = D_MODEL
CONV_WIDTH = 3
D_SSM = 2 * D_MODEL
HEAD_DIM = 64
HEADS = D_SSM // HEAD_DIM
GROUPS = 8
HEADS_PER_GROUP = HEADS // GROUPS
GROUP_X = HEADS_PER_GROUP * HEAD_DIM
STATE = 128
SSM_CONV_WIDTH = 4
D_FF = 4 * D_MODEL
LANES = 128
SUBLANES = 8
F32_TINY = float(np.finfo(np.float32).tiny)

OFF_GATE = 0
OFF_CB = 2 * D_MODEL
OFF_CC = OFF_CB + CONV_DIM
OFF_CX = OFF_CC + CONV_DIM
OFF_Z = OFF_CX + CONV_DIM
OFF_XS = OFF_Z + D_SSM
OFF_BS = OFF_XS + D_SSM
OFF_CS = OFF_BS + GROUPS * STATE
N_MAIN = OFF_CS + GROUPS * STATE

SCAN_CHUNK = 128
SLABS = SCAN_CHUNK // SUBLANES
GROUP_CONV = GROUP_X + 2 * STATE
SHORT_PER_GROUP = CONV_DIM // GROUPS
N_SPLIT = 3
COL_W = 2 * N_SPLIT * LANES

VMEM_LIMIT = 48 * 1024 * 1024


def _cparams(n_axes):
    return pltpu.CompilerParams(
        dimension_semantics=("arbitrary",) * n_axes,
        vmem_limit_bytes=VMEM_LIMIT)


def _dot(a, b):
    return jnp.dot(a, b, preferred_element_type=jnp.float32)


def _split3(v):
    hi = v.astype(jnp.bfloat16)
    r1 = v - hi.astype(jnp.float32)
    mid = r1.astype(jnp.bfloat16)
    lo = (r1 - mid.astype(jnp.float32)).astype(jnp.bfloat16)
    return hi, mid, lo


def _silu(v):
    return v / (1.0 + jnp.exp(-v))


def _chunk_time(p):
    return (p % SUBLANES) * SLABS + p // SUBLANES


def _ada_kernel(c_ref, w_ref, b_ref, o_ref):
    c = c_ref[...]
    o_ref[0] = _dot(_silu(c), w_ref[0]) + b_ref[0]


def _ada_call(c_pad, w_ada, b_ada):
    n_out = N_MOD * D_MODEL
    tn = D_MODEL
    return pl.pallas_call(
        _ada_kernel,
        grid=(DEPTH, n_out // tn),
        in_specs=[
            pl.BlockSpec((SUBLANES, D_MODEL), lambda l, j: (0, 0)),
            pl.BlockSpec((1, D_MODEL, tn), lambda l, j: (l, 0, j)),
            pl.BlockSpec((1, 1, tn), lambda l, j: (l, 0, j)),
        ],
        out_specs=pl.BlockSpec((1, SUBLANES, tn), lambda l, j: (l, 0, j)),
        out_shape=jax.ShapeDtypeStruct((DEPTH, SUBLANES, n_out), jnp.float32),
        compiler_params=_cparams(2),
        name="ada_mod",
    )(c_pad, w_ada, b_ada.reshape(DEPTH, 1, n_out))


def _modnorm_to(x_ref, ln_ref, scale, shift, u_ref, rows, chunk=256):
    w = ln_ref[...] * (1.0 + scale)

    def body(r, carry):
        r0 = pl.multiple_of(r * chunk, chunk)
        xf = x_ref[pl.ds(r0, chunk), :]
        ms = jnp.mean(xf * xf, axis=-1, keepdims=True)
        y = xf * lax.rsqrt(ms + EPS)
        u_ref[pl.ds(r0, chunk), :] = (y * w + shift).astype(u_ref.dtype)
        return carry

    lax.fori_loop(0, rows // chunk, body, 0)


def _inproj_kernel(x_ref, mod_ref, ln_ref, w_ref, wdt_ref, proj_ref, dt_ref, u_ref, *, tm):
    j = pl.program_id(1)

    @pl.when(j == 0)
    def _():
        shift = mod_ref[0, 0:1, :]
        scale = mod_ref[0, 1:2, :]
        _modnorm_to(x_ref, ln_ref, scale, shift, u_ref, tm)
        dt_ref[...] = _dot(u_ref[...], wdt_ref[...])

    proj_ref[...] = _dot(u_ref[...], w_ref[...]).astype(proj_ref.dtype)


def _inproj_call(x2, mod_l, ln, w_main, w_dt, seq, tm=1024, tn=1024):
    t_total = x2.shape[0]
    tiles_per_batch = seq // tm
    return pl.pallas_call(
        functools.partial(_inproj_kernel, tm=tm),
        grid=(t_total // tm, N_MAIN // tn),
        in_specs=[
            pl.BlockSpec((tm, D_MODEL), lambda i, j: (i, 0)),
            pl.BlockSpec((1, MOD_ROWS, D_MODEL), lambda i, j: (i // tiles_per_batch, 0, 0)),
            pl.BlockSpec((1, D_MODEL), lambda i, j: (0, 0)),
            pl.BlockSpec((D_MODEL, tn), lambda i, j: (0, j)),
            pl.BlockSpec((D_MODEL, LANES), lambda i, j: (0, 0)),
        ],
        out_specs=[
            pl.BlockSpec((tm, tn), lambda i, j: (i, j)),
            pl.BlockSpec((tm, LANES), lambda i, j: (i, 0)),
        ],
        out_shape=[
            jax.ShapeDtypeStruct((t_total, N_MAIN), jnp.bfloat16),
            jax.ShapeDtypeStruct((t_total, LANES), jnp.float32),
        ],
        scratch_shapes=[pltpu.VMEM((tm, D_MODEL), jnp.bfloat16)],
        compiler_params=_cparams(2),
        name="in_proj",
    )(x2, mod_l, ln, w_main, w_dt)


def _dtprep_kernel(dt_ref, hp_ref, cum_ref, col_ref, betat_ref, *, tp):
    dt_bias = hp_ref[0:1, :]
    a_neg = -jnp.exp(hp_ref[1:2, :])
    cum_b = cum_ref[...]
    q = SCAN_CHUNK
    for c in range(tp // q):
        r0 = c * q
        dtr = dt_ref[r0:r0 + q, :] + dt_bias
        dt = jnp.maximum(dtr, 0.0) + jnp.log1p(jnp.exp(-jnp.abs(dtr)))
        a_parts = _split3(dt * a_neg)
        acum = _dot(cum_b, a_parts[0]) + _dot(cum_b, a_parts[1]) + _dot(cum_b, a_parts[2])
        beta = acum - jnp.log(jnp.maximum(dt, F32_TINY))
        parts = _split3(acum) + _split3(beta)
        for k, part in enumerate(parts):
            col_ref[r0:r0 + q, k * LANES:(k + 1) * LANES] = part
        betat_ref[:, r0:r0 + q] = beta.T[0:HEADS, :]


def _dtprep_call(dt_raw, hp, cum, tp=1024):
    t_total = dt_raw.shape[0]
    return pl.pallas_call(
        functools.partial(_dtprep_kernel, tp=tp),
        grid=(t_total // tp,),
        in_specs=[
            pl.BlockSpec((tp, LANES), lambda i: (i, 0)),
            pl.BlockSpec((SUBLANES, LANES), lambda i: (0, 0)),
            pl.BlockSpec((SCAN_CHUNK, SCAN_CHUNK), lambda i: (0, 0)),
        ],
        out_specs=[
            pl.BlockSpec((tp, COL_W), lambda i: (i, 0)),
            pl.BlockSpec((HEADS, tp), lambda i: (0, i)),
        ],
        out_shape=[
            jax.ShapeDtypeStruct((t_total, COL_W), jnp.bfloat16),
            jax.ShapeDtypeStruct((HEADS, t_total), jnp.float32),
        ],
        compiler_params=_cparams(1),
        name="dt_prep",
    )(dt_raw, hp, cum)


def _fill_conv_slabs(buf_ref, vals, width, tb, first):
    n_wrap = width - 1
    run = SLABS + n_wrap
    nc = tb // SCAN_CHUNK
    chans = vals.shape[-1]
    tail = buf_ref[(nc - 1) * run + run - n_wrap:(nc - 1) * run + run]
    tail = jnp.where(first, jnp.zeros_like(tail), tail)
    v3 = vals.reshape(tb // SUBLANES, SUBLANES, chans)
    for c in range(nc):
        cur = v3[c * SLABS:(c + 1) * SLABS]
        cur_tail = cur[SLABS - n_wrap:]
        wrapped = jnp.concatenate([tail[:, SUBLANES - 1:, :], cur_tail[:, :SUBLANES - 1, :]],
                                  axis=1)
        buf_ref[c * run:c * run + n_wrap] = wrapped
        buf_ref[c * run + n_wrap:(c + 1) * run] = cur
        tail = cur_tail


def _mixer_kernel(xs_ref, bs_ref, cs_ref, z_ref, cb_ref, cc_ref, cx_ref, col_ref, betat_ref,
                  cw_ref, cbias_ref, scw_ref, dskip_ref, nw_ref, e3_ref,
                  yssm_ref, yconv_ref,
                  raw_ref, sraw_ref, state_ref, *, tb):
    t = pl.program_id(2)
    q = SCAN_CHUNK
    nc = tb // q
    first = t == 0

    @pl.when(first)
    def _():
        state_ref[...] = jnp.zeros_like(state_ref)
        raw_ref[...] = jnp.zeros_like(raw_ref)
        sraw_ref[...] = jnp.zeros_like(sraw_ref)

    xbc_raw = jnp.concatenate([xs_ref[...].astype(jnp.float32),
                               bs_ref[...].astype(jnp.float32),
                               cs_ref[...].astype(jnp.float32)], axis=-1)
    _fill_conv_slabs(raw_ref, xbc_raw, SSM_CONV_WIDTH, tb, first)
    _fill_conv_slabs(sraw_ref, cc_ref[...].astype(jnp.float32) * cx_ref[...].astype(jnp.float32),
                     CONV_WIDTH, tb, first)

    cw = cw_ref[0]
    cbias = cbias_ref[0]
    scw = scw_ref[0]
    dskip = dskip_ref[0]
    nw = nw_ref[0]
    e3 = e3_ref[0]

    t_row = _chunk_time(lax.broadcasted_iota(jnp.int32, (q, q), 0))
    t_col = _chunk_time(lax.broadcasted_iota(jnp.int32, (q, q), 1))
    causal = t_row >= t_col
    nt_dims = (((1,), (1,)), ((), ()))
    tn_dims = (((0,), (0,)), ((), ()))
    run_s = SLABS + CONV_WIDTH - 1
    run_x = SLABS + SSM_CONV_WIDTH - 1

    for c in range(nc):
        r0 = c * q
        sconv = scw[0:1, :] * sraw_ref[c * run_s:c * run_s + SLABS]
        for k in range(1, CONV_WIDTH):
            sconv += scw[k:k + 1, :] * sraw_ref[c * run_s + k:c * run_s + k + SLABS]
        yconv_ref[r0:r0 + q, :] = (cb_ref[r0:r0 + q, :].astype(jnp.float32)
                                   * sconv.reshape(q, SHORT_PER_GROUP)).astype(yconv_ref.dtype)

        acc = cbias + cw[0:1, :] * raw_ref[c * run_x:c * run_x + SLABS]
        for k in range(1, SSM_CONV_WIDTH):
            acc += cw[k:k + 1, :] * raw_ref[c * run_x + k:c * run_x + k + SLABS]
        xbc = _silu(acc).reshape(q, GROUP_CONV)
        xs = xbc[:, 0:GROUP_X]
        b_f = xbc[:, GROUP_X:GROUP_X + STATE]
        c_f = xbc[:, GROUP_X + STATE:GROUP_CONV]
        x_b = xs.astype(jnp.bfloat16)

        col = col_ref[r0:r0 + q, :]
        acum_l = _dot(col[:, 0:N_SPLIT * LANES], e3)
        beta_l = _dot(col[:, N_SPLIT * LANES:], e3)
        beta_t = betat_ref[0, :, r0:r0 + q]
        acum_last = acum_l[q - 1:q, :]

        scores = lax.dot_general(c_f.astype(jnp.bfloat16), b_f.astype(jnp.bfloat16), nt_dims,
                                 preferred_element_type=jnp.float32)
        ys = []
        for r in range(HEADS_PER_GROUP):
            al = acum_l[:, r * LANES:(r + 1) * LANES]
            bl = beta_l[:, r * LANES:(r + 1) * LANES]
            last = acum_last[:, r * LANES:(r + 1) * LANES]
            w_r = (scores * jnp.exp(jnp.where(causal, al - beta_t[r:r + 1, :], -jnp.inf))
                   ).astype(jnp.bfloat16)
            c_r = (c_f * jnp.exp(al)).astype(jnp.bfloat16)
            b_r = (b_f * jnp.exp(last - bl)).astype(jnp.bfloat16)
            x_r = x_b[:, r * HEAD_DIM:(r + 1) * HEAD_DIM]
            s_r = state_ref[r]
            ys.append(_dot(w_r, x_r) + _dot(c_r, s_r.astype(jnp.bfloat16)))
            state_ref[r] = (s_r * jnp.exp(last[:, 0:HEAD_DIM])
                            + lax.dot_general(b_r, x_r, tn_dims,
                                              preferred_element_type=jnp.float32))
        y = jnp.concatenate(ys, axis=-1)

        y = y + dskip * xs
        y = y * _silu(z_ref[r0:r0 + q, :].astype(jnp.float32))
        y = y * lax.rsqrt(jnp.mean(y * y, axis=-1, keepdims=True) + EPS)
        yssm_ref[r0:r0 + q, :] = (y * nw).astype(yssm_ref.dtype)


def _mixer_call(proj, col, beta_t, cw, cbias, scw, dskip_e, nw, e3, bsz, seq, tb=512):
    t_total = proj.shape[0]
    nt = seq // tb
    nc = tb // SCAN_CHUNK
    rowmap = lambda b, g, t: b * nt + t

    def colspec(width, off):
        base = off // width
        return pl.BlockSpec((tb, width), lambda b, g, t: (rowmap(b, g, t), base + g))

    def gspec(shape):
        return pl.BlockSpec((1,) + shape, lambda b, g, t: (g, 0, 0))

    return pl.pallas_call(
        functools.partial(_mixer_kernel, tb=tb),
        grid=(bsz, GROUPS, nt),
        in_specs=[
            colspec(GROUP_X, OFF_XS),
            colspec(STATE, OFF_BS),
            colspec(STATE, OFF_CS),
            colspec(GROUP_X, OFF_Z),
            colspec(SHORT_PER_GROUP, OFF_CB),
            colspec(SHORT_PER_GROUP, OFF_CC),
            colspec(SHORT_PER_GROUP, OFF_CX),
            pl.BlockSpec((tb, COL_W), lambda b, g, t: (rowmap(b, g, t), 0)),
            pl.BlockSpec((1, HEADS_PER_GROUP, tb), lambda b, g, t: (g, 0, rowmap(b, g, t))),
            gspec((SSM_CONV_WIDTH, GROUP_CONV)),
            gspec((1, GROUP_CONV)),
            gspec((CONV_WIDTH, SHORT_PER_GROUP)),
            gspec((1, GROUP_X)),
            gspec((1, GROUP_X)),
            gspec((N_SPLIT * LANES, HEADS_PER_GROUP * LANES)),
        ],
        out_specs=[
            pl.BlockSpec((tb, GROUP_X), lambda b, g, t: (rowmap(b, g, t), g)),
            pl.BlockSpec((tb, SHORT_PER_GROUP), lambda b, g, t: (rowmap(b, g, t), g)),
        ],
        out_shape=[
            jax.ShapeDtypeStruct((t_total, D_SSM), jnp.bfloat16),
            jax.ShapeDtypeStruct((t_total, CONV_DIM), jnp.bfloat16),
        ],
        scratch_shapes=[
            pltpu.VMEM((nc * (SLABS + SSM_CONV_WIDTH - 1), SUBLANES, GROUP_CONV), jnp.float32),
            pltpu.VMEM((nc * (SLABS + CONV_WIDTH - 1), SUBLANES, SHORT_PER_GROUP), jnp.float32),
            pltpu.VMEM((HEADS_PER_GROUP, STATE, HEAD_DIM), jnp.float32),
        ],
        compiler_params=_cparams(3),
        name="mixer",
    )(proj, proj, proj, proj, proj, proj, proj, col, beta_t,
      cw, cbias, scw, dskip_e, nw, e3)


def _out_kernel(yc_ref, ys_ref, gl_ref, x_ref, mod_ref, wc_ref, ws_ref, wo_ref, o_ref):
    p_conv = _dot(yc_ref[...], wc_ref[...])
    p_ssm = _dot(ys_ref[...], ws_ref[...])
    gl = gl_ref[...].astype(jnp.float32)
    g = 1.0 / (1.0 + jnp.exp(-gl))
    merged = g[:, :D_MODEL] * p_conv + g[:, D_MODEL:] * p_ssm
    mix = _dot(merged.astype(jnp.bfloat16), wo_ref[...])
    gate1 = mod_ref[0, 2:3, :]
    o_ref[...] = x_ref[...] + gate1 * mix


def _out_call(y_conv, y_ssm, proj, x2, mod_l, wc, ws, wo, seq, tm=512):
    t_total = x2.shape[0]
    tiles_per_batch = seq // tm
    return pl.pallas_call(
        _out_kernel,
        grid=(t_total // tm,),
        in_specs=[
            pl.BlockSpec((tm, CONV_DIM), lambda i: (i, 0)),
            pl.BlockSpec((tm, D_SSM), lambda i: (i, 0)),
            pl.BlockSpec((tm, 2 * D_MODEL), lambda i: (i, 0)),
            pl.BlockSpec((tm, D_MODEL), lambda i: (i, 0)),
            pl.BlockSpec((1, MOD_ROWS, D_MODEL), lambda i: (i // tiles_per_batch, 0, 0)),
            pl.BlockSpec((CONV_DIM, D_MODEL), lambda i: (0, 0)),
            pl.BlockSpec((D_SSM, D_MODEL), lambda i: (0, 0)),
            pl.BlockSpec((D_MODEL, D_MODEL), lambda i: (0, 0)),
        ],
        out_specs=pl.BlockSpec((tm, D_MODEL), lambda i: (i, 0)),
        out_shape=jax.ShapeDtypeStruct((t_total, D_MODEL), jnp.float32),
        compiler_params=_cparams(1),
        name="mix_out",
    )(y_conv, y_ssm, proj, x2, mod_l, wc, ws, wo)


def _mlp_kernel(x_ref, mod_ref, ln_ref, wup_ref, wdn_ref, fn_ref, o_ref, u_ref, acc_ref,
                *, tm, final):
    j = pl.program_id(1)

    @pl.when(j == 0)
    def _():
        shift = mod_ref[0, 3:4, :]
        scale = mod_ref[0, 4:5, :]
        _modnorm_to(x_ref, ln_ref, scale, shift, u_ref, tm)
        acc_ref[...] = jnp.zeros_like(acc_ref)

    h = jnp.maximum(_dot(u_ref[...], wup_ref[...]), 0.0)
    acc_ref[...] += _dot((h * h).astype(jnp.bfloat16), wdn_ref[...])

    @pl.when(j == pl.num_programs(1) - 1)
    def _():
        gate2 = mod_ref[0, 5:6, :]
        xn = x_ref[...] + gate2 * acc_ref[...]
        if final:
            ms = jnp.mean(xn * xn, axis=-1, keepdims=True)
            xn = xn * lax.rsqrt(ms + EPS) * fn_ref[...]
        o_ref[...] = xn


def _mlp_call(x2, mod_l, ln, w_up, w_dn, fnorm, seq, final, tm=1024, tf=1024):
    t_total = x2.shape[0]
    tiles_per_batch = seq // tm
    return pl.pallas_call(
        functools.partial(_mlp_kernel, tm=tm, final=final),
        grid=(t_total // tm, D_FF // tf),
        in_specs=[
            pl.BlockSpec((tm, D_MODEL), lambda i, j: (i, 0)),
            pl.BlockSpec((1, MOD_ROWS, D_MODEL), lambda i, j: (i // tiles_per_batch, 0, 0)),
            pl.BlockSpec((1, D_MODEL), lambda i, j: (0, 0)),
            pl.BlockSpec((D_MODEL, tf), lambda i, j: (0, j)),
            pl.BlockSpec((tf, D_MODEL), lambda i, j: (j, 0)),
            pl.BlockSpec((1, D_MODEL), lambda i, j: (0, 0)),
        ],
        out_specs=pl.BlockSpec((tm, D_MODEL), lambda i, j: (i, 0)),
        out_shape=jax.ShapeDtypeStruct((t_total, D_MODEL), jnp.float32),
        scratch_shapes=[pltpu.VMEM((tm, D_MODEL), jnp.bfloat16),
                        pltpu.VMEM((tm, D_MODEL), jnp.float32)],
        compiler_params=_cparams(2),
        name="mlp",
    )(x2, mod_l, ln, w_up, w_dn, fnorm)


def _head_constants():
    e3 = np.zeros((GROUPS, N_SPLIT, LANES, HEADS_PER_GROUP * LANES), np.float32)
    for g in range(GROUPS):
        for r in range(HEADS_PER_GROUP):
            e3[g, :, g * HEADS_PER_GROUP + r, r * LANES:(r + 1) * LANES] = 1.0
    e3 = e3.reshape(GROUPS, N_SPLIT * LANES, HEADS_PER_GROUP * LANES)
    tm = _chunk_time(np.arange(SCAN_CHUNK))
    cum = (tm[None, :] <= tm[:, None]).astype(np.float32)
    return jnp.asarray(e3, jnp.bfloat16), jnp.asarray(cum, jnp.bfloat16)


def _group_cols(w_l):
    k = w_l.shape[0]
    xs = w_l[:, :D_SSM].reshape(k, GROUPS, GROUP_X)
    bs = w_l[:, D_SSM:D_SSM + GROUPS * STATE].reshape(k, GROUPS, STATE)
    cs = w_l[:, D_SSM + GROUPS * STATE:].reshape(k, GROUPS, STATE)
    return jnp.transpose(jnp.concatenate([xs, bs, cs], axis=-1), (1, 0, 2))


def _permute_tokens(x, bsz, seq, inverse=False):
    a, b = (SLABS, SUBLANES) if inverse else (SUBLANES, SLABS)
    x5 = x.reshape(bsz, seq // SCAN_CHUNK, a, b, D_MODEL)
    return jnp.swapaxes(x5, 2, 3).reshape(bsz * seq, D_MODEL)


def kernel(x, c, w_ada, b_ada, ln1, ln2, w_in, conv_w, ssm_conv_w, ssm_conv_b, dt_bias, a_log,
           d_skip, ssm_norm_w, w_conv_out, w_ssm_out, w_o, w_up, w_down, final_norm):
    bsz, seq, _ = x.shape
    bf = jnp.bfloat16
    f32 = jnp.float32

    c_pad = jnp.zeros((SUBLANES, D_MODEL), f32).at[:bsz].set(c)
    mod_all = _ada_call(c_pad, w_ada, b_ada)
    mod_all = mod_all[:, :bsz].reshape(DEPTH, bsz, N_MOD, D_MODEL)
    mod_all = jnp.pad(mod_all, ((0, 0), (0, 0), (0, MOD_ROWS - N_MOD), (0, 0)))

    e3, cum = _head_constants()
    fnorm = final_norm.reshape(1, D_MODEL)
    x2 = _permute_tokens(x, bsz, seq)

    for l in range(DEPTH):
        mod_l = mod_all[l]
        w_main = w_in[l, :, :N_MAIN].astype(bf)
        w_dt = jnp.pad(w_in[l, :, N_MAIN:], ((0, 0), (0, LANES - HEADS))).astype(bf)
        proj, dt_raw = _inproj_call(x2, mod_l, ln1[l].reshape(1, D_MODEL), w_main, w_dt, seq)

        hp = jnp.zeros((SUBLANES, LANES), f32)
        hp = hp.at[0, :HEADS].set(dt_bias[l]).at[1, :HEADS].set(a_log[l])
        col, beta_t = _dtprep_call(dt_raw, hp, cum)
        beta_t = beta_t.reshape(GROUPS, HEADS_PER_GROUP, bsz * seq)

        cw = _group_cols(ssm_conv_w[l])
        cbias = _group_cols(ssm_conv_b[l].reshape(1, -1))
        scw = jnp.transpose(conv_w[l].reshape(CONV_WIDTH, GROUPS, SHORT_PER_GROUP), (1, 0, 2))
        dskip_e = jnp.repeat(d_skip[l], HEAD_DIM).reshape(GROUPS, 1, GROUP_X)
        nw = ssm_norm_w[l].reshape(GROUPS, 1, GROUP_X)
        y_ssm, y_conv = _mixer_call(proj, col, beta_t, cw, cbias, scw, dskip_e, nw, e3,
                                    bsz, seq)

        x2 = _out_call(y_conv, y_ssm, proj, x2, mod_l, w_conv_out[l].astype(bf),
                       w_ssm_out[l].astype(bf), w_o[l].astype(bf), seq)
        x2 = _mlp_call(x2, mod_l, ln2[l].reshape(1, D_MODEL), w_up[l].astype(bf),
                       w_down[l].astype(bf), fnorm, seq, final=(l == DEPTH - 1))
    return _permute_tokens(x2, bsz, seq, inverse=True).reshape(bsz, seq, D_MODEL)
```

```python
import functools

import numpy as np
import jax
import jax.numpy as jnp
from jax import lax
from jax.experimental import pallas as pl
from jax.experimental.pallas import tpu as pltpu

D_MODEL = 1024
DEPTH = 4
EPS = 1e-6
N_MOD = 6
MOD_ROWS = 8
CONV_DIM = D_MODEL
CONV_WIDTH = 3
D_SSM = 2 * D_MODEL
HEAD_DIM = 64
HEADS = D_SSM // HEAD_DIM
GROUPS = 8
HEADS_PER_GROUP = HEADS // GROUPS
GROUP_X = HEADS_PER_GROUP * HEAD_DIM
STATE = 128
SSM_CONV_WIDTH = 4
D_FF = 4 * D_MODEL
LANES = 128
SUBLANES = 8
F32_TINY = float(np.finfo(np.float32).tiny)
LOG2E = float(np.log2(np.e))

OFF_GATE = 0
OFF_CB = 2 * D_MODEL
OFF_CC = OFF_CB + CONV_DIM
OFF_CX = OFF_CC + CONV_DIM
OFF_Z = OFF_CX + CONV_DIM
OFF_XS = OFF_Z + D_SSM
OFF_BS = OFF_XS + D_SSM
OFF_CS = OFF_BS + GROUPS * STATE
N_MAIN = OFF_CS + GROUPS * STATE

SCAN_CHUNK = 128
SLABS = SCAN_CHUNK // SUBLANES
GROUP_CONV = GROUP_X + 2 * STATE
SHORT_PER_GROUP = CONV_DIM // GROUPS
N_SPLIT = 3
PART_ROWS = 32

VMEM_LIMIT = 48 * 1024 * 1024


def _cparams(n_axes):
    return pltpu.CompilerParams(
        dimension_semantics=("arbitrary",) * n_axes,
        vmem_limit_bytes=VMEM_LIMIT)


def _dot(a, b):
    return jnp.dot(a, b, preferred_element_type=jnp.float32)


def _split3(v):
    hi = v.astype(jnp.bfloat16)
    r1 = v - hi.astype(jnp.float32)
    mid = r1.astype(jnp.bfloat16)
    lo = (r1 - mid.astype(jnp.float32)).astype(jnp.bfloat16)
    return hi, mid, lo


def _silu(v):
    return v / (1.0 + jnp.exp(-v))


def _chunk_time(p):
    return (p % SUBLANES) * SLABS + p // SUBLANES


def _ada_kernel(c_ref, w_ref, b_ref, o_ref):
    c = c_ref[...]
    o_ref[0] = _dot(_silu(c), w_ref[0]) + b_ref[0]


def _ada_call(c_pad, w_ada, b_ada):
    n_out = N_MOD * D_MODEL
    tn = D_MODEL
    return pl.pallas_call(
        _ada_kernel,
        grid=(DEPTH, n_out // tn),
        in_specs=[
            pl.BlockSpec((SUBLANES, D_MODEL), lambda l, j: (0, 0)),
            pl.BlockSpec((1, D_MODEL, tn), lambda l, j: (l, 0, j)),
            pl.BlockSpec((1, 1, tn), lambda l, j: (l, 0, j)),
        ],
        out_specs=pl.BlockSpec((1, SUBLANES, tn), lambda l, j: (l, 0, j)),
        out_shape=jax.ShapeDtypeStruct((DEPTH, SUBLANES, n_out), jnp.float32),
        compiler_params=_cparams(2),
        name="ada_mod",
    )(c_pad, w_ada, b_ada.reshape(DEPTH, 1, n_out))


def _modnorm_to(x_ref, ln_ref, scale, shift, u_ref, rows, chunk=256):
    w = ln_ref[...] * (1.0 + scale)

    def body(r, carry):
        r0 = pl.multiple_of(r * chunk, chunk)
        xf = x_ref[pl.ds(r0, chunk), :]
        ms = jnp.mean(xf * xf, axis=-1, keepdims=True)
        y = xf * lax.rsqrt(ms + EPS)
        u_ref[pl.ds(r0, chunk), :] = (y * w + shift).astype(u_ref.dtype)
        return carry

    lax.fori_loop(0, rows // chunk, body, 0)


def _inproj_kernel(x_ref, mod_ref, ln_ref, w_ref, wdt_ref, proj_ref, dt_ref, u_ref, *, tm):
    j = pl.program_id(1)

    @pl.when(j == 0)
    def _():
        shift = mod_ref[0, 0:1, :]
        scale = mod_ref[0, 1:2, :]
        _modnorm_to(x_ref, ln_ref, scale, shift, u_ref, tm)
        dt_ref[...] = _dot(u_ref[...], wdt_ref[...])

    proj_ref[...] = _dot(u_ref[...], w_ref[...]).astype(proj_ref.dtype)


def _inproj_call(x2, mod_l, ln, w_main, w_dt, seq, tm=1024, tn=1024):
    t_total = x2.shape[0]
    tiles_per_batch = seq // tm
    return pl.pallas_call(
        functools.partial(_inproj_kernel, tm=tm),
        grid=(t_total // tm, N_MAIN // tn),
        in_specs=[
            pl.BlockSpec((tm, D_MODEL), lambda i, j: (i, 0)),
            pl.BlockSpec((1, MOD_ROWS, D_MODEL), lambda i, j: (i // tiles_per_batch, 0, 0)),
            pl.BlockSpec((1, D_MODEL), lambda i, j: (0, 0)),
            pl.BlockSpec((D_MODEL, tn), lambda i, j: (0, j)),
            pl.BlockSpec((D_MODEL, LANES), lambda i, j: (0, 0)),
        ],
        out_specs=[
            pl.BlockSpec((tm, tn), lambda i, j: (i, j)),
            pl.BlockSpec((tm, LANES), lambda i, j: (i, 0)),
        ],
        out_shape=[
            jax.ShapeDtypeStruct((t_total, N_MAIN), jnp.bfloat16),
            jax.ShapeDtypeStruct((t_total, LANES), jnp.float32),
        ],
        scratch_shapes=[pltpu.VMEM((tm, D_MODEL), jnp.bfloat16)],
        compiler_params=_cparams(2),
        name="in_proj",
    )(x2, mod_l, ln, w_main, w_dt)


def _dtprep_kernel(dt_ref, hp_ref, cum_ref, parts_ref, betat_ref, *, tp):
    dt_bias = hp_ref[0:1, :]
    a_neg = -jnp.exp(hp_ref[1:2, :])
    cum_b = cum_ref[...]
    q = SCAN_CHUNK
    for c in range(tp // q):
        r0 = c * q
        dtr = dt_ref[r0:r0 + q, :] + dt_bias
        dt = jnp.maximum(dtr, 0.0) + jnp.log1p(jnp.exp(-jnp.abs(dtr)))
        a_parts = _split3(dt * a_neg)
        acum = _dot(cum_b, a_parts[0]) + _dot(cum_b, a_parts[1]) + _dot(cum_b, a_parts[2])
        beta = acum - jnp.log(jnp.maximum(dt, F32_TINY))
        acum_t = (acum * LOG2E).T[0:HEADS, :]
        beta_t = (beta * LOG2E).T[0:HEADS, :]
        for k, part in enumerate(_split3(acum_t) + _split3(beta_t)):
            parts_ref[k, :, r0:r0 + q] = part
        betat_ref[:, r0:r0 + q] = beta_t


def _dtprep_call(dt_raw, hp, cum, tp=1024):
    t_total = dt_raw.shape[0]
    return pl.pallas_call(
        functools.partial(_dtprep_kernel, tp=tp),
        grid=(t_total // tp,),
        in_specs=[
            pl.BlockSpec((tp, LANES), lambda i: (i, 0)),
            pl.BlockSpec((SUBLANES, LANES), lambda i: (0, 0)),
            pl.BlockSpec((SCAN_CHUNK, SCAN_CHUNK), lambda i: (0, 0)),
        ],
        out_specs=[
            pl.BlockSpec((2 * N_SPLIT, HEADS, tp), lambda i: (0, 0, i)),
            pl.BlockSpec((HEADS, tp), lambda i: (0, i)),
        ],
        out_shape=[
            jax.ShapeDtypeStruct((2 * N_SPLIT, HEADS, t_total), jnp.bfloat16),
            jax.ShapeDtypeStruct((HEADS, t_total), jnp.float32),
        ],
        compiler_params=_cparams(1),
        name="dt_prep",
    )(dt_raw, hp, cum)


def _fill_conv_slabs(buf_ref, vals, width, tb, first):
    n_wrap = width - 1
    run = SLABS + n_wrap
    nc = tb // SCAN_CHUNK
    chans = vals.shape[-1]
    tail = buf_ref[(nc - 1) * run + run - n_wrap:(nc - 1) * run + run]
    tail = jnp.where(first, jnp.zeros_like(tail), tail)
    v3 = vals.reshape(tb // SUBLANES, SUBLANES, chans)
    for c in range(nc):
        cur = v3[c * SLABS:(c + 1) * SLABS]
        cur_tail = cur[SLABS - n_wrap:]
        wrapped = jnp.concatenate([tail[:, SUBLANES - 1:, :], cur_tail[:, :SUBLANES - 1, :]],
                                  axis=1)
        buf_ref[c * run:c * run + n_wrap] = wrapped
        buf_ref[c * run + n_wrap:(c + 1) * run] = cur
        tail = cur_tail


def _mixer_kernel(xs_ref, bs_ref, cs_ref, z_ref, cb_ref, cc_ref, cx_ref, parts_ref, betat_ref,
                  cw_ref, cbias_ref, scw_ref, dskip_ref, nw_ref, et_ref,
                  yssm_ref, yconv_ref,
                  raw_ref, sraw_ref, state_ref, *, tb):
    t = pl.program_id(2)
    q = SCAN_CHUNK
    nc = tb // q
    first = t == 0

    @pl.when(first)
    def _():
        state_ref[...] = jnp.zeros_like(state_ref)
        raw_ref[...] = jnp.zeros_like(raw_ref)
        sraw_ref[...] = jnp.zeros_like(sraw_ref)

    xbc_raw = jnp.concatenate([xs_ref[...].astype(jnp.float32),
                               bs_ref[...].astype(jnp.float32),
                               cs_ref[...].astype(jnp.float32)], axis=-1)
    _fill_conv_slabs(raw_ref, xbc_raw, SSM_CONV_WIDTH, tb, first)
    _fill_conv_slabs(sraw_ref, cc_ref[...].astype(jnp.float32) * cx_ref[...].astype(jnp.float32),
                     CONV_WIDTH, tb, first)

    cw = cw_ref[0]
    cbias = cbias_ref[0]
    scw = scw_ref[0]
    dskip = dskip_ref[0]
    nw = nw_ref[0]
    e_t = et_ref[...]

    t_row = _chunk_time(lax.broadcasted_iota(jnp.int32, (q, q), 0))
    t_col = _chunk_time(lax.broadcasted_iota(jnp.int32, (q, q), 1))
    causal = t_row >= t_col
    nt_dims = (((1,), (1,)), ((), ()))
    tn_dims = (((0,), (0,)), ((), ()))
    run_s = SLABS + CONV_WIDTH - 1
    run_x = SLABS + SSM_CONV_WIDTH - 1

    for c in range(nc):
        r0 = c * q
        sconv = scw[0:1, :] * sraw_ref[c * run_s:c * run_s + SLABS]
        for k in range(1, CONV_WIDTH):
            sconv += scw[k:k + 1, :] * sraw_ref[c * run_s + k:c * run_s + k + SLABS]
        yconv_ref[r0:r0 + q, :] = (cb_ref[r0:r0 + q, :].astype(jnp.float32)
                                   * sconv.reshape(q, SHORT_PER_GROUP)).astype(yconv_ref.dtype)

        acc = cbias + cw[0:1, :] * raw_ref[c * run_x:c * run_x + SLABS]
        for k in range(1, SSM_CONV_WIDTH):
            acc += cw[k:k + 1, :] * raw_ref[c * run_x + k:c * run_x + k + SLABS]
        xbc = _silu(acc).reshape(q, GROUP_CONV)
        xs = xbc[:, 0:GROUP_X]
        b_f = xbc[:, GROUP_X:GROUP_X + STATE]
        c_f = xbc[:, GROUP_X + STATE:GROUP_CONV]
        x_b = xs.astype(jnp.bfloat16)

        ab_l = lax.dot_general(parts_ref[0, :, r0:r0 + q], e_t, tn_dims,
                               preferred_element_type=jnp.float32)
        acum_l = ab_l[:, 0:HEADS_PER_GROUP * LANES]
        beta_l = ab_l[:, HEADS_PER_GROUP * LANES:]
        beta_t = betat_ref[0, :, r0:r0 + q]
        acum_last = acum_l[q - 1:q, :]

        c_b = c_f.astype(jnp.bfloat16)
        b_b = b_f.astype(jnp.bfloat16)
        scores_b = lax.dot_general(c_b, b_b, nt_dims, preferred_element_type=jnp.float32
                                   ).astype(jnp.bfloat16)
        ys = []
        for r in range(HEADS_PER_GROUP):
            al = acum_l[:, r * LANES:(r + 1) * LANES]
            bl = beta_l[:, r * LANES:(r + 1) * LANES]
            last = acum_last[:, r * LANES:(r + 1) * LANES]
            seg = (al - beta_t[r:r + 1, :]).astype(jnp.bfloat16)
            w_r = scores_b * jnp.exp2(jnp.where(causal, seg, -jnp.inf))
            c_r = c_b * jnp.exp2(al).astype(jnp.bfloat16)
            b_r = b_b * jnp.exp2(last - bl).astype(jnp.bfloat16)
            x_r = x_b[:, r * HEAD_DIM:(r + 1) * HEAD_DIM]
            s_r = state_ref[r]
            ys.append(_dot(w_r, x_r) + _dot(c_r, s_r.astype(jnp.bfloat16)))
            state_ref[r] = (s_r * jnp.exp2(last[:, 0:HEAD_DIM])
                            + lax.dot_general(b_r, x_r, tn_dims,
                                              preferred_element_type=jnp.float32))
        y = jnp.concatenate(ys, axis=-1)

        y = y + dskip * xs
        y = y * _silu(z_ref[r0:r0 + q, :].astype(jnp.float32))
        y = y * lax.rsqrt(jnp.mean(y * y, axis=-1, keepdims=True) + EPS)
        yssm_ref[r0:r0 + q, :] = (y * nw).astype(yssm_ref.dtype)


def _mixer_call(proj, parts, beta_t, cw, cbias, scw, dskip_e, nw, e_t, bsz, seq, tb=1024):
    t_total = proj.shape[0]
    nt = seq // tb
    nc = tb // SCAN_CHUNK
    rowmap = lambda b, g, t: b * nt + t

    def colspec(width, off):
        base = off // width
        return pl.BlockSpec((tb, width), lambda b, g, t: (rowmap(b, g, t), base + g))

    def gspec(shape):
        return pl.BlockSpec((1,) + shape, lambda b, g, t: (g, 0, 0))

    return pl.pallas_call(
        functools.partial(_mixer_kernel, tb=tb),
        grid=(bsz, GROUPS, nt),
        in_specs=[
            colspec(GROUP_X, OFF_XS),
            colspec(STATE, OFF_BS),
            colspec(STATE, OFF_CS),
            colspec(GROUP_X, OFF_Z),
            colspec(SHORT_PER_GROUP, OFF_CB),
            colspec(SHORT_PER_GROUP, OFF_CC),
            colspec(SHORT_PER_GROUP, OFF_CX),
            pl.BlockSpec((1, PART_ROWS, tb), lambda b, g, t: (g, 0, rowmap(b, g, t))),
            pl.BlockSpec((1, HEADS_PER_GROUP, tb), lambda b, g, t: (g, 0, rowmap(b, g, t))),
            gspec((SSM_CONV_WIDTH, GROUP_CONV)),
            gspec((1, GROUP_CONV)),
            gspec((CONV_WIDTH, SHORT_PER_GROUP)),
            gspec((1, GROUP_X)),
            gspec((1, GROUP_X)),
            pl.BlockSpec((PART_ROWS, 2 * HEADS_PER_GROUP * LANES), lambda b, g, t: (0, 0)),
        ],
        out_specs=[
            pl.BlockSpec((tb, GROUP_X), lambda b, g, t: (rowmap(b, g, t), g)),
            pl.BlockSpec((tb, SHORT_PER_GROUP), lambda b, g, t: (rowmap(b, g, t), g)),
        ],
        out_shape=[
            jax.ShapeDtypeStruct((t_total, D_SSM), jnp.bfloat16),
            jax.ShapeDtypeStruct((t_total, CONV_DIM), jnp.bfloat16),
        ],
        scratch_shapes=[
            pltpu.VMEM((nc * (SLABS + SSM_CONV_WIDTH - 1), SUBLANES, GROUP_CONV), jnp.float32),
            pltpu.VMEM((nc * (SLABS + CONV_WIDTH - 1), SUBLANES, SHORT_PER_GROUP), jnp.float32),
            pltpu.VMEM((HEADS_PER_GROUP, STATE, HEAD_DIM), jnp.float32),
        ],
        compiler_params=_cparams(3),
        name="mixer",
    )(proj, proj, proj, proj, proj, proj, proj, parts, beta_t,
      cw, cbias, scw, dskip_e, nw, e_t)


def _out_kernel(yc_ref, ys_ref, gl_ref, x_ref, mod_ref, wc_ref, ws_ref, wo_ref, o_ref):
    p_conv = _dot(yc_ref[...], wc_ref[...])
    p_ssm = _dot(ys_ref[...], ws_ref[...])
    gl = gl_ref[...].astype(jnp.float32)
    g = 1.0 / (1.0 + jnp.exp(-gl))
    merged = g[:, :D_MODEL] * p_conv + g[:, D_MODEL:] * p_ssm
    mix = _dot(merged.astype(jnp.bfloat16), wo_ref[...])
    gate1 = mod_ref[0, 2:3, :]
    o_ref[...] = x_ref[...] + gate1 * mix


def _out_call(y_conv, y_ssm, proj, x2, mod_l, wc, ws, wo, seq, tm=512):
    t_total = x2.shape[0]
    tiles_per_batch = seq // tm
    return pl.pallas_call(
        _out_kernel,
        grid=(t_total // tm,),
        in_specs=[
            pl.BlockSpec((tm, CONV_DIM), lambda i: (i, 0)),
            pl.BlockSpec((tm, D_SSM), lambda i: (i, 0)),
            pl.BlockSpec((tm, 2 * D_MODEL), lambda i: (i, 0)),
            pl.BlockSpec((tm, D_MODEL), lambda i: (i, 0)),
            pl.BlockSpec((1, MOD_ROWS, D_MODEL), lambda i: (i // tiles_per_batch, 0, 0)),
            pl.BlockSpec((CONV_DIM, D_MODEL), lambda i: (0, 0)),
            pl.BlockSpec((D_SSM, D_MODEL), lambda i: (0, 0)),
            pl.BlockSpec((D_MODEL, D_MODEL), lambda i: (0, 0)),
        ],
        out_specs=pl.BlockSpec((tm, D_MODEL), lambda i: (i, 0)),
        out_shape=jax.ShapeDtypeStruct((t_total, D_MODEL), jnp.float32),
        compiler_params=_cparams(1),
        name="mix_out",
    )(y_conv, y_ssm, proj, x2, mod_l, wc, ws, wo)


def _mlp_kernel(x_ref, mod_ref, ln_ref, wup_ref, wdn_ref, fn_ref, o_ref, u_ref, acc_ref,
                *, tm, final):
    j = pl.program_id(1)

    @pl.when(j == 0)
    def _():
        shift = mod_ref[0, 3:4, :]
        scale = mod_ref[0, 4:5, :]
        _modnorm_to(x_ref, ln_ref, scale, shift, u_ref, tm)
        acc_ref[...] = jnp.zeros_like(acc_ref)

    h = jnp.maximum(_dot(u_ref[...], wup_ref[...]), 0.0)
    acc_ref[...] += _dot((h * h).astype(jnp.bfloat16), wdn_ref[...])

    @pl.when(j == pl.num_programs(1) - 1)
    def _():
        gate2 = mod_ref[0, 5:6, :]
        xn = x_ref[...] + gate2 * acc_ref[...]
        if final:
            ms = jnp.mean(xn * xn, axis=-1, keepdims=True)
            xn = xn * lax.rsqrt(ms + EPS) * fn_ref[...]
        o_ref[...] = xn


def _mlp_call(x2, mod_l, ln, w_up, w_dn, fnorm, seq, final, tm=1024, tf=1024):
    t_total = x2.shape[0]
    tiles_per_batch = seq // tm
    return pl.pallas_call(
        functools.partial(_mlp_kernel, tm=tm, final=final),
        grid=(t_total // tm, D_FF // tf),
        in_specs=[
            pl.BlockSpec((tm, D_MODEL), lambda i, j: (i, 0)),
            pl.BlockSpec((1, MOD_ROWS, D_MODEL), lambda i, j: (i // tiles_per_batch, 0, 0)),
            pl.BlockSpec((1, D_MODEL), lambda i, j: (0, 0)),
            pl.BlockSpec((D_MODEL, tf), lambda i, j: (0, j)),
            pl.BlockSpec((tf, D_MODEL), lambda i, j: (j, 0)),
            pl.BlockSpec((1, D_MODEL), lambda i, j: (0, 0)),
        ],
        out_specs=pl.BlockSpec((tm, D_MODEL), lambda i, j: (i, 0)),
        out_shape=jax.ShapeDtypeStruct((t_total, D_MODEL), jnp.float32),
        scratch_shapes=[pltpu.VMEM((tm, D_MODEL), jnp.bfloat16),
                        pltpu.VMEM((tm, D_MODEL), jnp.float32)],
        compiler_params=_cparams(2),
        name="mlp",
    )(x2, mod_l, ln, w_up, w_dn, fnorm)


def _head_constants():
    e_t = np.zeros((PART_ROWS, 2, HEADS_PER_GROUP, LANES), np.float32)
    for k in range(2 * N_SPLIT):
        for r in range(HEADS_PER_GROUP):
            e_t[k * HEADS_PER_GROUP + r, k // N_SPLIT, r, :] = 1.0
    e_t = e_t.reshape(PART_ROWS, 2 * HEADS_PER_GROUP * LANES)
    tm = _chunk_time(np.arange(SCAN_CHUNK))
    cum = (tm[None, :] <= tm[:, None]).astype(np.float32)
    return jnp.asarray(e_t, jnp.bfloat16), jnp.asarray(cum, jnp.bfloat16)


def _group_cols(w_l):
    k = w_l.shape[0]
    xs = w_l[:, :D_SSM].reshape(k, GROUPS, GROUP_X)
    bs = w_l[:, D_SSM:D_SSM + GROUPS * STATE].reshape(k, GROUPS, STATE)
    cs = w_l[:, D_SSM + GROUPS * STATE:].reshape(k, GROUPS, STATE)
    return jnp.transpose(jnp.concatenate([xs, bs, cs], axis=-1), (1, 0, 2))


def _permute_tokens(x, bsz, seq, inverse=False):
    a, b = (SLABS, SUBLANES) if inverse else (SUBLANES, SLABS)
    x5 = x.reshape(bsz, seq // SCAN_CHUNK, a, b, D_MODEL)
    return jnp.swapaxes(x5, 2, 3).reshape(bsz * seq, D_MODEL)


def kernel(x, c, w_ada, b_ada, ln1, ln2, w_in, conv_w, ssm_conv_w, ssm_conv_b, dt_bias, a_log,
           d_skip, ssm_norm_w, w_conv_out, w_ssm_out, w_o, w_up, w_down, final_norm):
    bsz, seq, _ = x.shape
    bf = jnp.bfloat16
    f32 = jnp.float32

    c_pad = jnp.zeros((SUBLANES, D_MODEL), f32).at[:bsz].set(c)
    mod_all = _ada_call(c_pad, w_ada, b_ada)
    mod_all = mod_all[:, :bsz].reshape(DEPTH, bsz, N_MOD, D_MODEL)
    mod_all = jnp.pad(mod_all, ((0, 0), (0, 0), (0, MOD_ROWS - N_MOD), (0, 0)))

    e_t, cum = _head_constants()
    fnorm = final_norm.reshape(1, D_MODEL)
    x2 = _permute_tokens(x, bsz, seq)

    for l in range(DEPTH):
        mod_l = mod_all[l]
        w_main = w_in[l, :, :N_MAIN].astype(bf)
        w_dt = jnp.pad(w_in[l, :, N_MAIN:], ((0, 0), (0, LANES - HEADS))).astype(bf)
        proj, dt_raw = _inproj_call(x2, mod_l, ln1[l].reshape(1, D_MODEL), w_main, w_dt, seq)

        hp = jnp.zeros((SUBLANES, LANES), f32)
        hp = hp.at[0, :HEADS].set(dt_bias[l]).at[1, :HEADS].set(a_log[l])
        parts, beta_t = _dtprep_call(dt_raw, hp, cum)
        beta_t = beta_t.reshape(GROUPS, HEADS_PER_GROUP, bsz * seq)
        parts = parts.reshape(2 * N_SPLIT, GROUPS, HEADS_PER_GROUP, bsz * seq)
        parts = jnp.swapaxes(parts, 0, 1).reshape(GROUPS, 2 * N_SPLIT * HEADS_PER_GROUP, -1)
        parts = jnp.pad(parts, ((0, 0), (0, PART_ROWS - parts.shape[1]), (0, 0)))

        cw = _group_cols(ssm_conv_w[l])
        cbias = _group_cols(ssm_conv_b[l].reshape(1, -1))
        scw = jnp.transpose(conv_w[l].reshape(CONV_WIDTH, GROUPS, SHORT_PER_GROUP), (1, 0, 2))
        dskip_e = jnp.repeat(d_skip[l], HEAD_DIM).reshape(GROUPS, 1, GROUP_X)
        nw = ssm_norm_w[l].reshape(GROUPS, 1, GROUP_X)
        y_ssm, y_conv = _mixer_call(proj, parts, beta_t, cw, cbias, scw, dskip_e, nw, e_t,
                                    bsz, seq)

        x2 = _out_call(y_conv, y_ssm, proj, x2, mod_l, w_conv_out[l].astype(bf),
                       w_ssm_out[l].astype(bf), w_o[l].astype(bf), seq)
        x2 = _mlp_call(x2, mod_l, ln2[l].reshape(1, D_MODEL), w_up[l].astype(bf),
                       w_down[l].astype(bf), fnorm, seq, final=(l == DEPTH - 1))
    return _permute_tokens(x2, bsz, seq, inverse=True).reshape(bsz, seq, D_MODEL)
```

```python
import functools

import numpy as np
import jax
import jax.numpy as jnp
from jax import lax
from jax.experimental import pallas as pl
from jax.experimental.pallas import tpu as pltpu

D_MODEL = 1024
DEPTH = 4
EPS = 1e-6
N_MOD = 6
MOD_ROWS = 8
CONV_DIM = D_MODEL
CONV_WIDTH = 3
D_SSM = 2 * D_MODEL
HEAD_DIM = 64
HEADS = D_SSM // HEAD_DIM
GROUPS = 8
PAIRS = GROUPS // 2
HEADS_PER_GROUP = HEADS // GROUPS
GROUP_X = HEADS_PER_GROUP * HEAD_DIM
STATE = 128
SSM_CONV_WIDTH = 4
D_FF = 4 * D_MODEL
LANES = 128
SUBLANES = 8
F32_TINY = float(np.finfo(np.float32).tiny)
LOG2E = float(np.log2(np.e))

OFF_CB = 2 * D_MODEL
OFF_CC = OFF_CB + CONV_DIM
OFF_CX = OFF_CC + CONV_DIM
OFF_Z = OFF_CX + CONV_DIM
OFF_XS = OFF_Z + D_SSM
OFF_BS = OFF_XS + D_SSM
OFF_CS = OFF_BS + GROUPS * STATE
OFF_DT = OFF_CS + GROUPS * STATE

SCAN_CHUNK = 128
SLABS = SCAN_CHUNK // SUBLANES
GROUP_CONV = GROUP_X + 2 * STATE
SHORT_PER_GROUP = CONV_DIM // GROUPS
GATE_PER_GROUP = 2 * D_MODEL // GROUPS
G_XBC = 0
G_Z = G_XBC + GROUP_CONV
G_CB = G_Z + GROUP_X
G_CC = G_CB + SHORT_PER_GROUP
G_CX = G_CC + SHORT_PER_GROUP
G_GATE = G_CX + SHORT_PER_GROUP
GROUP_COLS = G_GATE + GATE_PER_GROUP
PROJ_PIECE = 256

N_SPLIT = 3
PART_ROWS = 32
BETA_ROW0 = N_SPLIT * HEADS_PER_GROUP

VMEM_LIMIT = 48 * 1024 * 1024
VMEM_LIMIT_MIXER = 56 * 1024 * 1024


def _cparams(n_axes, limit=VMEM_LIMIT):
    return pltpu.CompilerParams(
        dimension_semantics=("arbitrary",) * n_axes,
        vmem_limit_bytes=limit)


def _dot(a, b):
    return jnp.dot(a, b, preferred_element_type=jnp.float32)


def _split3(v):
    hi = v.astype(jnp.bfloat16)
    r1 = v - hi.astype(jnp.float32)
    mid = r1.astype(jnp.bfloat16)
    lo = (r1 - mid.astype(jnp.float32)).astype(jnp.bfloat16)
    return hi, mid, lo


def _silu(v):
    return v / (1.0 + jnp.exp2(v * (-LOG2E)))


def _chunk_time(p):
    return (p % SUBLANES) * SLABS + p // SUBLANES


def _short_channel0(g):
    return (g % 2) * (CONV_DIM // 2) + (g // 2) * SHORT_PER_GROUP


def _gate_col0(g):
    return (g % 2) * D_MODEL + (g // 2) * GATE_PER_GROUP


def _ada_kernel(c_ref, w_ref, b_ref, o_ref):
    c = c_ref[...]
    o_ref[0] = _dot(_silu(c), w_ref[0]) + b_ref[0]


def _ada_call(c_pad, w_ada, b_ada):
    n_out = N_MOD * D_MODEL
    tn = D_MODEL
    return pl.pallas_call(
        _ada_kernel,
        grid=(DEPTH, n_out // tn),
        in_specs=[
            pl.BlockSpec((SUBLANES, D_MODEL), lambda l, j: (0, 0)),
            pl.BlockSpec((1, D_MODEL, tn), lambda l, j: (l, 0, j)),
            pl.BlockSpec((1, 1, tn), lambda l, j: (l, 0, j)),
        ],
        out_specs=pl.BlockSpec((1, SUBLANES, tn), lambda l, j: (l, 0, j)),
        out_shape=jax.ShapeDtypeStruct((DEPTH, SUBLANES, n_out), jnp.float32),
        compiler_params=_cparams(2),
        name="ada_mod",
    )(c_pad, w_ada, b_ada.reshape(DEPTH, 1, n_out))


def _modnorm_to(x_ref, ln_ref, scale, shift, u_ref, rows, chunk=256):
    w = ln_ref[...] * (1.0 + scale)

    def body(r, carry):
        r0 = pl.multiple_of(r * chunk, chunk)
        xf = x_ref[pl.ds(r0, chunk), :]
        ms = jnp.mean(xf * xf, axis=-1, keepdims=True)
        y = xf * lax.rsqrt(ms + EPS)
        u_ref[pl.ds(r0, chunk), :] = (y * w + shift).astype(u_ref.dtype)
        return carry

    lax.fori_loop(0, rows // chunk, body, 0)


def _pre_kernel(x_ref, mod_ref, ln_ref, wdt_ref, hp_ref, cum_ref, perm_ref, u_ref, parts_ref,
                *, tp):
    shift = mod_ref[0, 0:1, :]
    scale = mod_ref[0, 1:2, :]
    _modnorm_to(x_ref, ln_ref, scale, shift, u_ref, tp)
    dt_raw = _dot(u_ref[...], wdt_ref[...])
    dt_bias = hp_ref[0:1, :]
    a_neg = -jnp.exp(hp_ref[1:2, :])
    cum_b = cum_ref[...]
    perm = perm_ref[...]
    q = SCAN_CHUNK
    for c in range(tp // q):
        r0 = c * q
        dtr = dt_raw[r0:r0 + q, :] + dt_bias
        dt = jnp.maximum(dtr, 0.0) + jnp.log1p(jnp.exp(-jnp.abs(dtr)))
        a_parts = _split3(dt * a_neg)
        acum = _dot(cum_b, a_parts[0]) + _dot(cum_b, a_parts[1]) + _dot(cum_b, a_parts[2])
        beta = acum - jnp.log(jnp.maximum(dt, F32_TINY))
        acum_t = (acum * LOG2E).T[0:HEADS, :]
        beta_t = (beta * LOG2E).T[0:HEADS, :]
        terms = jnp.concatenate(_split3(acum_t) + _split3(beta_t), axis=0)
        parts_ref[:, r0:r0 + q] = _dot(perm, terms).astype(parts_ref.dtype)


def _pre_call(x2, mod_l, ln, w_dt, hp, cum, perm, seq, tp=1024):
    t_total = x2.shape[0]
    tiles_per_batch = seq // tp
    return pl.pallas_call(
        functools.partial(_pre_kernel, tp=tp),
        grid=(t_total // tp,),
        in_specs=[
            pl.BlockSpec((tp, D_MODEL), lambda i: (i, 0)),
            pl.BlockSpec((1, MOD_ROWS, D_MODEL), lambda i: (i // tiles_per_batch, 0, 0)),
            pl.BlockSpec((1, D_MODEL), lambda i: (0, 0)),
            pl.BlockSpec((D_MODEL, LANES), lambda i: (0, 0)),
            pl.BlockSpec((SUBLANES, LANES), lambda i: (0, 0)),
            pl.BlockSpec((SCAN_CHUNK, SCAN_CHUNK), lambda i: (0, 0)),
            pl.BlockSpec((GROUPS * PART_ROWS, 2 * N_SPLIT * HEADS), lambda i: (0, 0)),
        ],
        out_specs=[
            pl.BlockSpec((tp, D_MODEL), lambda i: (i, 0)),
            pl.BlockSpec((GROUPS * PART_ROWS, tp), lambda i: (0, i)),
        ],
        out_shape=[
            jax.ShapeDtypeStruct((t_total, D_MODEL), jnp.bfloat16),
            jax.ShapeDtypeStruct((GROUPS * PART_ROWS, t_total), jnp.bfloat16),
        ],
        compiler_params=_cparams(1),
        name="pre",
    )(x2, mod_l, ln, w_dt, hp, cum, perm)


def _shifted_taps(cur, tail, width):
    n_wrap = width - 1
    cur_tail = cur[SLABS - n_wrap:]
    wrapped = jnp.concatenate([tail[:, SUBLANES - 1:, :], cur_tail[:, :SUBLANES - 1, :]], axis=1)
    taps = [cur]
    for s in range(1, width):
        taps.append(jnp.concatenate([wrapped[n_wrap - s:], cur[:SLABS - s]], axis=0))
    return taps, cur_tail


def _project_piece(u_ref, w_ref, col0, p_ref, piece, tb):
    lo = piece * PROJ_PIECE
    hi = min(lo + PROJ_PIECE, GROUP_COLS)
    p_ref[:, :, lo:hi] = _dot(u_ref[...], w_ref[0, :, col0 + lo:col0 + hi]
                              ).reshape(tb // SUBLANES, SUBLANES, hi - lo)


def _mix_group(p_ref, parts_ref, g, first, cw_ref, cbias_ref, scw_ref, dskip_ref, nw_ref, e_t,
               causal, ys_ref, yc_ref, gate_ref, state_ref, tail_ref, stail_ref, tb, after_chunk):
    q = SCAN_CHUNK
    nt_dims = (((1,), (1,)), ((), ()))
    tn_dims = (((0,), (0,)), ((), ()))
    cw = cw_ref[g]
    cbias = cbias_ref[g]
    scw = scw_ref[g]
    dskip = dskip_ref[g]
    nw = nw_ref[g]

    gate_ref[...] = p_ref[:, :, G_GATE:GROUP_COLS].reshape(tb, GATE_PER_GROUP).astype(gate_ref.dtype)

    state_ref[g] = jnp.where(first, 0.0, state_ref[g])
    tail = jnp.where(first, 0.0, tail_ref[g])
    stail = jnp.where(first, 0.0, stail_ref[g])

    for c in range(tb // q):
        r0 = c * q
        b0 = c * SLABS
        sv = p_ref[b0:b0 + SLABS, :, G_CC:G_CX] * p_ref[b0:b0 + SLABS, :, G_CX:G_GATE]
        staps, stail = _shifted_taps(sv, stail, CONV_WIDTH)
        sconv = scw[CONV_WIDTH - 1:CONV_WIDTH, :] * staps[0]
        for s in range(1, CONV_WIDTH):
            sconv += scw[CONV_WIDTH - 1 - s:CONV_WIDTH - s, :] * staps[s]
        yc = p_ref[b0:b0 + SLABS, :, G_CB:G_CC] * sconv
        yc_ref[r0:r0 + q, :] = yc.reshape(q, SHORT_PER_GROUP).astype(yc_ref.dtype)

        taps, tail = _shifted_taps(p_ref[b0:b0 + SLABS, :, G_XBC:G_Z], tail, SSM_CONV_WIDTH)
        acc = cbias + cw[SSM_CONV_WIDTH - 1:SSM_CONV_WIDTH, :] * taps[0]
        for s in range(1, SSM_CONV_WIDTH):
            acc += cw[SSM_CONV_WIDTH - 1 - s:SSM_CONV_WIDTH - s, :] * taps[s]
        xbc = _silu(acc).reshape(q, GROUP_CONV)
        xs = xbc[:, 0:GROUP_X]
        x_b = xs.astype(jnp.bfloat16)
        b_b = xbc[:, GROUP_X:GROUP_X + STATE].astype(jnp.bfloat16)
        c_b = xbc[:, GROUP_X + STATE:GROUP_CONV].astype(jnp.bfloat16)

        mc = parts_ref[:, r0:r0 + q]
        ab_l = lax.dot_general(mc, e_t, tn_dims, preferred_element_type=jnp.float32)
        acum_l = ab_l[:, 0:HEADS_PER_GROUP * LANES]
        beta_l = ab_l[:, HEADS_PER_GROUP * LANES:]
        acum_last = acum_l[q - 1:q, :]
        mf = mc.astype(jnp.float32)

        scores_b = lax.dot_general(c_b, b_b, nt_dims, preferred_element_type=jnp.float32
                                   ).astype(jnp.bfloat16)
        ys = []
        for r in range(HEADS_PER_GROUP):
            al = acum_l[:, r * LANES:(r + 1) * LANES]
            bl = beta_l[:, r * LANES:(r + 1) * LANES]
            last = acum_last[:, r * LANES:(r + 1) * LANES]
            beta_row = (mf[BETA_ROW0 + r:BETA_ROW0 + r + 1, :]
                        + mf[BETA_ROW0 + HEADS_PER_GROUP + r:BETA_ROW0 + HEADS_PER_GROUP + r + 1, :]
                        + mf[BETA_ROW0 + 2 * HEADS_PER_GROUP + r:
                             BETA_ROW0 + 2 * HEADS_PER_GROUP + r + 1, :])
            seg = (al - beta_row).astype(jnp.bfloat16)
            w_r = scores_b * jnp.exp2(jnp.where(causal, seg, -jnp.inf))
            c_r = c_b * jnp.exp2(al).astype(jnp.bfloat16)
            b_r = b_b * jnp.exp2(last - bl).astype(jnp.bfloat16)
            x_r = x_b[:, r * HEAD_DIM:(r + 1) * HEAD_DIM]
            s_r = state_ref[g, r]
            ys.append(_dot(jnp.concatenate([w_r, c_r], axis=1),
                           jnp.concatenate([x_r, s_r.astype(jnp.bfloat16)], axis=0)))
            state_ref[g, r] = (s_r * jnp.exp2(last[:, 0:HEAD_DIM])
                               + lax.dot_general(b_r, x_r, tn_dims,
                                                 preferred_element_type=jnp.float32))
        y = jnp.concatenate(ys, axis=-1)

        y = y + dskip * xs
        y = y * _silu(p_ref[b0:b0 + SLABS, :, G_Z:G_CB].reshape(q, GROUP_X))
        y = y * lax.rsqrt(jnp.mean(y * y, axis=-1, keepdims=True) + EPS)
        ys_ref[r0:r0 + q, :] = (y * nw).astype(ys_ref.dtype)
        after_chunk(c)

    tail_ref[g] = tail
    stail_ref[g] = stail


def _mixer_kernel(u_ref, w_ref, parts_a_ref, parts_b_ref,
                  cw_ref, cbias_ref, scw_ref, dskip_ref, nw_ref, et_ref,
                  ys_a_ref, yc_a_ref, gate_a_ref, ys_b_ref, yc_b_ref, gate_b_ref,
                  pa_ref, pb_ref, state_ref, tail_ref, stail_ref, *, tb, n_items, nt):
    s = pl.program_id(0)
    q = SCAN_CHUNK

    @pl.when(s == 0)
    def _():
        pb_ref[...] = jnp.zeros_like(pb_ref)
        state_ref[...] = jnp.zeros_like(state_ref)
        tail_ref[...] = jnp.zeros_like(tail_ref)
        stail_ref[...] = jnp.zeros_like(stail_ref)

    item_a = jnp.minimum(s, n_items - 1)
    item_b = jnp.maximum(s - 1, 0)
    g_a = 2 * (item_a % PAIRS)
    g_b = 2 * (item_b % PAIRS) + 1
    first_a = (item_a // PAIRS) % nt == 0
    first_b = (item_b // PAIRS) % nt == 0

    e_t = et_ref[...]
    t_row = _chunk_time(lax.broadcasted_iota(jnp.int32, (q, q), 0))
    t_col = _chunk_time(lax.broadcasted_iota(jnp.int32, (q, q), 1))
    causal = t_row >= t_col
    common = (cw_ref, cbias_ref, scw_ref, dskip_ref, nw_ref, e_t, causal)
    hist = (state_ref, tail_ref, stail_ref, tb)

    n_pieces = -(-GROUP_COLS // PROJ_PIECE)
    assert n_pieces <= tb // q

    def project_into(p_ref, col0):
        def after_chunk(c):
            if c < n_pieces:
                _project_piece(u_ref, w_ref, col0, p_ref, c, tb)
        return after_chunk

    _mix_group(pb_ref, parts_b_ref, g_b, first_b, *common, ys_b_ref, yc_b_ref, gate_b_ref, *hist,
               project_into(pa_ref, 0))
    _mix_group(pa_ref, parts_a_ref, g_a, first_a, *common, ys_a_ref, yc_a_ref, gate_a_ref, *hist,
               project_into(pb_ref, GROUP_COLS))


def _mixer_call(u, w_pairs, parts, cw, cbias, scw, dskip_e, nw, e_t, bsz, seq, tb=1024):
    t_total = u.shape[0]
    nt = seq // tb
    n_blocks = t_total // tb
    n_items = n_blocks * PAIRS
    rows_out = t_total + tb

    item_a = lambda s: jnp.minimum(s, n_items - 1)
    item_b = lambda s: jnp.maximum(s - 1, 0)
    out_a = lambda s: (jnp.where(s < n_items, item_a(s) // PAIRS, n_blocks), item_a(s) % PAIRS)
    out_b = lambda s: (jnp.where(s >= 1, item_b(s) // PAIRS, n_blocks), item_b(s) % PAIRS)

    def full(arr):
        nd = arr.ndim
        return pl.BlockSpec(arr.shape, lambda s: (0,) * nd)

    half = GROUPS // 2
    out_shapes = [
        jax.ShapeDtypeStruct((rows_out, half * GROUP_X), jnp.bfloat16),
        jax.ShapeDtypeStruct((rows_out, half * SHORT_PER_GROUP), jnp.bfloat16),
        jax.ShapeDtypeStruct((rows_out, half * GATE_PER_GROUP), jnp.bfloat16),
    ]
    widths = (GROUP_X, SHORT_PER_GROUP, GATE_PER_GROUP)
    return pl.pallas_call(
        functools.partial(_mixer_kernel, tb=tb, n_items=n_items, nt=nt),
        grid=(n_items + 1,),
        in_specs=[
            pl.BlockSpec((tb, D_MODEL), lambda s: (item_a(s) // PAIRS, 0)),
            pl.BlockSpec((1, D_MODEL, 2 * GROUP_COLS), lambda s: (item_a(s) % PAIRS, 0, 0)),
            pl.BlockSpec((PART_ROWS, tb), lambda s: (2 * (item_a(s) % PAIRS), item_a(s) // PAIRS)),
            pl.BlockSpec((PART_ROWS, tb),
                         lambda s: (2 * (item_b(s) % PAIRS) + 1, item_b(s) // PAIRS)),
            full(cw), full(cbias), full(scw), full(dskip_e), full(nw), full(e_t),
        ],
        out_specs=([pl.BlockSpec((tb, w), out_a) for w in widths]
                   + [pl.BlockSpec((tb, w), out_b) for w in widths]),
        out_shape=out_shapes + out_shapes,
        scratch_shapes=[
            pltpu.VMEM((tb // SUBLANES, SUBLANES, GROUP_COLS), jnp.float32),
            pltpu.VMEM((tb // SUBLANES, SUBLANES, GROUP_COLS), jnp.float32),
            pltpu.VMEM((GROUPS, HEADS_PER_GROUP, STATE, HEAD_DIM), jnp.float32),
            pltpu.VMEM((GROUPS, SSM_CONV_WIDTH - 1, SUBLANES, GROUP_CONV), jnp.float32),
            pltpu.VMEM((GROUPS, CONV_WIDTH - 1, SUBLANES, SHORT_PER_GROUP), jnp.float32),
        ],
        compiler_params=_cparams(1, VMEM_LIMIT_MIXER),
        name="mixer",
    )(u, w_pairs, parts, parts, cw, cbias, scw, dskip_e, nw, e_t)


def _out_kernel(yca_ref, ycb_ref, ysa_ref, ysb_ref, ga_ref, gb_ref, x_ref, mod_ref,
                wca_ref, wcb_ref, wsa_ref, wsb_ref, wo_ref, o_ref):
    p_conv = _dot(yca_ref[...], wca_ref[...]) + _dot(ycb_ref[...], wcb_ref[...])
    p_ssm = _dot(ysa_ref[...], wsa_ref[...]) + _dot(ysb_ref[...], wsb_ref[...])
    g_conv = 1.0 / (1.0 + jnp.exp(-ga_ref[...].astype(jnp.float32)))
    g_ssm = 1.0 / (1.0 + jnp.exp(-gb_ref[...].astype(jnp.float32)))
    merged = g_conv * p_conv + g_ssm * p_ssm
    mix = _dot(merged.astype(jnp.bfloat16), wo_ref[...])
    gate1 = mod_ref[0, 2:3, :]
    o_ref[...] = x_ref[...] + gate1 * mix


def _out_call(mixed, x2, mod_l, wca, wcb, wsa, wsb, wo, seq, tm=512):
    ys_a, yc_a, g_a, ys_b, yc_b, g_b = mixed
    t_total = x2.shape[0]
    tiles_per_batch = seq // tm
    row = lambda width: pl.BlockSpec((tm, width), lambda i: (i, 0))
    const = lambda shape: pl.BlockSpec(shape, lambda i: (0, 0))
    return pl.pallas_call(
        _out_kernel,
        grid=(t_total // tm,),
        in_specs=[
            row(CONV_DIM // 2), row(CONV_DIM // 2), row(D_SSM // 2), row(D_SSM // 2),
            row(D_MODEL), row(D_MODEL), row(D_MODEL),
            pl.BlockSpec((1, MOD_ROWS, D_MODEL), lambda i: (i // tiles_per_batch, 0, 0)),
            const((CONV_DIM // 2, D_MODEL)), const((CONV_DIM // 2, D_MODEL)),
            const((D_SSM // 2, D_MODEL)), const((D_SSM // 2, D_MODEL)),
            const((D_MODEL, D_MODEL)),
        ],
        out_specs=pl.BlockSpec((tm, D_MODEL), lambda i: (i, 0)),
        out_shape=jax.ShapeDtypeStruct((t_total, D_MODEL), jnp.float32),
        compiler_params=_cparams(1),
        name="mix_out",
    )(yc_a, yc_b, ys_a, ys_b, g_a, g_b, x2, mod_l, wca, wcb, wsa, wsb, wo)


def _mlp_kernel(x_ref, mod_ref, ln_ref, wup_ref, wdn_ref, fn_ref, o_ref, u_ref, acc_ref,
                *, tm, final):
    j = pl.program_id(1)

    @pl.when(j == 0)
    def _():
        shift = mod_ref[0, 3:4, :]
        scale = mod_ref[0, 4:5, :]
        _modnorm_to(x_ref, ln_ref, scale, shift, u_ref, tm)
        acc_ref[...] = jnp.zeros_like(acc_ref)

    h = jnp.maximum(_dot(u_ref[...], wup_ref[...]), 0.0)
    acc_ref[...] += _dot((h * h).astype(jnp.bfloat16), wdn_ref[...])

    @pl.when(j == pl.num_programs(1) - 1)
    def _():
        gate2 = mod_ref[0, 5:6, :]
        xn = x_ref[...] + gate2 * acc_ref[...]
        if final:
            ms = jnp.mean(xn * xn, axis=-1, keepdims=True)
            xn = xn * lax.rsqrt(ms + EPS) * fn_ref[...]
        o_ref[...] = xn


def _mlp_call(x2, mod_l, ln, w_up, w_dn, fnorm, seq, final, tm=1024, tf=1024):
    t_total = x2.shape[0]
    tiles_per_batch = seq // tm
    return pl.pallas_call(
        functools.partial(_mlp_kernel, tm=tm, final=final),
        grid=(t_total // tm, D_FF // tf),
        in_specs=[
            pl.BlockSpec((tm, D_MODEL), lambda i, j: (i, 0)),
            pl.BlockSpec((1, MOD_ROWS, D_MODEL), lambda i, j: (i // tiles_per_batch, 0, 0)),
            pl.BlockSpec((1, D_MODEL), lambda i, j: (0, 0)),
            pl.BlockSpec((D_MODEL, tf), lambda i, j: (0, j)),
            pl.BlockSpec((tf, D_MODEL), lambda i, j: (j, 0)),
            pl.BlockSpec((1, D_MODEL), lambda i, j: (0, 0)),
        ],
        out_specs=pl.BlockSpec((tm, D_MODEL), lambda i, j: (i, 0)),
        out_shape=jax.ShapeDtypeStruct((t_total, D_MODEL), jnp.float32),
        scratch_shapes=[pltpu.VMEM((tm, D_MODEL), jnp.bfloat16),
                        pltpu.VMEM((tm, D_MODEL), jnp.float32)],
        compiler_params=_cparams(2),
        name="mlp",
    )(x2, mod_l, ln, w_up, w_dn, fnorm)


def _head_constants():
    e_t = np.zeros((PART_ROWS, 2, HEADS_PER_GROUP, LANES), np.float32)
    perm = np.zeros((GROUPS, PART_ROWS, 2 * N_SPLIT, HEADS), np.float32)
    for k in range(2 * N_SPLIT):
        for r in range(HEADS_PER_GROUP):
            e_t[k * HEADS_PER_GROUP + r, k // N_SPLIT, r, :] = 1.0
            for g in range(GROUPS):
                perm[g, k * HEADS_PER_GROUP + r, k, g * HEADS_PER_GROUP + r] = 1.0
    e_t = e_t.reshape(PART_ROWS, 2 * HEADS_PER_GROUP * LANES)
    perm = perm.reshape(GROUPS * PART_ROWS, 2 * N_SPLIT * HEADS)
    tm = _chunk_time(np.arange(SCAN_CHUNK))
    cum = (tm[None, :] <= tm[:, None]).astype(np.float32)
    bf = jnp.bfloat16
    return jnp.asarray(e_t, bf), jnp.asarray(perm, bf), jnp.asarray(cum, bf)


def _group_cols(w_l):
    k = w_l.shape[0]
    xs = w_l[:, :D_SSM].reshape(k, GROUPS, GROUP_X)
    bs = w_l[:, D_SSM:D_SSM + GROUPS * STATE].reshape(k, GROUPS, STATE)
    cs = w_l[:, D_SSM + GROUPS * STATE:].reshape(k, GROUPS, STATE)
    return jnp.transpose(jnp.concatenate([xs, bs, cs], axis=-1), (1, 0, 2))


def _regroup_w_in(w):
    pairs = []
    for g in range(GROUPS):
        sc = _short_channel0(g)
        cols = [
            (OFF_XS + g * GROUP_X, GROUP_X), (OFF_BS + g * STATE, STATE), (OFF_CS + g * STATE, STATE),
            (OFF_Z + g * GROUP_X, GROUP_X),
            (OFF_CB + sc, SHORT_PER_GROUP), (OFF_CC + sc, SHORT_PER_GROUP), (OFF_CX + sc, SHORT_PER_GROUP),
            (_gate_col0(g), GATE_PER_GROUP),
        ]
        pairs.extend(w[:, o:o + n] for o, n in cols)
    return jnp.concatenate(pairs, axis=1).astype(jnp.bfloat16).reshape(
        D_MODEL, PAIRS, 2 * GROUP_COLS).transpose(1, 0, 2)


def _permute_tokens(x, bsz, seq, inverse=False):
    a, b = (SLABS, SUBLANES) if inverse else (SUBLANES, SLABS)
    x5 = x.reshape(bsz, seq // SCAN_CHUNK, a, b, D_MODEL)
    return jnp.swapaxes(x5, 2, 3).reshape(bsz * seq, D_MODEL)


def kernel(x, c, w_ada, b_ada, ln1, ln2, w_in, conv_w, ssm_conv_w, ssm_conv_b, dt_bias, a_log,
           d_skip, ssm_norm_w, w_conv_out, w_ssm_out, w_o, w_up, w_down, final_norm):
    bsz, seq, _ = x.shape
    bf = jnp.bfloat16
    f32 = jnp.float32

    c_pad = jnp.zeros((SUBLANES, D_MODEL), f32).at[:bsz].set(c)
    mod_all = _ada_call(c_pad, w_ada, b_ada)
    mod_all = mod_all[:, :bsz].reshape(DEPTH, bsz, N_MOD, D_MODEL)
    mod_all = jnp.pad(mod_all, ((0, 0), (0, 0), (0, MOD_ROWS - N_MOD), (0, 0)))

    e_t, perm, cum = _head_constants()
    fnorm = final_norm.reshape(1, D_MODEL)
    x2 = _permute_tokens(x, bsz, seq)

    for l in range(DEPTH):
        mod_l = mod_all[l]
        w_dt = jnp.pad(w_in[l, :, OFF_DT:], ((0, 0), (0, LANES - HEADS))).astype(bf)
        hp = jnp.zeros((SUBLANES, LANES), f32)
        hp = hp.at[0, :HEADS].set(dt_bias[l]).at[1, :HEADS].set(a_log[l])
        u, parts = _pre_call(x2, mod_l, ln1[l].reshape(1, D_MODEL), w_dt, hp, cum, perm, seq)

        cw = _group_cols(ssm_conv_w[l])
        cbias = _group_cols(ssm_conv_b[l].reshape(1, -1))
        scw = jnp.stack([conv_w[l][:, _short_channel0(g):_short_channel0(g) + SHORT_PER_GROUP]
                         for g in range(GROUPS)])
        dskip_e = jnp.repeat(d_skip[l], HEAD_DIM).reshape(GROUPS, 1, GROUP_X)
        nw = ssm_norm_w[l].reshape(GROUPS, 1, GROUP_X)
        mixed = _mixer_call(u, _regroup_w_in(w_in[l]), parts, cw, cbias, scw, dskip_e, nw, e_t,
                            bsz, seq)

        ws = w_ssm_out[l].astype(bf).reshape(PAIRS, 2, GROUP_X, D_MODEL)
        wc = w_conv_out[l].astype(bf)
        x2 = _out_call(mixed, x2, mod_l, wc[:CONV_DIM // 2], wc[CONV_DIM // 2:],
                       ws[:, 0].reshape(D_SSM // 2, D_MODEL), ws[:, 1].reshape(D_SSM // 2, D_MODEL),
                       w_o[l].astype(bf), seq)
        x2 = _mlp_call(x2, mod_l, ln2[l].reshape(1, D_MODEL), w_up[l].astype(bf),
                       w_down[l].astype(bf), fnorm, seq, final=(l == DEPTH - 1))
    return _permute_tokens(x2, bsz, seq, inverse=True).reshape(bsz, seq, D_MODEL)
```

```python
import functools

import numpy as np
import jax
import jax.numpy as jnp
from jax import lax
from jax.experimental import pallas as pl
from jax.experimental.pallas import tpu as pltpu

D_MODEL = 1024
DEPTH = 4
EPS = 1e-6
N_MOD = 6
MOD_ROWS = 8
CONV_DIM = D_MODEL
CONV_WIDTH = 3
D_SSM = 2 * D_MODEL
HEAD_DIM = 64
HEADS = D_SSM // HEAD_DIM
GROUPS = 8
PAIRS = GROUPS // 2
HEADS_PER_GROUP = HEADS // GROUPS
GROUP_X = HEADS_PER_GROUP * HEAD_DIM
STATE = 128
SSM_CONV_WIDTH = 4
D_FF = 4 * D_MODEL
LANES = 128
SUBLANES = 8
F32_TINY = float(np.finfo(np.float32).tiny)
LOG2E = float(np.log2(np.e))

OFF_CB = 2 * D_MODEL
OFF_CC = OFF_CB + CONV_DIM
OFF_CX = OFF_CC + CONV_DIM
OFF_Z = OFF_CX + CONV_DIM
OFF_XS = OFF_Z + D_SSM
OFF_BS = OFF_XS + D_SSM
OFF_CS = OFF_BS + GROUPS * STATE
OFF_DT = OFF_CS + GROUPS * STATE

SCAN_CHUNK = 128
SLABS = SCAN_CHUNK // SUBLANES
GROUP_CONV = GROUP_X + 2 * STATE
SHORT_PER_GROUP = CONV_DIM // GROUPS
GATE_PER_GROUP = 2 * D_MODEL // GROUPS
G_XBC = 0
G_Z = G_XBC + GROUP_CONV
G_CB = G_Z + GROUP_X
G_CC = G_CB + SHORT_PER_GROUP
G_CX = G_CC + SHORT_PER_GROUP
G_GATE = G_CX + SHORT_PER_GROUP
GROUP_COLS = G_GATE + GATE_PER_GROUP
PROJ_PIECE = 256
PIECES_WHILE_ODD = {0: (0,), 1: (1,), 2: (2,), 4: (3,), 5: (4,), 7: (5,)}
PIECES_WHILE_EVEN = {0: (6,), 1: (7,), 3: (8,), 4: (9,), 6: (10,)}

N_SPLIT = 3
PART_ROWS = 32
BETA_ROW0 = N_SPLIT * HEADS_PER_GROUP

VMEM_LIMIT = 48 * 1024 * 1024
VMEM_LIMIT_MIXER = 56 * 1024 * 1024


def _cparams(n_axes, limit=VMEM_LIMIT):
    return pltpu.CompilerParams(
        dimension_semantics=("arbitrary",) * n_axes,
        vmem_limit_bytes=limit)


def _dot(a, b):
    return jnp.dot(a, b, preferred_element_type=jnp.float32)


def _split3(v):
    hi = v.astype(jnp.bfloat16)
    r1 = v - hi.astype(jnp.float32)
    mid = r1.astype(jnp.bfloat16)
    lo = (r1 - mid.astype(jnp.float32)).astype(jnp.bfloat16)
    return hi, mid, lo


def _silu(v):
    return v / (1.0 + jnp.exp2(v * (-LOG2E)))


def _chunk_time(p):
    return (p % SUBLANES) * SLABS + p // SUBLANES


def _ada_kernel(c_ref, w_ref, b_ref, o_ref):
    c = c_ref[...]
    o_ref[0] = _dot(_silu(c), w_ref[0]) + b_ref[0]


def _ada_call(c_pad, w_ada, b_ada):
    n_out = N_MOD * D_MODEL
    tn = D_MODEL
    return pl.pallas_call(
        _ada_kernel,
        grid=(DEPTH, n_out // tn),
        in_specs=[
            pl.BlockSpec((SUBLANES, D_MODEL), lambda l, j: (0, 0)),
            pl.BlockSpec((1, D_MODEL, tn), lambda l, j: (l, 0, j)),
            pl.BlockSpec((1, 1, tn), lambda l, j: (l, 0, j)),
        ],
        out_specs=pl.BlockSpec((1, SUBLANES, tn), lambda l, j: (l, 0, j)),
        out_shape=jax.ShapeDtypeStruct((DEPTH, SUBLANES, n_out), jnp.float32),
        compiler_params=_cparams(2),
        name="ada_mod",
    )(c_pad, w_ada, b_ada.reshape(DEPTH, 1, n_out))


def _modnorm_to(x_ref, ln_ref, scale, shift, u_ref, rows, chunk=256):
    w = ln_ref[...] * (1.0 + scale)

    def body(r, carry):
        r0 = pl.multiple_of(r * chunk, chunk)
        xf = x_ref[pl.ds(r0, chunk), :]
        ms = jnp.mean(xf * xf, axis=-1, keepdims=True)
        y = xf * lax.rsqrt(ms + EPS)
        u_ref[pl.ds(r0, chunk), :] = (y * w + shift).astype(u_ref.dtype)
        return carry

    lax.fori_loop(0, rows // chunk, body, 0)


def _pre_kernel(x_ref, mod_ref, ln_ref, wdt_ref, hp_ref, cum_ref, perm_ref, u_ref, parts_ref,
                *, tp):
    shift = mod_ref[0, 0:1, :]
    scale = mod_ref[0, 1:2, :]
    _modnorm_to(x_ref, ln_ref, scale, shift, u_ref, tp)
    dt_raw = _dot(u_ref[...], wdt_ref[...])
    dt_bias = hp_ref[0:1, :]
    a_neg = -jnp.exp(hp_ref[1:2, :])
    cum_b = cum_ref[...]
    perm = perm_ref[...]
    q = SCAN_CHUNK
    for c in range(tp // q):
        r0 = c * q
        dtr = dt_raw[r0:r0 + q, :] + dt_bias
        dt = jnp.maximum(dtr, 0.0) + jnp.log1p(jnp.exp(-jnp.abs(dtr)))
        a_parts = _split3(dt * a_neg)
        acum = _dot(cum_b, a_parts[0]) + _dot(cum_b, a_parts[1]) + _dot(cum_b, a_parts[2])
        beta = acum - jnp.log(jnp.maximum(dt, F32_TINY))
        acum_t = (acum * LOG2E).T[0:HEADS, :]
        beta_t = (beta * LOG2E).T[0:HEADS, :]
        terms = jnp.concatenate(_split3(acum_t) + _split3(beta_t), axis=0)
        parts_ref[:, r0:r0 + q] = _dot(perm, terms).astype(parts_ref.dtype)


def _pre_call(x2, mod_l, ln, w_dt, hp, cum, perm, seq, tp=1024):
    t_total = x2.shape[0]
    tiles_per_batch = seq // tp
    return pl.pallas_call(
        functools.partial(_pre_kernel, tp=tp),
        grid=(t_total // tp,),
        in_specs=[
            pl.BlockSpec((tp, D_MODEL), lambda i: (i, 0)),
            pl.BlockSpec((1, MOD_ROWS, D_MODEL), lambda i: (i // tiles_per_batch, 0, 0)),
            pl.BlockSpec((1, D_MODEL), lambda i: (0, 0)),
            pl.BlockSpec((D_MODEL, LANES), lambda i: (0, 0)),
            pl.BlockSpec((SUBLANES, LANES), lambda i: (0, 0)),
            pl.BlockSpec((SCAN_CHUNK, SCAN_CHUNK), lambda i: (0, 0)),
            pl.BlockSpec((GROUPS * PART_ROWS, 2 * N_SPLIT * HEADS), lambda i: (0, 0)),
        ],
        out_specs=[
            pl.BlockSpec((tp, D_MODEL), lambda i: (i, 0)),
            pl.BlockSpec((GROUPS * PART_ROWS, tp), lambda i: (0, i)),
        ],
        out_shape=[
            jax.ShapeDtypeStruct((t_total, D_MODEL), jnp.bfloat16),
            jax.ShapeDtypeStruct((GROUPS * PART_ROWS, t_total), jnp.bfloat16),
        ],
        compiler_params=_cparams(1),
        name="pre",
    )(x2, mod_l, ln, w_dt, hp, cum, perm)


def _shifted_taps(cur, tail, width):
    n_wrap = width - 1
    cur_tail = cur[SLABS - n_wrap:]
    wrapped = jnp.concatenate([tail[:, SUBLANES - 1:, :], cur_tail[:, :SUBLANES - 1, :]], axis=1)
    taps = [cur]
    for s in range(1, width):
        taps.append(jnp.concatenate([wrapped[n_wrap - s:], cur[:SLABS - s]], axis=0))
    return taps, cur_tail


def _project_piece(u_ref, w_ref, pa_ref, pb_ref, piece, tb):
    lo = piece * PROJ_PIECE
    res = _dot(u_ref[...], w_ref[:, lo:lo + PROJ_PIECE]
               ).reshape(tb // SUBLANES, SUBLANES, PROJ_PIECE)
    n_a = min(max(GROUP_COLS - lo, 0), PROJ_PIECE)
    if n_a:
        pa_ref[:, :, lo:lo + n_a] = res[:, :, 0:n_a]
    if n_a < PROJ_PIECE:
        b0 = lo + n_a - GROUP_COLS
        pb_ref[:, :, b0:b0 + PROJ_PIECE - n_a] = res[:, :, n_a:]


def _mix_group(p_ref, parts_ref, g, first, cw_ref, cbias_ref, scw_ref, dskip_ref, nw_ref, e_t,
               causal, ys_ref, yc_ref, gate_ref, state_ref, tail_ref, stail_ref, tb, after_chunk):
    q = SCAN_CHUNK
    nt_dims = (((1,), (1,)), ((), ()))
    tn_dims = (((0,), (0,)), ((), ()))
    cw = cw_ref[g]
    cbias = cbias_ref[g]
    scw = scw_ref[g]
    dskip = dskip_ref[g]
    nw = nw_ref[g]

    gate_ref[...] = p_ref[:, :, G_GATE:GROUP_COLS].reshape(tb, GATE_PER_GROUP).astype(gate_ref.dtype)

    state_ref[g] = jnp.where(first, 0.0, state_ref[g])
    tail = jnp.where(first, 0.0, tail_ref[g])
    stail = jnp.where(first, 0.0, stail_ref[g])

    for c in range(tb // q):
        r0 = c * q
        b0 = c * SLABS
        sv = p_ref[b0:b0 + SLABS, :, G_CC:G_CX] * p_ref[b0:b0 + SLABS, :, G_CX:G_GATE]
        staps, stail = _shifted_taps(sv, stail, CONV_WIDTH)
        sconv = scw[CONV_WIDTH - 1:CONV_WIDTH, :] * staps[0]
        for s in range(1, CONV_WIDTH):
            sconv += scw[CONV_WIDTH - 1 - s:CONV_WIDTH - s, :] * staps[s]
        yc = p_ref[b0:b0 + SLABS, :, G_CB:G_CC] * sconv
        yc_ref[r0:r0 + q, :] = yc.reshape(q, SHORT_PER_GROUP).astype(yc_ref.dtype)

        taps, tail = _shifted_taps(p_ref[b0:b0 + SLABS, :, G_XBC:G_Z], tail, SSM_CONV_WIDTH)
        acc = cbias + cw[SSM_CONV_WIDTH - 1:SSM_CONV_WIDTH, :] * taps[0]
        for s in range(1, SSM_CONV_WIDTH):
            acc += cw[SSM_CONV_WIDTH - 1 - s:SSM_CONV_WIDTH - s, :] * taps[s]
        xbc = _silu(acc).reshape(q, GROUP_CONV)
        xs = xbc[:, 0:GROUP_X]
        xt_b = xs.astype(jnp.bfloat16).T
        b_b = xbc[:, GROUP_X:GROUP_X + STATE].astype(jnp.bfloat16)
        c_b = xbc[:, GROUP_X + STATE:GROUP_CONV].astype(jnp.bfloat16)

        mc = parts_ref[:, r0:r0 + q]
        ab_l = lax.dot_general(mc, e_t, tn_dims, preferred_element_type=jnp.float32)
        acum_l = ab_l[:, 0:HEADS_PER_GROUP * LANES]
        beta_l = ab_l[:, HEADS_PER_GROUP * LANES:]
        acum_last = acum_l[q - 1:q, :]
        mf = mc.astype(jnp.float32)

        scores_b = lax.dot_general(c_b, b_b, nt_dims, preferred_element_type=jnp.float32
                                   ).astype(jnp.bfloat16)
        ys = []
        for r in range(HEADS_PER_GROUP):
            al = acum_l[:, r * LANES:(r + 1) * LANES]
            bl = beta_l[:, r * LANES:(r + 1) * LANES]
            last = acum_last[:, r * LANES:(r + 1) * LANES]
            beta_row = (mf[BETA_ROW0 + r:BETA_ROW0 + r + 1, :]
                        + mf[BETA_ROW0 + HEADS_PER_GROUP + r:BETA_ROW0 + HEADS_PER_GROUP + r + 1, :]
                        + mf[BETA_ROW0 + 2 * HEADS_PER_GROUP + r:
                             BETA_ROW0 + 2 * HEADS_PER_GROUP + r + 1, :])
            seg = (al - beta_row).astype(jnp.bfloat16)
            w_r = scores_b * jnp.exp2(jnp.where(causal, seg, -jnp.inf))
            c_r = c_b * jnp.exp2(al).astype(jnp.bfloat16)
            b_r = b_b * jnp.exp2(last - bl).astype(jnp.bfloat16)
            xt_r = xt_b[r * HEAD_DIM:(r + 1) * HEAD_DIM, :]
            s_r = state_ref[g, r]
            ys.append(lax.dot_general(
                jnp.concatenate([w_r, c_r], axis=1),
                jnp.concatenate([xt_r, s_r.astype(jnp.bfloat16)], axis=1),
                nt_dims, preferred_element_type=jnp.float32))
            state_ref[g, r] = s_r * jnp.exp2(last) + _dot(xt_r, b_r)
        y = jnp.concatenate(ys, axis=-1)

        y = y + dskip * xs
        y = y * _silu(p_ref[b0:b0 + SLABS, :, G_Z:G_CB].reshape(q, GROUP_X))
        y = y * lax.rsqrt(jnp.mean(y * y, axis=-1, keepdims=True) + EPS)
        ys_ref[r0:r0 + q, :] = (y * nw).astype(ys_ref.dtype)
        after_chunk(c)

    tail_ref[g] = tail
    stail_ref[g] = stail


def _mixer_kernel(u_ref, w_ref, parts_a_ref, parts_b_ref,
                  cw_ref, cbias_ref, scw_ref, dskip_ref, nw_ref, et_ref,
                  ys_a_ref, yc_a_ref, gate_a_ref, ys_b_ref, yc_b_ref, gate_b_ref,
                  pa_ref, pb_ref, state_ref, tail_ref, stail_ref, *, tb, n_items, nt):
    s = pl.program_id(0)
    q = SCAN_CHUNK

    @pl.when(s == 0)
    def _():
        pb_ref[...] = jnp.zeros_like(pb_ref)
        state_ref[...] = jnp.zeros_like(state_ref)
        tail_ref[...] = jnp.zeros_like(tail_ref)
        stail_ref[...] = jnp.zeros_like(stail_ref)

    item_a = jnp.minimum(s, n_items - 1)
    item_b = jnp.maximum(s - 1, 0)
    g_a = 2 * (item_a % PAIRS)
    g_b = 2 * (item_b % PAIRS) + 1
    first_a = (item_a // PAIRS) % nt == 0
    first_b = (item_b // PAIRS) % nt == 0

    e_t = et_ref[...]
    t_row = _chunk_time(lax.broadcasted_iota(jnp.int32, (q, q), 0))
    t_col = _chunk_time(lax.broadcasted_iota(jnp.int32, (q, q), 1))
    causal = t_row >= t_col
    common = (cw_ref, cbias_ref, scw_ref, dskip_ref, nw_ref, e_t, causal)
    hist = (state_ref, tail_ref, stail_ref, tb)

    def project_after(schedule):
        def after_chunk(c):
            for piece in schedule.get(c, ()):
                _project_piece(u_ref, w_ref, pa_ref, pb_ref, piece, tb)
        return after_chunk

    _mix_group(pb_ref, parts_b_ref, g_b, first_b, *common, ys_b_ref, yc_b_ref, gate_b_ref, *hist,
               project_after(PIECES_WHILE_ODD))
    _mix_group(pa_ref, parts_a_ref, g_a, first_a, *common, ys_a_ref, yc_a_ref, gate_a_ref, *hist,
               project_after(PIECES_WHILE_EVEN))


def _mixer_call(u, w_pairs, parts, cw, cbias, scw, dskip_e, nw, e_t, bsz, seq, tb=1024):
    t_total = u.shape[0]
    nt = seq // tb
    n_blocks = t_total // tb
    n_items = n_blocks * PAIRS
    rows_out = t_total + tb

    item_a = lambda s: jnp.minimum(s, n_items - 1)
    item_b = lambda s: jnp.maximum(s - 1, 0)
    out_a = lambda s: (jnp.where(s < n_items, item_a(s) // PAIRS, n_blocks), item_a(s) % PAIRS)
    out_b = lambda s: (jnp.where(s >= 1, item_b(s) // PAIRS, n_blocks), item_b(s) % PAIRS)

    def full(arr):
        nd = arr.ndim
        return pl.BlockSpec(arr.shape, lambda s: (0,) * nd)

    half = GROUPS // 2
    out_shapes = [
        jax.ShapeDtypeStruct((rows_out, half * GROUP_X), jnp.bfloat16),
        jax.ShapeDtypeStruct((rows_out, half * SHORT_PER_GROUP), jnp.bfloat16),
        jax.ShapeDtypeStruct((rows_out, half * GATE_PER_GROUP), jnp.bfloat16),
    ]
    widths = (GROUP_X, SHORT_PER_GROUP, GATE_PER_GROUP)
    return pl.pallas_call(
        functools.partial(_mixer_kernel, tb=tb, n_items=n_items, nt=nt),
        grid=(n_items + 1,),
        in_specs=[
            pl.BlockSpec((tb, D_MODEL), lambda s: (item_a(s) // PAIRS, 0)),
            pl.BlockSpec((D_MODEL, 2 * GROUP_COLS), lambda s: (0, item_a(s) % PAIRS)),
            pl.BlockSpec((PART_ROWS, tb), lambda s: (2 * (item_a(s) % PAIRS), item_a(s) // PAIRS)),
            pl.BlockSpec((PART_ROWS, tb),
                         lambda s: (2 * (item_b(s) % PAIRS) + 1, item_b(s) // PAIRS)),
            full(cw), full(cbias), full(scw), full(dskip_e), full(nw), full(e_t),
        ],
        out_specs=([pl.BlockSpec((tb, w), out_a) for w in widths]
                   + [pl.BlockSpec((tb, w), out_b) for w in widths]),
        out_shape=out_shapes + out_shapes,
        scratch_shapes=[
            pltpu.VMEM((tb // SUBLANES, SUBLANES, GROUP_COLS), jnp.float32),
            pltpu.VMEM((tb // SUBLANES, SUBLANES, GROUP_COLS), jnp.float32),
            pltpu.VMEM((GROUPS, HEADS_PER_GROUP, HEAD_DIM, STATE), jnp.float32),
            pltpu.VMEM((GROUPS, SSM_CONV_WIDTH - 1, SUBLANES, GROUP_CONV), jnp.float32),
            pltpu.VMEM((GROUPS, CONV_WIDTH - 1, SUBLANES, SHORT_PER_GROUP), jnp.float32),
        ],
        compiler_params=_cparams(1, VMEM_LIMIT_MIXER),
        name="mixer",
    )(u, w_pairs, parts, parts, cw, cbias, scw, dskip_e, nw, e_t)


def _out_kernel(yca_ref, ycb_ref, ysa_ref, ysb_ref, ga_ref, gb_ref, x_ref, mod_ref,
                wca_ref, wcb_ref, wsa_ref, wsb_ref, wo_ref, o_ref):
    p_conv = _dot(yca_ref[...], wca_ref[...]) + _dot(ycb_ref[...], wcb_ref[...])
    p_ssm = _dot(ysa_ref[...], wsa_ref[...]) + _dot(ysb_ref[...], wsb_ref[...])
    g_conv = 1.0 / (1.0 + jnp.exp(-ga_ref[...].astype(jnp.float32)))
    g_ssm = 1.0 / (1.0 + jnp.exp(-gb_ref[...].astype(jnp.float32)))
    merged = g_conv * p_conv + g_ssm * p_ssm
    mix = _dot(merged.astype(jnp.bfloat16), wo_ref[...])
    gate1 = mod_ref[0, 2:3, :]
    o_ref[...] = x_ref[...] + gate1 * mix


def _out_call(mixed, x2, mod_l, wca, wcb, wsa, wsb, wo, seq, tm=512):
    ys_a, yc_a, g_a, ys_b, yc_b, g_b = mixed
    t_total = x2.shape[0]
    tiles_per_batch = seq // tm
    row = lambda width: pl.BlockSpec((tm, width), lambda i: (i, 0))
    const = lambda shape: pl.BlockSpec(shape, lambda i: (0, 0))
    return pl.pallas_call(
        _out_kernel,
        grid=(t_total // tm,),
        in_specs=[
            row(CONV_DIM // 2), row(CONV_DIM // 2), row(D_SSM // 2), row(D_SSM // 2),
            row(D_MODEL), row(D_MODEL), row(D_MODEL),
            pl.BlockSpec((1, MOD_ROWS, D_MODEL), lambda i: (i // tiles_per_batch, 0, 0)),
            const((CONV_DIM // 2, D_MODEL)), const((CONV_DIM // 2, D_MODEL)),
            const((D_SSM // 2, D_MODEL)), const((D_SSM // 2, D_MODEL)),
            const((D_MODEL, D_MODEL)),
        ],
        out_specs=pl.BlockSpec((tm, D_MODEL), lambda i: (i, 0)),
        out_shape=jax.ShapeDtypeStruct((t_total, D_MODEL), jnp.float32),
        compiler_params=_cparams(1),
        name="mix_out",
    )(yc_a, yc_b, ys_a, ys_b, g_a, g_b, x2, mod_l, wca, wcb, wsa, wsb, wo)


def _mlp_kernel(x_ref, mod_ref, ln_ref, wup_ref, wdn_ref, fn_ref, o_ref, u_ref, acc_ref,
                *, tm, final):
    j = pl.program_id(1)

    @pl.when(j == 0)
    def _():
        shift = mod_ref[0, 3:4, :]
        scale = mod_ref[0, 4:5, :]
        _modnorm_to(x_ref, ln_ref, scale, shift, u_ref, tm)
        acc_ref[...] = jnp.zeros_like(acc_ref)

    h = jnp.maximum(_dot(u_ref[...], wup_ref[...]), 0.0)
    acc_ref[...] += _dot((h * h).astype(jnp.bfloat16), wdn_ref[...])

    @pl.when(j == pl.num_programs(1) - 1)
    def _():
        gate2 = mod_ref[0, 5:6, :]
        xn = x_ref[...] + gate2 * acc_ref[...]
        if final:
            ms = jnp.mean(xn * xn, axis=-1, keepdims=True)
            xn = xn * lax.rsqrt(ms + EPS) * fn_ref[...]
        o_ref[...] = xn


def _mlp_call(x2, mod_l, ln, w_up, w_dn, fnorm, seq, final, tm=1024, tf=1024):
    t_total = x2.shape[0]
    tiles_per_batch = seq // tm
    return pl.pallas_call(
        functools.partial(_mlp_kernel, tm=tm, final=final),
        grid=(t_total // tm, D_FF // tf),
        in_specs=[
            pl.BlockSpec((tm, D_MODEL), lambda i, j: (i, 0)),
            pl.BlockSpec((1, MOD_ROWS, D_MODEL), lambda i, j: (i // tiles_per_batch, 0, 0)),
            pl.BlockSpec((1, D_MODEL), lambda i, j: (0, 0)),
            pl.BlockSpec((D_MODEL, tf), lambda i, j: (0, j)),
            pl.BlockSpec((tf, D_MODEL), lambda i, j: (j, 0)),
            pl.BlockSpec((1, D_MODEL), lambda i, j: (0, 0)),
        ],
        out_specs=pl.BlockSpec((tm, D_MODEL), lambda i, j: (i, 0)),
        out_shape=jax.ShapeDtypeStruct((t_total, D_MODEL), jnp.float32),
        scratch_shapes=[pltpu.VMEM((tm, D_MODEL), jnp.bfloat16),
                        pltpu.VMEM((tm, D_MODEL), jnp.float32)],
        compiler_params=_cparams(2),
        name="mlp",
    )(x2, mod_l, ln, w_up, w_dn, fnorm)


def _head_constants():
    e_t = np.zeros((PART_ROWS, 2, HEADS_PER_GROUP, LANES), np.float32)
    perm = np.zeros((GROUPS, PART_ROWS, 2 * N_SPLIT, HEADS), np.float32)
    for k in range(2 * N_SPLIT):
        for r in range(HEADS_PER_GROUP):
            e_t[k * HEADS_PER_GROUP + r, k // N_SPLIT, r, :] = 1.0
            for g in range(GROUPS):
                perm[g, k * HEADS_PER_GROUP + r, k, g * HEADS_PER_GROUP + r] = 1.0
    e_t = e_t.reshape(PART_ROWS, 2 * HEADS_PER_GROUP * LANES)
    perm = perm.reshape(GROUPS * PART_ROWS, 2 * N_SPLIT * HEADS)
    tm = _chunk_time(np.arange(SCAN_CHUNK))
    cum = (tm[None, :] <= tm[:, None]).astype(np.float32)
    bf = jnp.bfloat16
    return jnp.asarray(e_t, bf), jnp.asarray(perm, bf), jnp.asarray(cum, bf)


def _pair_major(w, n_per_half):
    lead = w.shape[:-1]
    return jnp.swapaxes(w.reshape(*lead, 2, PAIRS, n_per_half), -3, -2)


def _by_group(w, n):
    return w.reshape(*w.shape[:-1], PAIRS, 2, n)


def _regroup_w_in(w):
    pieces = [
        _by_group(w[..., OFF_XS:OFF_BS], GROUP_X),
        _by_group(w[..., OFF_BS:OFF_CS], STATE),
        _by_group(w[..., OFF_CS:OFF_DT], STATE),
        _by_group(w[..., OFF_Z:OFF_XS], GROUP_X),
        _pair_major(w[..., OFF_CB:OFF_CC], SHORT_PER_GROUP),
        _pair_major(w[..., OFF_CC:OFF_CX], SHORT_PER_GROUP),
        _pair_major(w[..., OFF_CX:OFF_Z], SHORT_PER_GROUP),
        _pair_major(w[..., 0:OFF_CB], GATE_PER_GROUP),
    ]
    out = jnp.concatenate(pieces, axis=-1).astype(jnp.bfloat16)
    return out.reshape(*w.shape[:-1], GROUPS * GROUP_COLS)


def _group_conv_params(p):
    k = p.shape[1]
    xs = p[..., :D_SSM].reshape(DEPTH, k, GROUPS, GROUP_X)
    bs = p[..., D_SSM:D_SSM + GROUPS * STATE].reshape(DEPTH, k, GROUPS, STATE)
    cs = p[..., D_SSM + GROUPS * STATE:].reshape(DEPTH, k, GROUPS, STATE)
    return jnp.swapaxes(jnp.concatenate([xs, bs, cs], axis=-1), 1, 2)


def _permute_tokens(x, bsz, seq, inverse=False):
    a, b = (SLABS, SUBLANES) if inverse else (SUBLANES, SLABS)
    x5 = x.reshape(bsz, seq // SCAN_CHUNK, a, b, D_MODEL)
    return jnp.swapaxes(x5, 2, 3).reshape(bsz * seq, D_MODEL)


def kernel(x, c, w_ada, b_ada, ln1, ln2, w_in, conv_w, ssm_conv_w, ssm_conv_b, dt_bias, a_log,
           d_skip, ssm_norm_w, w_conv_out, w_ssm_out, w_o, w_up, w_down, final_norm):
    bsz, seq, _ = x.shape
    bf = jnp.bfloat16
    f32 = jnp.float32

    c_pad = jnp.zeros((SUBLANES, D_MODEL), f32).at[:bsz].set(c)
    mod_all = _ada_call(c_pad, w_ada, b_ada)
    mod_all = mod_all[:, :bsz].reshape(DEPTH, bsz, N_MOD, D_MODEL)
    mod_all = jnp.pad(mod_all, ((0, 0), (0, 0), (0, MOD_ROWS - N_MOD), (0, 0)))

    w_pairs = _regroup_w_in(w_in)
    w_dt = jnp.pad(w_in[..., OFF_DT:], ((0, 0), (0, 0), (0, LANES - HEADS))).astype(bf)
    hp = jnp.pad(jnp.stack([dt_bias, a_log], axis=1),
                 ((0, 0), (0, SUBLANES - 2), (0, LANES - HEADS)))
    cw = _group_conv_params(ssm_conv_w)
    cbias = _group_conv_params(ssm_conv_b[:, None, :])
    scw = conv_w.reshape(DEPTH, CONV_WIDTH, 2, PAIRS, SHORT_PER_GROUP)
    scw = jnp.transpose(scw, (0, 3, 2, 1, 4)).reshape(DEPTH, GROUPS, CONV_WIDTH, SHORT_PER_GROUP)
    dskip_e = jnp.repeat(d_skip, HEAD_DIM, axis=-1).reshape(DEPTH, GROUPS, 1, GROUP_X)
    nw = ssm_norm_w.reshape(DEPTH, GROUPS, 1, GROUP_X)
    ws = w_ssm_out.astype(bf).reshape(DEPTH, PAIRS, 2, GROUP_X, D_MODEL)
    ws_a = ws[:, :, 0].reshape(DEPTH, D_SSM // 2, D_MODEL)
    ws_b = ws[:, :, 1].reshape(DEPTH, D_SSM // 2, D_MODEL)
    wc = w_conv_out.astype(bf)
    wo = w_o.astype(bf)
    wup = w_up.astype(bf)
    wdn = w_down.astype(bf)

    e_t, perm, cum = _head_constants()
    fnorm = final_norm.reshape(1, D_MODEL)
    x2 = _permute_tokens(x, bsz, seq)

    for l in range(DEPTH):
        mod_l = mod_all[l]
        u, parts = _pre_call(x2, mod_l, ln1[l].reshape(1, D_MODEL), w_dt[l], hp[l], cum, perm, seq)
        mixed = _mixer_call(u, w_pairs[l], parts, cw[l], cbias[l], scw[l], dskip_e[l], nw[l], e_t,
                            bsz, seq)
        x2 = _out_call(mixed, x2, mod_l, wc[l, :CONV_DIM // 2], wc[l, CONV_DIM // 2:],
                       ws_a[l], ws_b[l], wo[l], seq)
        x2 = _mlp_call(x2, mod_l, ln2[l].reshape(1, D_MODEL), wup[l], wdn[l], fnorm, seq,
                       final=(l == DEPTH - 1))
    return _permute_tokens(x2, bsz, seq, inverse=True).reshape(bsz, seq, D_MODEL)
```

```python
import functools

import numpy as np
import jax
import jax.numpy as jnp
from jax import lax
from jax.experimental import pallas as pl
from jax.experimental.pallas import tpu as pltpu

D_MODEL = 1024
DEPTH = 4
EPS = 1e-6
N_MOD = 6
MOD_ROWS = 8
CONV_DIM = D_MODEL
CONV_WIDTH = 3
D_SSM = 2 * D_MODEL
HEAD_DIM = 64
HEADS = D_SSM // HEAD_DIM
GROUPS = 8
PAIRS = GROUPS // 2
HEADS_PER_GROUP = HEADS // GROUPS
GROUP_X = HEADS_PER_GROUP * HEAD_DIM
STATE = 128
SSM_CONV_WIDTH = 4
D_FF = 4 * D_MODEL
LANES = 128
SUBLANES = 8
F32_TINY = float(np.finfo(np.float32).tiny)
LOG2E = float(np.log2(np.e))

OFF_CB = 2 * D_MODEL
OFF_CC = OFF_CB + CONV_DIM
OFF_CX = OFF_CC + CONV_DIM
OFF_Z = OFF_CX + CONV_DIM
OFF_XS = OFF_Z + D_SSM
OFF_BS = OFF_XS + D_SSM
OFF_CS = OFF_BS + GROUPS * STATE
OFF_DT = OFF_CS + GROUPS * STATE

SCAN_CHUNK = 128
SLABS = SCAN_CHUNK // SUBLANES
GROUP_CONV = GROUP_X + 2 * STATE
SHORT_PER_GROUP = CONV_DIM // GROUPS
GATE_PER_GROUP = 2 * D_MODEL // GROUPS
G_XBC = 0
G_Z = G_XBC + GROUP_CONV
G_CB = G_Z + GROUP_X
G_CC = G_CB + SHORT_PER_GROUP
G_CX = G_CC + SHORT_PER_GROUP
G_GATE = G_CX + SHORT_PER_GROUP
GROUP_COLS = G_GATE + GATE_PER_GROUP
PROJ_PIECE = 256
PIECES_WHILE_ODD = {0: (0,), 1: (1,), 2: (2,), 4: (3,), 5: (4,), 7: (5,)}
PIECES_WHILE_EVEN = {0: (6,), 1: (7,), 3: (8,), 4: (9,), 6: (10,)}

N_SPLIT = 3
PART_ROWS = 32
BETA_ROW0 = N_SPLIT * HEADS_PER_GROUP

VMEM_LIMIT = 48 * 1024 * 1024
VMEM_LIMIT_MIXER = 56 * 1024 * 1024


def _cparams(n_axes, limit=VMEM_LIMIT):
    return pltpu.CompilerParams(
        dimension_semantics=("arbitrary",) * n_axes,
        vmem_limit_bytes=limit)


def _dot(a, b):
    return jnp.dot(a, b, preferred_element_type=jnp.float32)


def _split3(v):
    hi = v.astype(jnp.bfloat16)
    r1 = v - hi.astype(jnp.float32)
    mid = r1.astype(jnp.bfloat16)
    lo = (r1 - mid.astype(jnp.float32)).astype(jnp.bfloat16)
    return hi, mid, lo


def _silu(v):
    return v / (1.0 + jnp.exp2(v * (-LOG2E)))


def _chunk_time(p):
    return (p % SUBLANES) * SLABS + p // SUBLANES


def _ada_kernel(c_ref, w_ref, b_ref, o_ref):
    c = c_ref[...]
    o_ref[0] = _dot(_silu(c), w_ref[0]) + b_ref[0]


def _ada_call(c_pad, w_ada, b_ada):
    n_out = N_MOD * D_MODEL
    tn = D_MODEL
    return pl.pallas_call(
        _ada_kernel,
        grid=(DEPTH, n_out // tn),
        in_specs=[
            pl.BlockSpec((SUBLANES, D_MODEL), lambda l, j: (0, 0)),
            pl.BlockSpec((1, D_MODEL, tn), lambda l, j: (l, 0, j)),
            pl.BlockSpec((1, 1, tn), lambda l, j: (l, 0, j)),
        ],
        out_specs=pl.BlockSpec((1, SUBLANES, tn), lambda l, j: (l, 0, j)),
        out_shape=jax.ShapeDtypeStruct((DEPTH, SUBLANES, n_out), jnp.float32),
        compiler_params=_cparams(2),
        name="ada_mod",
    )(c_pad, w_ada, b_ada.reshape(DEPTH, 1, n_out))


def _modnorm_to(x_ref, ln_ref, scale, shift, u_ref, rows, chunk=256):
    w = ln_ref[...] * (1.0 + scale)

    def body(r, carry):
        r0 = pl.multiple_of(r * chunk, chunk)
        xf = x_ref[pl.ds(r0, chunk), :]
        ms = jnp.mean(xf * xf, axis=-1, keepdims=True)
        y = xf * lax.rsqrt(ms + EPS)
        u_ref[pl.ds(r0, chunk), :] = (y * w + shift).astype(u_ref.dtype)
        return carry

    lax.fori_loop(0, rows // chunk, body, 0)


def _pre_kernel(x_ref, mod_ref, ln_ref, wdt_ref, hp_ref, cum_ref, perm_ref, u_ref, parts_ref,
                *, tp):
    shift = mod_ref[0, 0:1, :]
    scale = mod_ref[0, 1:2, :]
    _modnorm_to(x_ref, ln_ref, scale, shift, u_ref, tp)
    dt_raw = _dot(u_ref[...], wdt_ref[...])
    dt_bias = hp_ref[0:1, :]
    a_neg = -jnp.exp(hp_ref[1:2, :])
    cum_b = cum_ref[...]
    perm = perm_ref[...]
    q = SCAN_CHUNK
    for c in range(tp // q):
        r0 = c * q
        dtr = dt_raw[r0:r0 + q, :] + dt_bias
        dt = jnp.maximum(dtr, 0.0) + jnp.log1p(jnp.exp(-jnp.abs(dtr)))
        a_parts = _split3(dt * a_neg)
        acum = _dot(cum_b, a_parts[0]) + _dot(cum_b, a_parts[1]) + _dot(cum_b, a_parts[2])
        beta = acum - jnp.log(jnp.maximum(dt, F32_TINY))
        acum_t = (acum * LOG2E).T[0:HEADS, :]
        beta_t = (beta * LOG2E).T[0:HEADS, :]
        terms = jnp.concatenate(_split3(acum_t) + _split3(beta_t), axis=0)
        parts_ref[:, r0:r0 + q] = _dot(perm, terms).astype(parts_ref.dtype)


def _pre_call(x2, mod_l, ln, w_dt, hp, cum, perm, seq, tp=1024):
    t_total = x2.shape[0]
    tiles_per_batch = seq // tp
    return pl.pallas_call(
        functools.partial(_pre_kernel, tp=tp),
        grid=(t_total // tp,),
        in_specs=[
            pl.BlockSpec((tp, D_MODEL), lambda i: (i, 0)),
            pl.BlockSpec((1, MOD_ROWS, D_MODEL), lambda i: (i // tiles_per_batch, 0, 0)),
            pl.BlockSpec((1, D_MODEL), lambda i: (0, 0)),
            pl.BlockSpec((D_MODEL, LANES), lambda i: (0, 0)),
            pl.BlockSpec((SUBLANES, LANES), lambda i: (0, 0)),
            pl.BlockSpec((SCAN_CHUNK, SCAN_CHUNK), lambda i: (0, 0)),
            pl.BlockSpec((GROUPS * PART_ROWS, 2 * N_SPLIT * HEADS), lambda i: (0, 0)),
        ],
        out_specs=[
            pl.BlockSpec((tp, D_MODEL), lambda i: (i, 0)),
            pl.BlockSpec((GROUPS * PART_ROWS, tp), lambda i: (0, i)),
        ],
        out_shape=[
            jax.ShapeDtypeStruct((t_total, D_MODEL), jnp.bfloat16),
            jax.ShapeDtypeStruct((GROUPS * PART_ROWS, t_total), jnp.bfloat16),
        ],
        compiler_params=_cparams(1),
        name="pre",
    )(x2, mod_l, ln, w_dt, hp, cum, perm)


def _shifted_taps(cur, tail, width):
    n_wrap = width - 1
    cur_tail = cur[SLABS - n_wrap:]
    wrapped = jnp.concatenate([tail[:, SUBLANES - 1:, :], cur_tail[:, :SUBLANES - 1, :]], axis=1)
    taps = [cur]
    for s in range(1, width):
        taps.append(jnp.concatenate([wrapped[n_wrap - s:], cur[:SLABS - s]], axis=0))
    return taps, cur_tail


def _project_piece(u_ref, w_ref, pa_ref, pb_ref, piece, tb):
    lo = piece * PROJ_PIECE
    res = _dot(u_ref[...], w_ref[:, lo:lo + PROJ_PIECE]
               ).reshape(tb // SUBLANES, SUBLANES, PROJ_PIECE)
    n_a = min(max(GROUP_COLS - lo, 0), PROJ_PIECE)
    if n_a:
        pa_ref[:, :, lo:lo + n_a] = res[:, :, 0:n_a]
    if n_a < PROJ_PIECE:
        b0 = lo + n_a - GROUP_COLS
        pb_ref[:, :, b0:b0 + PROJ_PIECE - n_a] = res[:, :, n_a:]


def _mix_group(p_ref, parts_ref, g, first, cw_ref, cbias_ref, scw_ref, dskip_ref, nw_ref, e_t,
               causal, ys_ref, yc_ref, gate_ref, state_ref, tail_ref, stail_ref, tb, after_chunk):
    q = SCAN_CHUNK
    nt_dims = (((1,), (1,)), ((), ()))
    tn_dims = (((0,), (0,)), ((), ()))
    cw = cw_ref[g]
    cbias = cbias_ref[g]
    scw = scw_ref[g]
    dskip = dskip_ref[g]
    nw = nw_ref[g]

    gate_ref[...] = p_ref[:, :, G_GATE:GROUP_COLS].reshape(tb, GATE_PER_GROUP).astype(gate_ref.dtype)

    state_ref[g] = jnp.where(first, 0.0, state_ref[g])
    tail = jnp.where(first, 0.0, tail_ref[g])
    stail = jnp.where(first, 0.0, stail_ref[g])

    for c in range(tb // q):
        r0 = c * q
        b0 = c * SLABS
        sv = p_ref[b0:b0 + SLABS, :, G_CC:G_CX] * p_ref[b0:b0 + SLABS, :, G_CX:G_GATE]
        staps, stail = _shifted_taps(sv, stail, CONV_WIDTH)
        sconv = scw[CONV_WIDTH - 1:CONV_WIDTH, :] * staps[0]
        for s in range(1, CONV_WIDTH):
            sconv += scw[CONV_WIDTH - 1 - s:CONV_WIDTH - s, :] * staps[s]
        yc = p_ref[b0:b0 + SLABS, :, G_CB:G_CC] * sconv
        yc_ref[r0:r0 + q, :] = yc.reshape(q, SHORT_PER_GROUP).astype(yc_ref.dtype)

        taps, tail = _shifted_taps(p_ref[b0:b0 + SLABS, :, G_XBC:G_Z], tail, SSM_CONV_WIDTH)
        acc = cbias + cw[SSM_CONV_WIDTH - 1:SSM_CONV_WIDTH, :] * taps[0]
        for s in range(1, SSM_CONV_WIDTH):
            acc += cw[SSM_CONV_WIDTH - 1 - s:SSM_CONV_WIDTH - s, :] * taps[s]
        xbc = _silu(acc).reshape(q, GROUP_CONV)
        xs = xbc[:, 0:GROUP_X]
        xt_b = xs.astype(jnp.bfloat16).T
        b_b = xbc[:, GROUP_X:GROUP_X + STATE].astype(jnp.bfloat16)
        c_b = xbc[:, GROUP_X + STATE:GROUP_CONV].astype(jnp.bfloat16)

        mc = parts_ref[:, r0:r0 + q]
        ab_l = lax.dot_general(mc, e_t, tn_dims, preferred_element_type=jnp.float32)
        acum_l = ab_l[:, 0:HEADS_PER_GROUP * LANES]
        beta_l = ab_l[:, HEADS_PER_GROUP * LANES:]
        acum_last = acum_l[q - 1:q, :]
        mf = mc.astype(jnp.float32)

        scores_b = lax.dot_general(c_b, b_b, nt_dims, preferred_element_type=jnp.float32
                                   ).astype(jnp.bfloat16)
        ys = []
        for r in range(HEADS_PER_GROUP):
            al = acum_l[:, r * LANES:(r + 1) * LANES]
            bl = beta_l[:, r * LANES:(r + 1) * LANES]
            last = acum_last[:, r * LANES:(r + 1) * LANES]
            beta_row = (mf[BETA_ROW0 + r:BETA_ROW0 + r + 1, :]
                        + mf[BETA_ROW0 + HEADS_PER_GROUP + r:BETA_ROW0 + HEADS_PER_GROUP + r + 1, :]
                        + mf[BETA_ROW0 + 2 * HEADS_PER_GROUP + r:
                             BETA_ROW0 + 2 * HEADS_PER_GROUP + r + 1, :])
            seg = (al - beta_row).astype(jnp.bfloat16)
            w_r = scores_b * jnp.exp2(jnp.where(causal, seg, -jnp.inf))
            c_r = c_b * jnp.exp2(al).astype(jnp.bfloat16)
            b_r = b_b * jnp.exp2(last - bl).astype(jnp.bfloat16)
            xt_r = xt_b[r * HEAD_DIM:(r + 1) * HEAD_DIM, :]
            s_r = state_ref[g, r]
            ys.append(lax.dot_general(
                jnp.concatenate([w_r, c_r], axis=1),
                jnp.concatenate([xt_r, s_r.astype(jnp.bfloat16)], axis=1),
                nt_dims, preferred_element_type=jnp.float32))
            state_ref[g, r] = s_r * jnp.exp2(last) + _dot(xt_r, b_r)
        y = jnp.concatenate(ys, axis=-1)

        y = y + dskip * xs
        y = y * _silu(p_ref[b0:b0 + SLABS, :, G_Z:G_CB].reshape(q, GROUP_X))
        y = y * lax.rsqrt(jnp.mean(y * y, axis=-1, keepdims=True) + EPS)
        ys_ref[r0:r0 + q, :] = (y * nw).astype(ys_ref.dtype)
        after_chunk(c)

    tail_ref[g] = tail
    stail_ref[g] = stail


def _mixer_kernel(u_ref, w_ref, parts_a_ref, parts_b_ref,
                  cw_ref, cbias_ref, scw_ref, dskip_ref, nw_ref, et_ref,
                  ys_a_ref, yc_a_ref, gate_a_ref, ys_b_ref, yc_b_ref, gate_b_ref,
                  pa_ref, pb_ref, state_ref, tail_ref, stail_ref, *, tb, n_items, nt):
    s = pl.program_id(0)
    q = SCAN_CHUNK

    @pl.when(s == 0)
    def _():
        pb_ref[...] = jnp.zeros_like(pb_ref)
        state_ref[...] = jnp.zeros_like(state_ref)
        tail_ref[...] = jnp.zeros_like(tail_ref)
        stail_ref[...] = jnp.zeros_like(stail_ref)

    item_a = jnp.minimum(s, n_items - 1)
    item_b = jnp.maximum(s - 1, 0)
    g_a = 2 * (item_a % PAIRS)
    g_b = 2 * (item_b % PAIRS) + 1
    first_a = (item_a // PAIRS) % nt == 0
    first_b = (item_b // PAIRS) % nt == 0

    e_t = et_ref[...]
    t_row = _chunk_time(lax.broadcasted_iota(jnp.int32, (q, q), 0))
    t_col = _chunk_time(lax.broadcasted_iota(jnp.int32, (q, q), 1))
    causal = t_row >= t_col
    common = (cw_ref, cbias_ref, scw_ref, dskip_ref, nw_ref, e_t, causal)
    hist = (state_ref, tail_ref, stail_ref, tb)

    def project_after(schedule):
        def after_chunk(c):
            for piece in schedule.get(c, ()):
                _project_piece(u_ref, w_ref, pa_ref, pb_ref, piece, tb)
        return after_chunk

    _mix_group(pb_ref, parts_b_ref, g_b, first_b, *common, ys_b_ref, yc_b_ref, gate_b_ref, *hist,
               project_after(PIECES_WHILE_ODD))
    _mix_group(pa_ref, parts_a_ref, g_a, first_a, *common, ys_a_ref, yc_a_ref, gate_a_ref, *hist,
               project_after(PIECES_WHILE_EVEN))


def _mixer_call(u, w_pairs, parts, cw, cbias, scw, dskip_e, nw, e_t, bsz, seq, tb=1024):
    t_total = u.shape[0]
    nt = seq // tb
    n_blocks = t_total // tb
    n_items = n_blocks * PAIRS
    rows_out = t_total + tb

    item_a = lambda s: jnp.minimum(s, n_items - 1)
    item_b = lambda s: jnp.maximum(s - 1, 0)
    out_a = lambda s: (jnp.where(s < n_items, item_a(s) // PAIRS, n_blocks), item_a(s) % PAIRS)
    out_b = lambda s: (jnp.where(s >= 1, item_b(s) // PAIRS, n_blocks), item_b(s) % PAIRS)

    def full(arr):
        nd = arr.ndim
        return pl.BlockSpec(arr.shape, lambda s: (0,) * nd)

    half = GROUPS // 2
    out_shapes = [
        jax.ShapeDtypeStruct((rows_out, half * GROUP_X), jnp.bfloat16),
        jax.ShapeDtypeStruct((rows_out, half * SHORT_PER_GROUP), jnp.bfloat16),
        jax.ShapeDtypeStruct((rows_out, half * GATE_PER_GROUP), jnp.bfloat16),
    ]
    widths = (GROUP_X, SHORT_PER_GROUP, GATE_PER_GROUP)
    return pl.pallas_call(
        functools.partial(_mixer_kernel, tb=tb, n_items=n_items, nt=nt),
        grid=(n_items + 1,),
        in_specs=[
            pl.BlockSpec((tb, D_MODEL), lambda s: (item_a(s) // PAIRS, 0)),
            pl.BlockSpec((D_MODEL, 2 * GROUP_COLS), lambda s: (0, item_a(s) % PAIRS)),
            pl.BlockSpec((PART_ROWS, tb), lambda s: (2 * (item_a(s) % PAIRS), item_a(s) // PAIRS)),
            pl.BlockSpec((PART_ROWS, tb),
                         lambda s: (2 * (item_b(s) % PAIRS) + 1, item_b(s) // PAIRS)),
            full(cw), full(cbias), full(scw), full(dskip_e), full(nw), full(e_t),
        ],
        out_specs=([pl.BlockSpec((tb, w), out_a) for w in widths]
                   + [pl.BlockSpec((tb, w), out_b) for w in widths]),
        out_shape=out_shapes + out_shapes,
        scratch_shapes=[
            pltpu.VMEM((tb // SUBLANES, SUBLANES, GROUP_COLS), jnp.float32),
            pltpu.VMEM((tb // SUBLANES, SUBLANES, GROUP_COLS), jnp.float32),
            pltpu.VMEM((GROUPS, HEADS_PER_GROUP, HEAD_DIM, STATE), jnp.float32),
            pltpu.VMEM((GROUPS, SSM_CONV_WIDTH - 1, SUBLANES, GROUP_CONV), jnp.float32),
            pltpu.VMEM((GROUPS, CONV_WIDTH - 1, SUBLANES, SHORT_PER_GROUP), jnp.float32),
        ],
        compiler_params=_cparams(1, VMEM_LIMIT_MIXER),
        name="mixer",
    )(u, w_pairs, parts, parts, cw, cbias, scw, dskip_e, nw, e_t)


def _out_kernel(yca_ref, ycb_ref, ysa_ref, ysb_ref, ga_ref, gb_ref, x_ref, mod_ref,
                wc_ref, ws_ref, wo_ref, o_ref):
    p_conv = _dot(jnp.concatenate([yca_ref[...], ycb_ref[...]], axis=1), wc_ref[...])
    y_ssm = jnp.concatenate(
        [ref[:, i * GROUP_X:(i + 1) * GROUP_X] for i in range(PAIRS) for ref in (ysa_ref, ysb_ref)],
        axis=1)
    p_ssm = _dot(y_ssm, ws_ref[...])
    g_conv = 1.0 / (1.0 + jnp.exp(-ga_ref[...].astype(jnp.float32)))
    g_ssm = 1.0 / (1.0 + jnp.exp(-gb_ref[...].astype(jnp.float32)))
    merged = g_conv * p_conv + g_ssm * p_ssm
    mix = _dot(merged.astype(jnp.bfloat16), wo_ref[...])
    gate1 = mod_ref[0, 2:3, :]
    o_ref[...] = x_ref[...] + gate1 * mix


def _out_call(mixed, x2, mod_l, wc, ws, wo, seq, tm=512):
    ys_a, yc_a, g_a, ys_b, yc_b, g_b = mixed
    t_total = x2.shape[0]
    tiles_per_batch = seq // tm
    row = lambda width: pl.BlockSpec((tm, width), lambda i: (i, 0))
    const = lambda shape: pl.BlockSpec(shape, lambda i: (0, 0))
    return pl.pallas_call(
        _out_kernel,
        grid=(t_total // tm,),
        in_specs=[
            row(CONV_DIM // 2), row(CONV_DIM // 2), row(D_SSM // 2), row(D_SSM // 2),
            row(D_MODEL), row(D_MODEL), row(D_MODEL),
            pl.BlockSpec((1, MOD_ROWS, D_MODEL), lambda i: (i // tiles_per_batch, 0, 0)),
            const((CONV_DIM, D_MODEL)), const((D_SSM, D_MODEL)), const((D_MODEL, D_MODEL)),
        ],
        out_specs=pl.BlockSpec((tm, D_MODEL), lambda i: (i, 0)),
        out_shape=jax.ShapeDtypeStruct((t_total, D_MODEL), jnp.float32),
        compiler_params=_cparams(1),
        name="mix_out",
    )(yc_a, yc_b, ys_a, ys_b, g_a, g_b, x2, mod_l, wc, ws, wo)


def _mlp_kernel(x_ref, mod_ref, ln_ref, wup_ref, wdn_ref, fn_ref, o_ref, u_ref, acc_ref,
                *, tm, final):
    j = pl.program_id(1)

    @pl.when(j == 0)
    def _():
        shift = mod_ref[0, 3:4, :]
        scale = mod_ref[0, 4:5, :]
        _modnorm_to(x_ref, ln_ref, scale, shift, u_ref, tm)
        acc_ref[...] = jnp.zeros_like(acc_ref)

    h = jnp.maximum(_dot(u_ref[...], wup_ref[...]), 0.0)
    acc_ref[...] += _dot((h * h).astype(jnp.bfloat16), wdn_ref[...])

    @pl.when(j == pl.num_programs(1) - 1)
    def _():
        gate2 = mod_ref[0, 5:6, :]
        xn = x_ref[...] + gate2 * acc_ref[...]
        if final:
            ms = jnp.mean(xn * xn, axis=-1, keepdims=True)
            xn = xn * lax.rsqrt(ms + EPS) * fn_ref[...]
        o_ref[...] = xn


def _mlp_call(x2, mod_l, ln, w_up, w_dn, fnorm, seq, final, tm=1024, tf=1024):
    t_total = x2.shape[0]
    tiles_per_batch = seq // tm
    return pl.pallas_call(
        functools.partial(_mlp_kernel, tm=tm, final=final),
        grid=(t_total // tm, D_FF // tf),
        in_specs=[
            pl.BlockSpec((tm, D_MODEL), lambda i, j: (i, 0)),
            pl.BlockSpec((1, MOD_ROWS, D_MODEL), lambda i, j: (i // tiles_per_batch, 0, 0)),
            pl.BlockSpec((1, D_MODEL), lambda i, j: (0, 0)),
            pl.BlockSpec((D_MODEL, tf), lambda i, j: (0, j)),
            pl.BlockSpec((tf, D_MODEL), lambda i, j: (j, 0)),
            pl.BlockSpec((1, D_MODEL), lambda i, j: (0, 0)),
        ],
        out_specs=pl.BlockSpec((tm, D_MODEL), lambda i, j: (i, 0)),
        out_shape=jax.ShapeDtypeStruct((t_total, D_MODEL), jnp.float32),
        scratch_shapes=[pltpu.VMEM((tm, D_MODEL), jnp.bfloat16),
                        pltpu.VMEM((tm, D_MODEL), jnp.float32)],
        compiler_params=_cparams(2),
        name="mlp",
    )(x2, mod_l, ln, w_up, w_dn, fnorm)


def _head_constants():
    e_t = np.zeros((PART_ROWS, 2, HEADS_PER_GROUP, LANES), np.float32)
    perm = np.zeros((GROUPS, PART_ROWS, 2 * N_SPLIT, HEADS), np.float32)
    for k in range(2 * N_SPLIT):
        for r in range(HEADS_PER_GROUP):
            e_t[k * HEADS_PER_GROUP + r, k // N_SPLIT, r, :] = 1.0
            for g in range(GROUPS):
                perm[g, k * HEADS_PER_GROUP + r, k, g * HEADS_PER_GROUP + r] = 1.0
    e_t = e_t.reshape(PART_ROWS, 2 * HEADS_PER_GROUP * LANES)
    perm = perm.reshape(GROUPS * PART_ROWS, 2 * N_SPLIT * HEADS)
    tm = _chunk_time(np.arange(SCAN_CHUNK))
    cum = (tm[None, :] <= tm[:, None]).astype(np.float32)
    bf = jnp.bfloat16
    return jnp.asarray(e_t, bf), jnp.asarray(perm, bf), jnp.asarray(cum, bf)


def _regroup_kernel(xs_ref, bs_ref, cs_ref, z_ref, cb_ref, cc_ref, cx_ref, gate_ref, o_ref):
    for ref, lo, hi in ((xs_ref, G_XBC, G_XBC + GROUP_X), (bs_ref, G_XBC + GROUP_X, G_Z - STATE),
                        (cs_ref, G_Z - STATE, G_Z), (z_ref, G_Z, G_CB), (cb_ref, G_CB, G_CC),
                        (cc_ref, G_CC, G_CX), (cx_ref, G_CX, G_GATE), (gate_ref, G_GATE, GROUP_COLS)):
        o_ref[0, :, lo:hi] = ref[0].astype(o_ref.dtype)


def _regroup_w_in(w):
    def cols(width, off, paired):
        base = off // width
        if paired:
            return pl.BlockSpec((1, D_MODEL, width),
                                lambda l, g: (l, 0, base + (g % 2) * PAIRS + g // 2))
        return pl.BlockSpec((1, D_MODEL, width), lambda l, g: (l, 0, base + g))

    return pl.pallas_call(
        _regroup_kernel,
        grid=(DEPTH, GROUPS),
        in_specs=[
            cols(GROUP_X, OFF_XS, False), cols(STATE, OFF_BS, False), cols(STATE, OFF_CS, False),
            cols(GROUP_X, OFF_Z, False),
            cols(SHORT_PER_GROUP, OFF_CB, True), cols(SHORT_PER_GROUP, OFF_CC, True),
            cols(SHORT_PER_GROUP, OFF_CX, True), cols(GATE_PER_GROUP, 0, True),
        ],
        out_specs=pl.BlockSpec((1, D_MODEL, GROUP_COLS), lambda l, g: (l, 0, g)),
        out_shape=jax.ShapeDtypeStruct((DEPTH, D_MODEL, GROUPS * GROUP_COLS), jnp.bfloat16),
        compiler_params=_cparams(2),
        name="w_in_regroup",
    )(w, w, w, w, w, w, w, w)


def _group_conv_params(p):
    k = p.shape[1]
    xs = p[..., :D_SSM].reshape(DEPTH, k, GROUPS, GROUP_X)
    bs = p[..., D_SSM:D_SSM + GROUPS * STATE].reshape(DEPTH, k, GROUPS, STATE)
    cs = p[..., D_SSM + GROUPS * STATE:].reshape(DEPTH, k, GROUPS, STATE)
    return jnp.swapaxes(jnp.concatenate([xs, bs, cs], axis=-1), 1, 2)


def _permute_tokens(x, bsz, seq, inverse=False):
    a, b = (SLABS, SUBLANES) if inverse else (SUBLANES, SLABS)
    x5 = x.reshape(bsz, seq // SCAN_CHUNK, a, b, D_MODEL)
    return jnp.swapaxes(x5, 2, 3).reshape(bsz * seq, D_MODEL)


def kernel(x, c, w_ada, b_ada, ln1, ln2, w_in, conv_w, ssm_conv_w, ssm_conv_b, dt_bias, a_log,
           d_skip, ssm_norm_w, w_conv_out, w_ssm_out, w_o, w_up, w_down, final_norm):
    bsz, seq, _ = x.shape
    bf = jnp.bfloat16
    f32 = jnp.float32

    c_pad = jnp.zeros((SUBLANES, D_MODEL), f32).at[:bsz].set(c)
    mod_all = _ada_call(c_pad, w_ada, b_ada)
    mod_all = mod_all[:, :bsz].reshape(DEPTH, bsz, N_MOD, D_MODEL)
    mod_all = jnp.pad(mod_all, ((0, 0), (0, 0), (0, MOD_ROWS - N_MOD), (0, 0)))

    w_pairs = _regroup_w_in(w_in)
    w_dt = jnp.pad(w_in[..., OFF_DT:], ((0, 0), (0, 0), (0, LANES - HEADS))).astype(bf)
    hp = jnp.pad(jnp.stack([dt_bias, a_log], axis=1),
                 ((0, 0), (0, SUBLANES - 2), (0, LANES - HEADS)))
    cw = _group_conv_params(ssm_conv_w)
    cbias = _group_conv_params(ssm_conv_b[:, None, :])
    scw = conv_w.reshape(DEPTH, CONV_WIDTH, 2, PAIRS, SHORT_PER_GROUP)
    scw = jnp.transpose(scw, (0, 3, 2, 1, 4)).reshape(DEPTH, GROUPS, CONV_WIDTH, SHORT_PER_GROUP)
    dskip_e = jnp.repeat(d_skip, HEAD_DIM, axis=-1).reshape(DEPTH, GROUPS, 1, GROUP_X)
    nw = ssm_norm_w.reshape(DEPTH, GROUPS, 1, GROUP_X)
    ws = w_ssm_out.astype(bf)
    wc = w_conv_out.astype(bf)
    wo = w_o.astype(bf)
    wup = w_up.astype(bf)
    wdn = w_down.astype(bf)

    e_t, perm, cum = _head_constants()
    fnorm = final_norm.reshape(1, D_MODEL)
    x2 = _permute_tokens(x, bsz, seq)

    for l in range(DEPTH):
        mod_l = mod_all[l]
        u, parts = _pre_call(x2, mod_l, ln1[l].reshape(1, D_MODEL), w_dt[l], hp[l], cum, perm, seq)
        mixed = _mixer_call(u, w_pairs[l], parts, cw[l], cbias[l], scw[l], dskip_e[l], nw[l], e_t,
                            bsz, seq)
        x2 = _out_call(mixed, x2, mod_l, wc[l], ws[l], wo[l], seq)
        x2 = _mlp_call(x2, mod_l, ln2[l].reshape(1, D_MODEL), wup[l], wdn[l], fnorm, seq,
                       final=(l == DEPTH - 1))
    return _permute_tokens(x2, bsz, seq, inverse=True).reshape(bsz, seq, D_MODEL)
```

```python
import functools

import numpy as np
import jax
import jax.numpy as jnp
from jax import lax
from jax.experimental import pallas as pl
from jax.experimental.pallas import tpu as pltpu

D_MODEL = 1024
DEPTH = 4
EPS = 1e-6
N_MOD = 6
MOD_ROWS = 8
CONV_DIM = D_MODEL
CONV_WIDTH = 3
D_SSM = 2 * D_MODEL
HEAD_DIM = 64
HEADS = D_SSM // HEAD_DIM
GROUPS = 8
PAIRS = GROUPS // 2
HEADS_PER_GROUP = HEADS // GROUPS
GROUP_X = HEADS_PER_GROUP * HEAD_DIM
STATE = 128
SSM_CONV_WIDTH = 4
D_FF = 4 * D_MODEL
LANES = 128
SUBLANES = 8
F32_TINY = float(np.finfo(np.float32).tiny)
LOG2E = float(np.log2(np.e))

OFF_CB = 2 * D_MODEL
OFF_CC = OFF_CB + CONV_DIM
OFF_CX = OFF_CC + CONV_DIM
OFF_Z = OFF_CX + CONV_DIM
OFF_XS = OFF_Z + D_SSM
OFF_BS = OFF_XS + D_SSM
OFF_CS = OFF_BS + GROUPS * STATE
OFF_DT = OFF_CS + GROUPS * STATE

SCAN_CHUNK = 128
SLABS = SCAN_CHUNK // SUBLANES
GROUP_CONV = GROUP_X + 2 * STATE
SHORT_PER_GROUP = CONV_DIM // GROUPS
GATE_PER_GROUP = 2 * D_MODEL // GROUPS
G_XBC = 0
G_Z = G_XBC + GROUP_CONV
G_CB = G_Z + GROUP_X
G_CC = G_CB + SHORT_PER_GROUP
G_CX = G_CC + SHORT_PER_GROUP
G_GATE = G_CX + SHORT_PER_GROUP
GROUP_COLS = G_GATE + GATE_PER_GROUP
PROJ_PIECE = 256
PIECES_WHILE_ODD = {0: (0,), 1: (1,), 2: (2,), 4: (3,), 5: (4,), 7: (5,)}
PIECES_WHILE_EVEN = {0: (6,), 1: (7,), 3: (8,), 4: (9,), 6: (10,)}

N_SPLIT = 3
PART_ROWS = 32
BETA_ROW0 = N_SPLIT * HEADS_PER_GROUP

VMEM_LIMIT = 48 * 1024 * 1024
VMEM_LIMIT_MIXER = 56 * 1024 * 1024


def _cparams(n_axes, limit=VMEM_LIMIT):
    return pltpu.CompilerParams(
        dimension_semantics=("arbitrary",) * n_axes,
        vmem_limit_bytes=limit)


def _dot(a, b):
    return jnp.dot(a, b, preferred_element_type=jnp.float32)


def _split3(v):
    hi = v.astype(jnp.bfloat16)
    r1 = v - hi.astype(jnp.float32)
    mid = r1.astype(jnp.bfloat16)
    lo = (r1 - mid.astype(jnp.float32)).astype(jnp.bfloat16)
    return hi, mid, lo


def _silu(v):
    return v / (1.0 + jnp.exp2(v * (-LOG2E)))


def _chunk_time(p):
    return (p % SUBLANES) * SLABS + p // SUBLANES


def _ada_kernel(c_ref, w_ref, b_ref, o_ref):
    c = c_ref[...]
    o_ref[0] = _dot(_silu(c), w_ref[0]) + b_ref[0]


def _ada_call(c_pad, w_ada, b_ada):
    n_out = N_MOD * D_MODEL
    tn = D_MODEL
    return pl.pallas_call(
        _ada_kernel,
        grid=(DEPTH, n_out // tn),
        in_specs=[
            pl.BlockSpec((SUBLANES, D_MODEL), lambda l, j: (0, 0)),
            pl.BlockSpec((1, D_MODEL, tn), lambda l, j: (l, 0, j)),
            pl.BlockSpec((1, 1, tn), lambda l, j: (l, 0, j)),
        ],
        out_specs=pl.BlockSpec((1, SUBLANES, tn), lambda l, j: (l, 0, j)),
        out_shape=jax.ShapeDtypeStruct((DEPTH, SUBLANES, n_out), jnp.float32),
        compiler_params=_cparams(2),
        name="ada_mod",
    )(c_pad, w_ada, b_ada.reshape(DEPTH, 1, n_out))


def _modnorm_to(x_ref, ln_ref, scale, shift, u_ref, rows, chunk=256):
    w = ln_ref[...] * (1.0 + scale)

    def body(r, carry):
        r0 = pl.multiple_of(r * chunk, chunk)
        xf = x_ref[pl.ds(r0, chunk), :]
        ms = jnp.mean(xf * xf, axis=-1, keepdims=True)
        y = xf * lax.rsqrt(ms + EPS)
        u_ref[pl.ds(r0, chunk), :] = (y * w + shift).astype(u_ref.dtype)
        return carry

    lax.fori_loop(0, rows // chunk, body, 0)


def _pre_kernel(x_ref, mod_ref, ln_ref, wdt_ref, hp_ref, cum_ref, perm_ref, u_ref, parts_ref,
                *, tp):
    shift = mod_ref[0, 0:1, :]
    scale = mod_ref[0, 1:2, :]
    _modnorm_to(x_ref, ln_ref, scale, shift, u_ref, tp)
    dt_raw = _dot(u_ref[...], wdt_ref[...])
    dt_bias = hp_ref[0:1, :]
    a_neg = -jnp.exp(hp_ref[1:2, :])
    cum_b = cum_ref[...]
    perm = perm_ref[...]
    q = SCAN_CHUNK
    for c in range(tp // q):
        r0 = c * q
        dtr = dt_raw[r0:r0 + q, :] + dt_bias
        dt = jnp.maximum(dtr, 0.0) + jnp.log1p(jnp.exp(-jnp.abs(dtr)))
        a_parts = _split3(dt * a_neg)
        acum = _dot(cum_b, a_parts[0]) + _dot(cum_b, a_parts[1]) + _dot(cum_b, a_parts[2])
        beta = acum - jnp.log(jnp.maximum(dt, F32_TINY))
        acum_t = (acum * LOG2E).T[0:HEADS, :]
        beta_t = (beta * LOG2E).T[0:HEADS, :]
        terms = jnp.concatenate(_split3(acum_t) + _split3(beta_t), axis=0)
        parts_ref[:, r0:r0 + q] = _dot(perm, terms).astype(parts_ref.dtype)


def _pre_call(x2, mod_l, ln, w_dt, hp, cum, perm, seq, tp=1024):
    t_total = x2.shape[0]
    tiles_per_batch = seq // tp
    return pl.pallas_call(
        functools.partial(_pre_kernel, tp=tp),
        grid=(t_total // tp,),
        in_specs=[
            pl.BlockSpec((tp, D_MODEL), lambda i: (i, 0)),
            pl.BlockSpec((1, MOD_ROWS, D_MODEL), lambda i: (i // tiles_per_batch, 0, 0)),
            pl.BlockSpec((1, D_MODEL), lambda i: (0, 0)),
            pl.BlockSpec((D_MODEL, LANES), lambda i: (0, 0)),
            pl.BlockSpec((SUBLANES, LANES), lambda i: (0, 0)),
            pl.BlockSpec((SCAN_CHUNK, SCAN_CHUNK), lambda i: (0, 0)),
            pl.BlockSpec((GROUPS * PART_ROWS, 2 * N_SPLIT * HEADS), lambda i: (0, 0)),
        ],
        out_specs=[
            pl.BlockSpec((tp, D_MODEL), lambda i: (i, 0)),
            pl.BlockSpec((GROUPS * PART_ROWS, tp), lambda i: (0, i)),
        ],
        out_shape=[
            jax.ShapeDtypeStruct((t_total, D_MODEL), jnp.bfloat16),
            jax.ShapeDtypeStruct((GROUPS * PART_ROWS, t_total), jnp.bfloat16),
        ],
        compiler_params=_cparams(1),
        name="pre",
    )(x2, mod_l, ln, w_dt, hp, cum, perm)


def _shifted_taps(cur, tail, width):
    n_wrap = width - 1
    cur_tail = cur[SLABS - n_wrap:]
    wrapped = jnp.concatenate([tail[:, SUBLANES - 1:, :], cur_tail[:, :SUBLANES - 1, :]], axis=1)
    taps = [cur]
    for s in range(1, width):
        taps.append(jnp.concatenate([wrapped[n_wrap - s:], cur[:SLABS - s]], axis=0))
    return taps, cur_tail


def _project_piece(u_ref, w_ref, pa_ref, pb_ref, piece, tb):
    lo = piece * PROJ_PIECE
    res = _dot(u_ref[...], w_ref[0, :, lo:lo + PROJ_PIECE]
               ).reshape(tb // SUBLANES, SUBLANES, PROJ_PIECE)
    n_a = min(max(GROUP_COLS - lo, 0), PROJ_PIECE)
    if n_a:
        pa_ref[:, :, lo:lo + n_a] = res[:, :, 0:n_a]
    if n_a < PROJ_PIECE:
        b0 = lo + n_a - GROUP_COLS
        pb_ref[:, :, b0:b0 + PROJ_PIECE - n_a] = res[:, :, n_a:]


def _mix_group(p_ref, parts_ref, g, first, cw_ref, cbias_ref, scw_ref, dskip_ref, nw_ref, e_t,
               causal, ys_ref, yc_ref, gate_ref, state_ref, tail_ref, stail_ref, tb, after_chunk):
    q = SCAN_CHUNK
    nt_dims = (((1,), (1,)), ((), ()))
    tn_dims = (((0,), (0,)), ((), ()))
    cw = cw_ref[g]
    cbias = cbias_ref[g]
    scw = scw_ref[g]
    dskip = dskip_ref[g]
    nw = nw_ref[g]

    gate_ref[...] = p_ref[:, :, G_GATE:GROUP_COLS].reshape(tb, GATE_PER_GROUP).astype(gate_ref.dtype)

    state_ref[g] = jnp.where(first, 0.0, state_ref[g])
    tail = jnp.where(first, 0.0, tail_ref[g])
    stail = jnp.where(first, 0.0, stail_ref[g])

    for c in range(tb // q):
        r0 = c * q
        b0 = c * SLABS
        sv = p_ref[b0:b0 + SLABS, :, G_CC:G_CX] * p_ref[b0:b0 + SLABS, :, G_CX:G_GATE]
        staps, stail = _shifted_taps(sv, stail, CONV_WIDTH)
        sconv = scw[CONV_WIDTH - 1:CONV_WIDTH, :] * staps[0]
        for s in range(1, CONV_WIDTH):
            sconv += scw[CONV_WIDTH - 1 - s:CONV_WIDTH - s, :] * staps[s]
        yc = p_ref[b0:b0 + SLABS, :, G_CB:G_CC] * sconv
        yc_ref[r0:r0 + q, :] = yc.reshape(q, SHORT_PER_GROUP).astype(yc_ref.dtype)

        taps, tail = _shifted_taps(p_ref[b0:b0 + SLABS, :, G_XBC:G_Z], tail, SSM_CONV_WIDTH)
        acc = cbias + cw[SSM_CONV_WIDTH - 1:SSM_CONV_WIDTH, :] * taps[0]
        for s in range(1, SSM_CONV_WIDTH):
            acc += cw[SSM_CONV_WIDTH - 1 - s:SSM_CONV_WIDTH - s, :] * taps[s]
        xbc = _silu(acc).reshape(q, GROUP_CONV)
        xs = xbc[:, 0:GROUP_X]
        xt_b = xs.astype(jnp.bfloat16).T
        b_b = xbc[:, GROUP_X:GROUP_X + STATE].astype(jnp.bfloat16)
        c_b = xbc[:, GROUP_X + STATE:GROUP_CONV].astype(jnp.bfloat16)

        mc = parts_ref[:, r0:r0 + q]
        acum_l = lax.dot_general(mc, e_t, tn_dims, preferred_element_type=jnp.float32)
        acum_last = acum_l[q - 1:q, :]
        mf = mc.astype(jnp.float32)

        scores_b = lax.dot_general(c_b, b_b, nt_dims, preferred_element_type=jnp.float32
                                   ).astype(jnp.bfloat16)
        ys = []
        for r in range(HEADS_PER_GROUP):
            al = acum_l[:, r * LANES:(r + 1) * LANES]
            last = acum_last[:, r * LANES:(r + 1) * LANES]
            beta_row = (mf[BETA_ROW0 + r:BETA_ROW0 + r + 1, :]
                        + mf[BETA_ROW0 + HEADS_PER_GROUP + r:BETA_ROW0 + HEADS_PER_GROUP + r + 1, :]
                        + mf[BETA_ROW0 + 2 * HEADS_PER_GROUP + r:
                             BETA_ROW0 + 2 * HEADS_PER_GROUP + r + 1, :])
            seg = (al - beta_row).astype(jnp.bfloat16)
            w_r = scores_b * jnp.exp2(jnp.where(causal, seg, -jnp.inf))
            c_r = c_b * jnp.exp2(al).astype(jnp.bfloat16)
            xt_r = xt_b[r * HEAD_DIM:(r + 1) * HEAD_DIM, :]
            xw_r = xt_r * jnp.exp2(last - beta_row).astype(jnp.bfloat16)
            s_r = state_ref[g, r]
            ys.append(lax.dot_general(
                jnp.concatenate([w_r, c_r], axis=1),
                jnp.concatenate([xt_r, s_r.astype(jnp.bfloat16)], axis=1),
                nt_dims, preferred_element_type=jnp.float32))
            state_ref[g, r] = s_r * jnp.exp2(last) + _dot(xw_r, b_b)
        y = jnp.concatenate(ys, axis=-1)

        y = y + dskip * xs
        y = y * _silu(p_ref[b0:b0 + SLABS, :, G_Z:G_CB].reshape(q, GROUP_X))
        y = y * lax.rsqrt(jnp.mean(y * y, axis=-1, keepdims=True) + EPS)
        ys_ref[r0:r0 + q, :] = (y * nw).astype(ys_ref.dtype)
        after_chunk(c)

    tail_ref[g] = tail
    stail_ref[g] = stail


def _mixer_kernel(u_ref, w_ref, parts_a_ref, parts_b_ref,
                  cw_ref, cbias_ref, scw_ref, dskip_ref, nw_ref, et_ref,
                  ys_a_ref, yc_a_ref, gate_a_ref, ys_b_ref, yc_b_ref, gate_b_ref,
                  pa_ref, pb_ref, state_ref, tail_ref, stail_ref, *, tb, n_items, nt):
    s = pl.program_id(0)
    q = SCAN_CHUNK

    @pl.when(s == 0)
    def _():
        pb_ref[...] = jnp.zeros_like(pb_ref)
        state_ref[...] = jnp.zeros_like(state_ref)
        tail_ref[...] = jnp.zeros_like(tail_ref)
        stail_ref[...] = jnp.zeros_like(stail_ref)

    item_a = jnp.minimum(s, n_items - 1)
    item_b = jnp.maximum(s - 1, 0)
    g_a = 2 * (item_a % PAIRS)
    g_b = 2 * (item_b % PAIRS) + 1
    first_a = (item_a // PAIRS) % nt == 0
    first_b = (item_b // PAIRS) % nt == 0

    e_t = et_ref[...]
    t_row = _chunk_time(lax.broadcasted_iota(jnp.int32, (q, q), 0))
    t_col = _chunk_time(lax.broadcasted_iota(jnp.int32, (q, q), 1))
    causal = t_row >= t_col
    common = (cw_ref, cbias_ref, scw_ref, dskip_ref, nw_ref, e_t, causal)
    hist = (state_ref, tail_ref, stail_ref, tb)

    def project_after(schedule):
        def after_chunk(c):
            for piece in schedule.get(c, ()):
                _project_piece(u_ref, w_ref, pa_ref, pb_ref, piece, tb)
        return after_chunk

    _mix_group(pb_ref, parts_b_ref, g_b, first_b, *common, ys_b_ref, yc_b_ref, gate_b_ref, *hist,
               project_after(PIECES_WHILE_ODD))
    _mix_group(pa_ref, parts_a_ref, g_a, first_a, *common, ys_a_ref, yc_a_ref, gate_a_ref, *hist,
               project_after(PIECES_WHILE_EVEN))


def _mixer_call(u, w_pairs, layer, parts, cw, cbias, scw, dskip_e, nw, e_t, bsz, seq, tb=1024):
    t_total = u.shape[0]
    nt = seq // tb
    n_blocks = t_total // tb
    n_items = n_blocks * PAIRS
    rows_out = t_total + tb

    item_a = lambda s: jnp.minimum(s, n_items - 1)
    item_b = lambda s: jnp.maximum(s - 1, 0)
    out_a = lambda s: (jnp.where(s < n_items, item_a(s) // PAIRS, n_blocks), item_a(s) % PAIRS)
    out_b = lambda s: (jnp.where(s >= 1, item_b(s) // PAIRS, n_blocks), item_b(s) % PAIRS)

    def full(arr):
        nd = arr.ndim
        return pl.BlockSpec(arr.shape, lambda s: (0,) * nd)

    half = GROUPS // 2
    out_shapes = [
        jax.ShapeDtypeStruct((rows_out, half * GROUP_X), jnp.bfloat16),
        jax.ShapeDtypeStruct((rows_out, half * SHORT_PER_GROUP), jnp.bfloat16),
        jax.ShapeDtypeStruct((rows_out, half * GATE_PER_GROUP), jnp.bfloat16),
    ]
    widths = (GROUP_X, SHORT_PER_GROUP, GATE_PER_GROUP)
    return pl.pallas_call(
        functools.partial(_mixer_kernel, tb=tb, n_items=n_items, nt=nt),
        grid=(n_items + 1,),
        in_specs=[
            pl.BlockSpec((tb, D_MODEL), lambda s: (item_a(s) // PAIRS, 0)),
            pl.BlockSpec((1, D_MODEL, 2 * GROUP_COLS), lambda s: (layer, 0, item_a(s) % PAIRS)),
            pl.BlockSpec((PART_ROWS, tb), lambda s: (2 * (item_a(s) % PAIRS), item_a(s) // PAIRS)),
            pl.BlockSpec((PART_ROWS, tb),
                         lambda s: (2 * (item_b(s) % PAIRS) + 1, item_b(s) // PAIRS)),
            full(cw), full(cbias), full(scw), full(dskip_e), full(nw), full(e_t),
        ],
        out_specs=([pl.BlockSpec((tb, w), out_a) for w in widths]
                   + [pl.BlockSpec((tb, w), out_b) for w in widths]),
        out_shape=out_shapes + out_shapes,
        scratch_shapes=[
            pltpu.VMEM((tb // SUBLANES, SUBLANES, GROUP_COLS), jnp.float32),
            pltpu.VMEM((tb // SUBLANES, SUBLANES, GROUP_COLS), jnp.float32),
            pltpu.VMEM((GROUPS, HEADS_PER_GROUP, HEAD_DIM, STATE), jnp.float32),
            pltpu.VMEM((GROUPS, SSM_CONV_WIDTH - 1, SUBLANES, GROUP_CONV), jnp.float32),
            pltpu.VMEM((GROUPS, CONV_WIDTH - 1, SUBLANES, SHORT_PER_GROUP), jnp.float32),
        ],
        compiler_params=_cparams(1, VMEM_LIMIT_MIXER),
        name="mixer",
    )(u, w_pairs, parts, parts, cw, cbias, scw, dskip_e, nw, e_t)


def _out_kernel(yca_ref, ycb_ref, ysa_ref, ysb_ref, ga_ref, gb_ref, x_ref, mod_ref,
                wc_ref, ws_ref, wo_ref, o_ref):
    p_conv = _dot(jnp.concatenate([yca_ref[...], ycb_ref[...]], axis=1), wc_ref[0])
    y_ssm = jnp.concatenate(
        [ref[:, i * GROUP_X:(i + 1) * GROUP_X] for i in range(PAIRS) for ref in (ysa_ref, ysb_ref)],
        axis=1)
    p_ssm = _dot(y_ssm, ws_ref[0])
    g_conv = 1.0 / (1.0 + jnp.exp(-ga_ref[...].astype(jnp.float32)))
    g_ssm = 1.0 / (1.0 + jnp.exp(-gb_ref[...].astype(jnp.float32)))
    merged = g_conv * p_conv + g_ssm * p_ssm
    mix = _dot(merged.astype(jnp.bfloat16), wo_ref[0])
    gate1 = mod_ref[0, 2:3, :]
    o_ref[...] = x_ref[...] + gate1 * mix


def _out_call(mixed, x2, mod_l, wc, ws, wo, layer, seq, tm=512):
    ys_a, yc_a, g_a, ys_b, yc_b, g_b = mixed
    t_total = x2.shape[0]
    tiles_per_batch = seq // tm
    row = lambda width: pl.BlockSpec((tm, width), lambda i: (i, 0))
    const = lambda shape: pl.BlockSpec((1,) + shape, lambda i: (layer, 0, 0))
    return pl.pallas_call(
        _out_kernel,
        grid=(t_total // tm,),
        in_specs=[
            row(CONV_DIM // 2), row(CONV_DIM // 2), row(D_SSM // 2), row(D_SSM // 2),
            row(D_MODEL), row(D_MODEL), row(D_MODEL),
            pl.BlockSpec((1, MOD_ROWS, D_MODEL), lambda i: (i // tiles_per_batch, 0, 0)),
            const((CONV_DIM, D_MODEL)), const((D_SSM, D_MODEL)), const((D_MODEL, D_MODEL)),
        ],
        out_specs=pl.BlockSpec((tm, D_MODEL), lambda i: (i, 0)),
        out_shape=jax.ShapeDtypeStruct((t_total, D_MODEL), jnp.float32),
        compiler_params=_cparams(1),
        name="mix_out",
    )(yc_a, yc_b, ys_a, ys_b, g_a, g_b, x2, mod_l, wc, ws, wo)


def _mlp_kernel(x_ref, mod_ref, ln_ref, wup_ref, wdn_ref, fn_ref, o_ref, u_ref, acc_ref,
                *, tm, final):
    j = pl.program_id(1)

    @pl.when(j == 0)
    def _():
        shift = mod_ref[0, 3:4, :]
        scale = mod_ref[0, 4:5, :]
        _modnorm_to(x_ref, ln_ref, scale, shift, u_ref, tm)
        acc_ref[...] = jnp.zeros_like(acc_ref)

    h = jnp.maximum(_dot(u_ref[...], wup_ref[0]), 0.0)
    acc_ref[...] += _dot((h * h).astype(jnp.bfloat16), wdn_ref[0])

    @pl.when(j == pl.num_programs(1) - 1)
    def _():
        gate2 = mod_ref[0, 5:6, :]
        xn = x_ref[...] + gate2 * acc_ref[...]
        if final:
            ms = jnp.mean(xn * xn, axis=-1, keepdims=True)
            xn = xn * lax.rsqrt(ms + EPS) * fn_ref[...]
        o_ref[...] = xn


def _mlp_call(x2, mod_l, ln, w_up, w_dn, layer, fnorm, seq, final, tm=1024, tf=1024):
    t_total = x2.shape[0]
    tiles_per_batch = seq // tm
    return pl.pallas_call(
        functools.partial(_mlp_kernel, tm=tm, final=final),
        grid=(t_total // tm, D_FF // tf),
        in_specs=[
            pl.BlockSpec((tm, D_MODEL), lambda i, j: (i, 0)),
            pl.BlockSpec((1, MOD_ROWS, D_MODEL), lambda i, j: (i // tiles_per_batch, 0, 0)),
            pl.BlockSpec((1, D_MODEL), lambda i, j: (0, 0)),
            pl.BlockSpec((1, D_MODEL, tf), lambda i, j: (layer, 0, j)),
            pl.BlockSpec((1, tf, D_MODEL), lambda i, j: (layer, j, 0)),
            pl.BlockSpec((1, D_MODEL), lambda i, j: (0, 0)),
        ],
        out_specs=pl.BlockSpec((tm, D_MODEL), lambda i, j: (i, 0)),
        out_shape=jax.ShapeDtypeStruct((t_total, D_MODEL), jnp.float32),
        scratch_shapes=[pltpu.VMEM((tm, D_MODEL), jnp.bfloat16),
                        pltpu.VMEM((tm, D_MODEL), jnp.float32)],
        compiler_params=_cparams(2),
        name="mlp",
    )(x2, mod_l, ln, w_up, w_dn, fnorm)


def _head_constants():
    e_t = np.zeros((PART_ROWS, HEADS_PER_GROUP, LANES), np.float32)
    perm = np.zeros((GROUPS, PART_ROWS, 2 * N_SPLIT, HEADS), np.float32)
    for k in range(2 * N_SPLIT):
        for r in range(HEADS_PER_GROUP):
            if k < N_SPLIT:
                e_t[k * HEADS_PER_GROUP + r, r, :] = 1.0
            for g in range(GROUPS):
                perm[g, k * HEADS_PER_GROUP + r, k, g * HEADS_PER_GROUP + r] = 1.0
    e_t = e_t.reshape(PART_ROWS, HEADS_PER_GROUP * LANES)
    perm = perm.reshape(GROUPS * PART_ROWS, 2 * N_SPLIT * HEADS)
    tm = _chunk_time(np.arange(SCAN_CHUNK))
    cum = (tm[None, :] <= tm[:, None]).astype(np.float32)
    bf = jnp.bfloat16
    return jnp.asarray(e_t, bf), jnp.asarray(perm, bf), jnp.asarray(cum, bf)


def _regroup_kernel(xs_ref, bs_ref, cs_ref, z_ref, cb_ref, cc_ref, cx_ref, gate_ref, o_ref):
    for ref, lo, hi in ((xs_ref, G_XBC, G_XBC + GROUP_X), (bs_ref, G_XBC + GROUP_X, G_Z - STATE),
                        (cs_ref, G_Z - STATE, G_Z), (z_ref, G_Z, G_CB), (cb_ref, G_CB, G_CC),
                        (cc_ref, G_CC, G_CX), (cx_ref, G_CX, G_GATE), (gate_ref, G_GATE, GROUP_COLS)):
        o_ref[0, :, lo:hi] = ref[0].astype(o_ref.dtype)


def _regroup_w_in(w):
    def cols(width, off, paired):
        base = off // width
        if paired:
            return pl.BlockSpec((1, D_MODEL, width),
                                lambda l, g: (l, 0, base + (g % 2) * PAIRS + g // 2))
        return pl.BlockSpec((1, D_MODEL, width), lambda l, g: (l, 0, base + g))

    return pl.pallas_call(
        _regroup_kernel,
        grid=(DEPTH, GROUPS),
        in_specs=[
            cols(GROUP_X, OFF_XS, False), cols(STATE, OFF_BS, False), cols(STATE, OFF_CS, False),
            cols(GROUP_X, OFF_Z, False),
            cols(SHORT_PER_GROUP, OFF_CB, True), cols(SHORT_PER_GROUP, OFF_CC, True),
            cols(SHORT_PER_GROUP, OFF_CX, True), cols(GATE_PER_GROUP, 0, True),
        ],
        out_specs=pl.BlockSpec((1, D_MODEL, GROUP_COLS), lambda l, g: (l, 0, g)),
        out_shape=jax.ShapeDtypeStruct((DEPTH, D_MODEL, GROUPS * GROUP_COLS), jnp.bfloat16),
        compiler_params=_cparams(2),
        name="w_in_regroup",
    )(w, w, w, w, w, w, w, w)


def _group_conv_params(p):
    k = p.shape[1]
    xs = p[..., :D_SSM].reshape(DEPTH, k, GROUPS, GROUP_X)
    bs = p[..., D_SSM:D_SSM + GROUPS * STATE].reshape(DEPTH, k, GROUPS, STATE)
    cs = p[..., D_SSM + GROUPS * STATE:].reshape(DEPTH, k, GROUPS, STATE)
    return jnp.swapaxes(jnp.concatenate([xs, bs, cs], axis=-1), 1, 2)


def _permute_tokens(x, bsz, seq, inverse=False):
    a, b = (SLABS, SUBLANES) if inverse else (SUBLANES, SLABS)
    x5 = x.reshape(bsz, seq // SCAN_CHUNK, a, b, D_MODEL)
    return jnp.swapaxes(x5, 2, 3).reshape(bsz * seq, D_MODEL)


def kernel(x, c, w_ada, b_ada, ln1, ln2, w_in, conv_w, ssm_conv_w, ssm_conv_b, dt_bias, a_log,
           d_skip, ssm_norm_w, w_conv_out, w_ssm_out, w_o, w_up, w_down, final_norm):
    bsz, seq, _ = x.shape
    bf = jnp.bfloat16
    f32 = jnp.float32

    c_pad = jnp.zeros((SUBLANES, D_MODEL), f32).at[:bsz].set(c)
    mod_all = _ada_call(c_pad, w_ada, b_ada)
    mod_all = mod_all[:, :bsz].reshape(DEPTH, bsz, N_MOD, D_MODEL)
    mod_all = jnp.pad(mod_all, ((0, 0), (0, 0), (0, MOD_ROWS - N_MOD), (0, 0)))

    w_pairs = _regroup_w_in(w_in)
    w_dt = jnp.pad(w_in[..., OFF_DT:], ((0, 0), (0, 0), (0, LANES - HEADS))).astype(bf)
    hp = jnp.pad(jnp.stack([dt_bias, a_log], axis=1),
                 ((0, 0), (0, SUBLANES - 2), (0, LANES - HEADS)))
    cw = _group_conv_params(ssm_conv_w)
    cbias = _group_conv_params(ssm_conv_b[:, None, :])
    scw = conv_w.reshape(DEPTH, CONV_WIDTH, 2, PAIRS, SHORT_PER_GROUP)
    scw = jnp.transpose(scw, (0, 3, 2, 1, 4)).reshape(DEPTH, GROUPS, CONV_WIDTH, SHORT_PER_GROUP)
    dskip_e = jnp.repeat(d_skip, HEAD_DIM, axis=-1).reshape(DEPTH, GROUPS, 1, GROUP_X)
    nw = ssm_norm_w.reshape(DEPTH, GROUPS, 1, GROUP_X)
    ws = w_ssm_out.astype(bf)
    wc = w_conv_out.astype(bf)
    wo = w_o.astype(bf)
    wup = w_up.astype(bf)
    wdn = w_down.astype(bf)

    e_t, perm, cum = _head_constants()
    fnorm = final_norm.reshape(1, D_MODEL)
    x2 = _permute_tokens(x, bsz, seq)

    for l in range(DEPTH):
        mod_l = mod_all[l]
        u, parts = _pre_call(x2, mod_l, ln1[l].reshape(1, D_MODEL), w_dt[l], hp[l], cum, perm, seq)
        mixed = _mixer_call(u, w_pairs, l, parts, cw[l], cbias[l], scw[l], dskip_e[l], nw[l], e_t,
                            bsz, seq)
        x2 = _out_call(mixed, x2, mod_l, wc, ws, wo, l, seq)
        x2 = _mlp_call(x2, mod_l, ln2[l].reshape(1, D_MODEL), wup, wdn, l, fnorm, seq,
                       final=(l == DEPTH - 1))
    return _permute_tokens(x2, bsz, seq, inverse=True).reshape(bsz, seq, D_MODEL)
```

```python
import functools

import numpy as np
import jax
import jax.numpy as jnp
from jax import lax
from jax.experimental import pallas as pl
from jax.experimental.pallas import tpu as pltpu

D_MODEL = 1024
DEPTH = 4
EPS = 1e-6
N_MOD = 6
MOD_ROWS = 8
CONV_DIM = D_MODEL
CONV_WIDTH = 3
D_SSM = 2 * D_MODEL
HEAD_DIM = 64
HEADS = D_SSM // HEAD_DIM
GROUPS = 8
PAIRS = GROUPS // 2
HEADS_PER_GROUP = HEADS // GROUPS
GROUP_X = HEADS_PER_GROUP * HEAD_DIM
STATE = 128
SSM_CONV_WIDTH = 4
D_FF = 4 * D_MODEL
LANES = 128
SUBLANES = 8
F32_TINY = float(np.finfo(np.float32).tiny)
LOG2E = float(np.log2(np.e))

OFF_CB = 2 * D_MODEL
OFF_CC = OFF_CB + CONV_DIM
OFF_CX = OFF_CC + CONV_DIM
OFF_Z = OFF_CX + CONV_DIM
OFF_XS = OFF_Z + D_SSM
OFF_BS = OFF_XS + D_SSM
OFF_CS = OFF_BS + GROUPS * STATE
OFF_DT = OFF_CS + GROUPS * STATE

SCAN_CHUNK = 128
SLABS = SCAN_CHUNK // SUBLANES
GROUP_CONV = GROUP_X + 2 * STATE
SHORT_PER_GROUP = CONV_DIM // GROUPS
GATE_PER_GROUP = 2 * D_MODEL // GROUPS
G_XBC = 0
G_Z = G_XBC + GROUP_CONV
G_CB = G_Z + GROUP_X
G_CC = G_CB + SHORT_PER_GROUP
G_CX = G_CC + SHORT_PER_GROUP
G_GATE = G_CX + SHORT_PER_GROUP
GROUP_COLS = G_GATE + GATE_PER_GROUP
PROJ_PIECE = 256
PIECES_WHILE_ODD = {0: (0,), 1: (1,), 2: (2,), 4: (3,), 5: (4,), 7: (5,)}
PIECES_WHILE_EVEN = {0: (6,), 1: (7,), 3: (8,), 4: (9,), 6: (10,)}

N_SPLIT = 3
PART_ROWS = 32
BETA_ROW0 = N_SPLIT * HEADS_PER_GROUP

VMEM_LIMIT = 48 * 1024 * 1024
VMEM_LIMIT_MIXER = 56 * 1024 * 1024


def _cparams(n_axes, limit=VMEM_LIMIT):
    return pltpu.CompilerParams(
        dimension_semantics=("arbitrary",) * n_axes,
        vmem_limit_bytes=limit)


def _dot(a, b):
    return jnp.dot(a, b, preferred_element_type=jnp.float32)


def _split3(v):
    hi = v.astype(jnp.bfloat16)
    r1 = v - hi.astype(jnp.float32)
    mid = r1.astype(jnp.bfloat16)
    lo = (r1 - mid.astype(jnp.float32)).astype(jnp.bfloat16)
    return hi, mid, lo


def _silu(v):
    return v / (1.0 + jnp.exp2(v * (-LOG2E)))


def _chunk_time(p):
    return (p % SUBLANES) * SLABS + p // SUBLANES


def _ada_kernel(c_ref, w_ref, b_ref, o_ref):
    c = c_ref[...]
    o_ref[0] = _dot(_silu(c), w_ref[0]) + b_ref[0]


def _ada_call(c_pad, w_ada, b_ada):
    n_out = N_MOD * D_MODEL
    tn = D_MODEL
    return pl.pallas_call(
        _ada_kernel,
        grid=(DEPTH, n_out // tn),
        in_specs=[
            pl.BlockSpec((SUBLANES, D_MODEL), lambda l, j: (0, 0)),
            pl.BlockSpec((1, D_MODEL, tn), lambda l, j: (l, 0, j)),
            pl.BlockSpec((1, 1, tn), lambda l, j: (l, 0, j)),
        ],
        out_specs=pl.BlockSpec((1, SUBLANES, tn), lambda l, j: (l, 0, j)),
        out_shape=jax.ShapeDtypeStruct((DEPTH, SUBLANES, n_out), jnp.float32),
        compiler_params=_cparams(2),
        name="ada_mod",
    )(c_pad, w_ada, b_ada.reshape(DEPTH, 1, n_out))


def _modnorm_to(x_ref, ln_ref, scale, shift, u_ref, rows, chunk=256):
    w = ln_ref[...] * (1.0 + scale)

    def body(r, carry):
        r0 = pl.multiple_of(r * chunk, chunk)
        xf = x_ref[pl.ds(r0, chunk), :]
        ms = jnp.mean(xf * xf, axis=-1, keepdims=True)
        y = xf * lax.rsqrt(ms + EPS)
        u_ref[pl.ds(r0, chunk), :] = (y * w + shift).astype(u_ref.dtype)
        return carry

    lax.fori_loop(0, rows // chunk, body, 0)


def _permute_rows(src_ref, dst_ref, onehot_b, rows):
    def body(c, carry):
        r0 = pl.multiple_of(c * SCAN_CHUNK, SCAN_CHUNK)
        hi, mid, lo = _split3(src_ref[pl.ds(r0, SCAN_CHUNK), :])
        dst_ref[pl.ds(r0, SCAN_CHUNK), :] = (_dot(onehot_b, hi) + _dot(onehot_b, mid)
                                             + _dot(onehot_b, lo)).astype(dst_ref.dtype)
        return carry

    lax.fori_loop(0, rows // SCAN_CHUNK, body, 0)


def _pre_kernel(*refs, tp, permute_in):
    if permute_in:
        (xnat_ref, mod_ref, ln_ref, wdt_ref, hp_ref, cum_ref, perm_ref, toperm_ref,
         x_ref, u_ref, parts_ref) = refs
        _permute_rows(xnat_ref, x_ref, toperm_ref[...], tp)
    else:
        x_ref, mod_ref, ln_ref, wdt_ref, hp_ref, cum_ref, perm_ref, u_ref, parts_ref = refs
    shift = mod_ref[0, 0:1, :]
    scale = mod_ref[0, 1:2, :]
    _modnorm_to(x_ref, ln_ref, scale, shift, u_ref, tp)
    dt_raw = _dot(u_ref[...], wdt_ref[...])
    dt_bias = hp_ref[0:1, :]
    a_neg = -jnp.exp(hp_ref[1:2, :])
    cum_b = cum_ref[...]
    perm = perm_ref[...]
    q = SCAN_CHUNK
    for c in range(tp // q):
        r0 = c * q
        dtr = dt_raw[r0:r0 + q, :] + dt_bias
        dt = jnp.maximum(dtr, 0.0) + jnp.log1p(jnp.exp(-jnp.abs(dtr)))
        a_parts = _split3(dt * a_neg)
        acum = _dot(cum_b, a_parts[0]) + _dot(cum_b, a_parts[1]) + _dot(cum_b, a_parts[2])
        beta = acum - jnp.log(jnp.maximum(dt, F32_TINY))
        acum_t = (acum * LOG2E).T[0:HEADS, :]
        beta_t = (beta * LOG2E).T[0:HEADS, :]
        terms = jnp.concatenate(_split3(acum_t) + _split3(beta_t), axis=0)
        parts_ref[:, r0:r0 + q] = _dot(perm, terms).astype(parts_ref.dtype)


def _pre_call(x2, mod_l, ln, w_dt, hp, cum, perm, seq, to_perm=None, tp=1024):
    t_total = x2.shape[0]
    tiles_per_batch = seq // tp
    permute_in = to_perm is not None
    square = pl.BlockSpec((SCAN_CHUNK, SCAN_CHUNK), lambda i: (0, 0))
    rows = pl.BlockSpec((tp, D_MODEL), lambda i: (i, 0))
    return pl.pallas_call(
        functools.partial(_pre_kernel, tp=tp, permute_in=permute_in),
        grid=(t_total // tp,),
        in_specs=[
            rows,
            pl.BlockSpec((1, MOD_ROWS, D_MODEL), lambda i: (i // tiles_per_batch, 0, 0)),
            pl.BlockSpec((1, D_MODEL), lambda i: (0, 0)),
            pl.BlockSpec((D_MODEL, LANES), lambda i: (0, 0)),
            pl.BlockSpec((SUBLANES, LANES), lambda i: (0, 0)),
            square,
            pl.BlockSpec((GROUPS * PART_ROWS, 2 * N_SPLIT * HEADS), lambda i: (0, 0)),
        ] + ([square] if permute_in else []),
        out_specs=([rows] if permute_in else []) + [
            rows,
            pl.BlockSpec((GROUPS * PART_ROWS, tp), lambda i: (0, i)),
        ],
        out_shape=([jax.ShapeDtypeStruct((t_total, D_MODEL), jnp.float32)] if permute_in else []) + [
            jax.ShapeDtypeStruct((t_total, D_MODEL), jnp.bfloat16),
            jax.ShapeDtypeStruct((GROUPS * PART_ROWS, t_total), jnp.bfloat16),
        ],
        compiler_params=_cparams(1),
        name="pre",
    )(*((x2, mod_l, ln, w_dt, hp, cum, perm) + ((to_perm,) if permute_in else ())))


def _shifted_taps(cur, tail, width):
    n_wrap = width - 1
    cur_tail = cur[SLABS - n_wrap:]
    wrapped = jnp.concatenate([tail[:, SUBLANES - 1:, :], cur_tail[:, :SUBLANES - 1, :]], axis=1)
    taps = [cur]
    for s in range(1, width):
        taps.append(jnp.concatenate([wrapped[n_wrap - s:], cur[:SLABS - s]], axis=0))
    return taps, cur_tail


def _project_piece(u_ref, w_ref, pa_ref, pb_ref, piece, tb):
    lo = piece * PROJ_PIECE
    res = _dot(u_ref[...], w_ref[0, :, lo:lo + PROJ_PIECE]
               ).reshape(tb // SUBLANES, SUBLANES, PROJ_PIECE)
    n_a = min(max(GROUP_COLS - lo, 0), PROJ_PIECE)
    if n_a:
        pa_ref[:, :, lo:lo + n_a] = res[:, :, 0:n_a]
    if n_a < PROJ_PIECE:
        b0 = lo + n_a - GROUP_COLS
        pb_ref[:, :, b0:b0 + PROJ_PIECE - n_a] = res[:, :, n_a:]


def _mix_group(p_ref, parts_ref, g, first, cw_ref, cbias_ref, scw_ref, dskip_ref, nw_ref, e_t,
               causal, ys_ref, yc_ref, gate_ref, state_ref, tail_ref, stail_ref, tb, after_chunk,
               hist_src, keep_start):
    q = SCAN_CHUNK
    nt_dims = (((1,), (1,)), ((), ()))
    tn_dims = (((0,), (0,)), ((), ()))
    cw = cw_ref[g]
    cbias = cbias_ref[g]
    scw = scw_ref[g]
    dskip = dskip_ref[g]
    nw = nw_ref[g]

    gate_ref[...] = p_ref[:, :, G_GATE:GROUP_COLS].reshape(tb, GATE_PER_GROUP).astype(gate_ref.dtype)

    state_in = jnp.where(first, 0.0, state_ref[hist_src])
    tail = jnp.where(first, 0.0, tail_ref[hist_src])
    stail = jnp.where(first, 0.0, stail_ref[hist_src])
    if keep_start:
        state_ref[GROUPS] = state_in
        tail_ref[GROUPS] = tail
        stail_ref[GROUPS] = stail
    state_ref[g] = state_in

    for c in range(tb // q):
        r0 = c * q
        b0 = c * SLABS
        sv = p_ref[b0:b0 + SLABS, :, G_CC:G_CX] * p_ref[b0:b0 + SLABS, :, G_CX:G_GATE]
        staps, stail = _shifted_taps(sv, stail, CONV_WIDTH)
        sconv = scw[CONV_WIDTH - 1:CONV_WIDTH, :] * staps[0]
        for s in range(1, CONV_WIDTH):
            sconv += scw[CONV_WIDTH - 1 - s:CONV_WIDTH - s, :] * staps[s]
        yc = p_ref[b0:b0 + SLABS, :, G_CB:G_CC] * sconv
        yc_ref[r0:r0 + q, :] = yc.reshape(q, SHORT_PER_GROUP).astype(yc_ref.dtype)

        taps, tail = _shifted_taps(p_ref[b0:b0 + SLABS, :, G_XBC:G_Z], tail, SSM_CONV_WIDTH)
        acc = cbias + cw[SSM_CONV_WIDTH - 1:SSM_CONV_WIDTH, :] * taps[0]
        for s in range(1, SSM_CONV_WIDTH):
            acc += cw[SSM_CONV_WIDTH - 1 - s:SSM_CONV_WIDTH - s, :] * taps[s]
        xbc = _silu(acc).reshape(q, GROUP_CONV)
        xs = xbc[:, 0:GROUP_X]
        xt_b = xs.astype(jnp.bfloat16).T
        b_b = xbc[:, GROUP_X:GROUP_X + STATE].astype(jnp.bfloat16)
        c_b = xbc[:, GROUP_X + STATE:GROUP_CONV].astype(jnp.bfloat16)

        mc = parts_ref[:, r0:r0 + q]
        acum_l = lax.dot_general(mc, e_t, tn_dims, preferred_element_type=jnp.float32)
        acum_last = acum_l[q - 1:q, :]
        mf = mc.astype(jnp.float32)

        scores_b = lax.dot_general(c_b, b_b, nt_dims, preferred_element_type=jnp.float32
                                   ).astype(jnp.bfloat16)
        ys = []
        for r in range(HEADS_PER_GROUP):
            al = acum_l[:, r * LANES:(r + 1) * LANES]
            last = acum_last[:, r * LANES:(r + 1) * LANES]
            beta_row = (mf[BETA_ROW0 + r:BETA_ROW0 + r + 1, :]
                        + mf[BETA_ROW0 + HEADS_PER_GROUP + r:BETA_ROW0 + HEADS_PER_GROUP + r + 1, :]
                        + mf[BETA_ROW0 + 2 * HEADS_PER_GROUP + r:
                             BETA_ROW0 + 2 * HEADS_PER_GROUP + r + 1, :])
            seg = (al - beta_row).astype(jnp.bfloat16)
            w_r = scores_b * jnp.exp2(jnp.where(causal, seg, -jnp.inf))
            c_r = c_b * jnp.exp2(al).astype(jnp.bfloat16)
            xt_r = xt_b[r * HEAD_DIM:(r + 1) * HEAD_DIM, :]
            xw_r = xt_r * jnp.exp2(last - beta_row).astype(jnp.bfloat16)
            s_r = state_ref[g, r]
            ys.append(lax.dot_general(
                jnp.concatenate([w_r, c_r], axis=1),
                jnp.concatenate([xt_r, s_r.astype(jnp.bfloat16)], axis=1),
                nt_dims, preferred_element_type=jnp.float32))
            state_ref[g, r] = s_r * jnp.exp2(last) + _dot(xw_r, b_b)
        y = jnp.concatenate(ys, axis=-1)

        y = y + dskip * xs
        y = y * _silu(p_ref[b0:b0 + SLABS, :, G_Z:G_CB].reshape(q, GROUP_X))
        y = y * lax.rsqrt(jnp.mean(y * y, axis=-1, keepdims=True) + EPS)
        ys_ref[r0:r0 + q, :] = (y * nw).astype(ys_ref.dtype)
        after_chunk(c)

    tail_ref[g] = tail
    stail_ref[g] = stail


def _mixer_kernel(u_ref, w_ref, parts_a_ref, parts_b_ref,
                  cw_ref, cbias_ref, scw_ref, dskip_ref, nw_ref, et_ref,
                  ys_a_ref, yc_a_ref, gate_a_ref, ys_b_ref, yc_b_ref, gate_b_ref,
                  pa_ref, pb_ref, state_ref, tail_ref, stail_ref, *, tb, n_items, nt):
    s = pl.program_id(0)
    q = SCAN_CHUNK

    @pl.when(s == 0)
    def _():
        pb_ref[...] = jnp.zeros_like(pb_ref)
        state_ref[...] = jnp.zeros_like(state_ref)
        tail_ref[...] = jnp.zeros_like(tail_ref)
        stail_ref[...] = jnp.zeros_like(stail_ref)

    item_a = jnp.minimum(s, n_items - 1)
    item_b = jnp.maximum(s - 1, 0)
    g_a = 2 * (item_a % PAIRS)
    g_b = 2 * (item_b % PAIRS) + 1
    first_a = (item_a // PAIRS) % nt == 0
    first_b = (item_b // PAIRS) % nt == 0

    e_t = et_ref[...]
    t_row = _chunk_time(lax.broadcasted_iota(jnp.int32, (q, q), 0))
    t_col = _chunk_time(lax.broadcasted_iota(jnp.int32, (q, q), 1))
    causal = t_row >= t_col
    common = (cw_ref, cbias_ref, scw_ref, dskip_ref, nw_ref, e_t, causal)
    hist = (state_ref, tail_ref, stail_ref, tb)

    def project_after(schedule):
        def after_chunk(c):
            for piece in schedule.get(c, ()):
                _project_piece(u_ref, w_ref, pa_ref, pb_ref, piece, tb)
        return after_chunk

    _mix_group(pb_ref, parts_b_ref, g_b, first_b, *common, ys_b_ref, yc_b_ref, gate_b_ref, *hist,
               project_after(PIECES_WHILE_ODD), g_b, False)
    drain = s == n_items
    _mix_group(pa_ref, parts_a_ref, g_a, first_a, *common, ys_a_ref, yc_a_ref, gate_a_ref, *hist,
               project_after(PIECES_WHILE_EVEN), jnp.where(drain, GROUPS, g_a), True)


def _mixer_call(u, w_pairs, layer, parts, cw, cbias, scw, dskip_e, nw, e_t, bsz, seq, tb=1024):
    t_total = u.shape[0]
    nt = seq // tb
    n_blocks = t_total // tb
    n_items = n_blocks * PAIRS
    item_a = lambda s: jnp.minimum(s, n_items - 1)
    item_b = lambda s: jnp.maximum(s - 1, 0)
    out_a = lambda s: (item_a(s) // PAIRS, item_a(s) % PAIRS)
    out_b = lambda s: (item_b(s) // PAIRS, item_b(s) % PAIRS)

    def full(arr):
        nd = arr.ndim
        return pl.BlockSpec(arr.shape, lambda s: (0,) * nd)

    half = GROUPS // 2
    out_shapes = [
        jax.ShapeDtypeStruct((t_total, half *GROUP_X), jnp.bfloat16),
        jax.ShapeDtypeStruct((t_total, half *SHORT_PER_GROUP), jnp.bfloat16),
        jax.ShapeDtypeStruct((t_total, half *GATE_PER_GROUP), jnp.bfloat16),
    ]
    widths = (GROUP_X, SHORT_PER_GROUP, GATE_PER_GROUP)
    return pl.pallas_call(
        functools.partial(_mixer_kernel, tb=tb, n_items=n_items, nt=nt),
        grid=(n_items + 1,),
        in_specs=[
            pl.BlockSpec((tb, D_MODEL), lambda s: (item_a(s) // PAIRS, 0)),
            pl.BlockSpec((1, D_MODEL, 2 * GROUP_COLS), lambda s: (layer, 0, item_a(s) % PAIRS)),
            pl.BlockSpec((PART_ROWS, tb), lambda s: (2 * (item_a(s) % PAIRS), item_a(s) // PAIRS)),
            pl.BlockSpec((PART_ROWS, tb),
                         lambda s: (2 * (item_b(s) % PAIRS) + 1, item_b(s) // PAIRS)),
            full(cw), full(cbias), full(scw), full(dskip_e), full(nw), full(e_t),
        ],
        out_specs=([pl.BlockSpec((tb, w), out_a) for w in widths]
                   + [pl.BlockSpec((tb, w), out_b) for w in widths]),
        out_shape=out_shapes + out_shapes,
        scratch_shapes=[
            pltpu.VMEM((tb // SUBLANES, SUBLANES, GROUP_COLS), jnp.float32),
            pltpu.VMEM((tb // SUBLANES, SUBLANES, GROUP_COLS), jnp.float32),
            pltpu.VMEM((GROUPS + 1, HEADS_PER_GROUP, HEAD_DIM, STATE), jnp.float32),
            pltpu.VMEM((GROUPS + 1, SSM_CONV_WIDTH - 1, SUBLANES, GROUP_CONV), jnp.float32),
            pltpu.VMEM((GROUPS + 1, CONV_WIDTH - 1, SUBLANES, SHORT_PER_GROUP), jnp.float32),
        ],
        compiler_params=_cparams(1, VMEM_LIMIT_MIXER),
        name="mixer",
    )(u, w_pairs, parts, parts, cw, cbias, scw, dskip_e, nw, e_t)


def _out_kernel(yca_ref, ycb_ref, ysa_ref, ysb_ref, ga_ref, gb_ref, x_ref, mod_ref,
                wc_ref, ws_ref, wo_ref, o_ref):
    p_conv = _dot(jnp.concatenate([yca_ref[...], ycb_ref[...]], axis=1), wc_ref[0])
    y_ssm = jnp.concatenate(
        [ref[:, i * GROUP_X:(i + 1) * GROUP_X] for i in range(PAIRS) for ref in (ysa_ref, ysb_ref)],
        axis=1)
    p_ssm = _dot(y_ssm, ws_ref[0])
    g_conv = 1.0 / (1.0 + jnp.exp(-ga_ref[...].astype(jnp.float32)))
    g_ssm = 1.0 / (1.0 + jnp.exp(-gb_ref[...].astype(jnp.float32)))
    merged = g_conv * p_conv + g_ssm * p_ssm
    mix = _dot(merged.astype(jnp.bfloat16), wo_ref[0])
    gate1 = mod_ref[0, 2:3, :]
    o_ref[...] = x_ref[...] + gate1 * mix


def _out_call(mixed, x2, mod_l, wc, ws, wo, layer, seq, tm=512):
    ys_a, yc_a, g_a, ys_b, yc_b, g_b = mixed
    t_total = x2.shape[0]
    tiles_per_batch = seq // tm
    row = lambda width: pl.BlockSpec((tm, width), lambda i: (i, 0))
    const = lambda shape: pl.BlockSpec((1,) + shape, lambda i: (layer, 0, 0))
    return pl.pallas_call(
        _out_kernel,
        grid=(t_total // tm,),
        in_specs=[
            row(CONV_DIM // 2), row(CONV_DIM // 2), row(D_SSM // 2), row(D_SSM // 2),
            row(D_MODEL), row(D_MODEL), row(D_MODEL),
            pl.BlockSpec((1, MOD_ROWS, D_MODEL), lambda i: (i // tiles_per_batch, 0, 0)),
            const((CONV_DIM, D_MODEL)), const((D_SSM, D_MODEL)), const((D_MODEL, D_MODEL)),
        ],
        out_specs=pl.BlockSpec((tm, D_MODEL), lambda i: (i, 0)),
        out_shape=jax.ShapeDtypeStruct((t_total, D_MODEL), jnp.float32),
        compiler_params=_cparams(1),
        name="mix_out",
    )(yc_a, yc_b, ys_a, ys_b, g_a, g_b, x2, mod_l, wc, ws, wo)


def _mlp_kernel(x_ref, mod_ref, ln_ref, wup_ref, wdn_ref, fn_ref, fromperm_ref, o_ref, u_ref,
                acc_ref, *, tm, final):
    j = pl.program_id(1)

    @pl.when(j == 0)
    def _():
        shift = mod_ref[0, 3:4, :]
        scale = mod_ref[0, 4:5, :]
        _modnorm_to(x_ref, ln_ref, scale, shift, u_ref, tm)
        acc_ref[...] = jnp.zeros_like(acc_ref)

    h = jnp.maximum(_dot(u_ref[...], wup_ref[0]), 0.0)
    acc_ref[...] += _dot((h * h).astype(jnp.bfloat16), wdn_ref[0])

    @pl.when(j == pl.num_programs(1) - 1)
    def _():
        gate2 = mod_ref[0, 5:6, :]
        xn = x_ref[...] + gate2 * acc_ref[...]
        if final:
            ms = jnp.mean(xn * xn, axis=-1, keepdims=True)
            acc_ref[...] = xn * lax.rsqrt(ms + EPS) * fn_ref[...]
            _permute_rows(acc_ref, o_ref, fromperm_ref[...], tm)
        else:
            o_ref[...] = xn


def _mlp_call(x2, mod_l, ln, w_up, w_dn, layer, fnorm, from_perm, seq, final, tm=1024, tf=1024):
    t_total = x2.shape[0]
    tiles_per_batch = seq // tm
    return pl.pallas_call(
        functools.partial(_mlp_kernel, tm=tm, final=final),
        grid=(t_total // tm, D_FF // tf),
        in_specs=[
            pl.BlockSpec((tm, D_MODEL), lambda i, j: (i, 0)),
            pl.BlockSpec((1, MOD_ROWS, D_MODEL), lambda i, j: (i // tiles_per_batch, 0, 0)),
            pl.BlockSpec((1, D_MODEL), lambda i, j: (0, 0)),
            pl.BlockSpec((1, D_MODEL, tf), lambda i, j: (layer, 0, j)),
            pl.BlockSpec((1, tf, D_MODEL), lambda i, j: (layer, j, 0)),
            pl.BlockSpec((1, D_MODEL), lambda i, j: (0, 0)),
            pl.BlockSpec((SCAN_CHUNK, SCAN_CHUNK), lambda i, j: (0, 0)),
        ],
        out_specs=pl.BlockSpec((tm, D_MODEL), lambda i, j: (i, 0)),
        out_shape=jax.ShapeDtypeStruct((t_total, D_MODEL), jnp.float32),
        scratch_shapes=[pltpu.VMEM((tm, D_MODEL), jnp.bfloat16),
                        pltpu.VMEM((tm, D_MODEL), jnp.float32)],
        compiler_params=_cparams(2),
        name="mlp",
    )(x2, mod_l, ln, w_up, w_dn, fnorm, from_perm)


def _head_constants():
    e_t = np.zeros((PART_ROWS, HEADS_PER_GROUP, LANES), np.float32)
    perm = np.zeros((GROUPS, PART_ROWS, 2 * N_SPLIT, HEADS), np.float32)
    for k in range(2 * N_SPLIT):
        for r in range(HEADS_PER_GROUP):
            if k < N_SPLIT:
                e_t[k * HEADS_PER_GROUP + r, r, :] = 1.0
            for g in range(GROUPS):
                perm[g, k * HEADS_PER_GROUP + r, k, g * HEADS_PER_GROUP + r] = 1.0
    e_t = e_t.reshape(PART_ROWS, HEADS_PER_GROUP * LANES)
    perm = perm.reshape(GROUPS * PART_ROWS, 2 * N_SPLIT * HEADS)
    tm = _chunk_time(np.arange(SCAN_CHUNK))
    cum = (tm[None, :] <= tm[:, None]).astype(np.float32)
    to_perm = (np.arange(SCAN_CHUNK)[None, :] == tm[:, None]).astype(np.float32)
    bf = jnp.bfloat16
    return (jnp.asarray(e_t, bf), jnp.asarray(perm, bf), jnp.asarray(cum, bf),
            jnp.asarray(to_perm, bf), jnp.asarray(to_perm.T, bf))


def _regroup_kernel(xs_ref, bs_ref, cs_ref, z_ref, cb_ref, cc_ref, cx_ref, gate_ref, o_ref):
    for ref, lo, hi in ((xs_ref, G_XBC, G_XBC + GROUP_X), (bs_ref, G_XBC + GROUP_X, G_Z - STATE),
                        (cs_ref, G_Z - STATE, G_Z), (z_ref, G_Z, G_CB), (cb_ref, G_CB, G_CC),
                        (cc_ref, G_CC, G_CX), (cx_ref, G_CX, G_GATE), (gate_ref, G_GATE, GROUP_COLS)):
        o_ref[0, :, lo:hi] = ref[0].astype(o_ref.dtype)


def _regroup_w_in(w):
    def cols(width, off, paired):
        base = off // width
        if paired:
            return pl.BlockSpec((1, D_MODEL, width),
                                lambda l, g: (l, 0, base + (g % 2) * PAIRS + g // 2))
        return pl.BlockSpec((1, D_MODEL, width), lambda l, g: (l, 0, base + g))

    return pl.pallas_call(
        _regroup_kernel,
        grid=(DEPTH, GROUPS),
        in_specs=[
            cols(GROUP_X, OFF_XS, False), cols(STATE, OFF_BS, False), cols(STATE, OFF_CS, False),
            cols(GROUP_X, OFF_Z, False),
            cols(SHORT_PER_GROUP, OFF_CB, True), cols(SHORT_PER_GROUP, OFF_CC, True),
            cols(SHORT_PER_GROUP, OFF_CX, True), cols(GATE_PER_GROUP, 0, True),
        ],
        out_specs=pl.BlockSpec((1, D_MODEL, GROUP_COLS), lambda l, g: (l, 0, g)),
        out_shape=jax.ShapeDtypeStruct((DEPTH, D_MODEL, GROUPS * GROUP_COLS), jnp.bfloat16),
        compiler_params=_cparams(2),
        name="w_in_regroup",
    )(w, w, w, w, w, w, w, w)


def _group_conv_params(p):
    k = p.shape[1]
    xs = p[..., :D_SSM].reshape(DEPTH, k, GROUPS, GROUP_X)
    bs = p[..., D_SSM:D_SSM + GROUPS * STATE].reshape(DEPTH, k, GROUPS, STATE)
    cs = p[..., D_SSM + GROUPS * STATE:].reshape(DEPTH, k, GROUPS, STATE)
    return jnp.swapaxes(jnp.concatenate([xs, bs, cs], axis=-1), 1, 2)


def kernel(x, c, w_ada, b_ada, ln1, ln2, w_in, conv_w, ssm_conv_w, ssm_conv_b, dt_bias, a_log,
           d_skip, ssm_norm_w, w_conv_out, w_ssm_out, w_o, w_up, w_down, final_norm):
    bsz, seq, _ = x.shape
    bf = jnp.bfloat16
    f32 = jnp.float32

    c_pad = jnp.zeros((SUBLANES, D_MODEL), f32).at[:bsz].set(c)
    mod_all = _ada_call(c_pad, w_ada, b_ada)
    mod_all = mod_all[:, :bsz].reshape(DEPTH, bsz, N_MOD, D_MODEL)
    mod_all = jnp.pad(mod_all, ((0, 0), (0, 0), (0, MOD_ROWS - N_MOD), (0, 0)))

    w_pairs = _regroup_w_in(w_in)
    w_dt = jnp.pad(w_in[..., OFF_DT:], ((0, 0), (0, 0), (0, LANES - HEADS))).astype(bf)
    hp = jnp.pad(jnp.stack([dt_bias, a_log], axis=1),
                 ((0, 0), (0, SUBLANES - 2), (0, LANES - HEADS)))
    cw = _group_conv_params(ssm_conv_w)
    cbias = _group_conv_params(ssm_conv_b[:, None, :])
    scw = conv_w.reshape(DEPTH, CONV_WIDTH, 2, PAIRS, SHORT_PER_GROUP)
    scw = jnp.transpose(scw, (0, 3, 2, 1, 4)).reshape(DEPTH, GROUPS, CONV_WIDTH, SHORT_PER_GROUP)
    dskip_e = jnp.repeat(d_skip, HEAD_DIM, axis=-1).reshape(DEPTH, GROUPS, 1, GROUP_X)
    nw = ssm_norm_w.reshape(DEPTH, GROUPS, 1, GROUP_X)
    ws = w_ssm_out.astype(bf)
    wc = w_conv_out.astype(bf)
    wo = w_o.astype(bf)
    wup = w_up.astype(bf)
    wdn = w_down.astype(bf)

    e_t, perm, cum, to_perm, from_perm = _head_constants()
    fnorm = final_norm.reshape(1, D_MODEL)
    x2 = x.reshape(bsz * seq, D_MODEL)

    for l in range(DEPTH):
        mod_l = mod_all[l]
        pre_args = (mod_l, ln1[l].reshape(1, D_MODEL), w_dt[l], hp[l], cum, perm, seq)
        if l == 0:
            x2, u, parts = _pre_call(x2, *pre_args, to_perm=to_perm)
        else:
            u, parts = _pre_call(x2, *pre_args)
        mixed = _mixer_call(u, w_pairs, l, parts, cw[l], cbias[l], scw[l], dskip_e[l], nw[l], e_t,
                            bsz, seq)
        x2 = _out_call(mixed, x2, mod_l, wc, ws, wo, l, seq)
        x2 = _mlp_call(x2, mod_l, ln2[l].reshape(1, D_MODEL), wup, wdn, l, fnorm, from_perm, seq,
                       final=(l == DEPTH - 1))
    return x2.reshape(bsz, seq, D_MODEL)
```

```python
import functools

import numpy as np
import jax
import jax.numpy as jnp
from jax import lax
from jax.experimental import pallas as pl
from jax.experimental.pallas import tpu as pltpu

D_MODEL = 1024
DEPTH = 4
EPS = 1e-6
N_MOD = 6
MOD_ROWS = 8
CONV_DIM = D_MODEL
CONV_WIDTH = 3
D_SSM = 2 * D_MODEL
HEAD_DIM = 64
HEADS = D_SSM // HEAD_DIM
GROUPS = 8
PAIRS = GROUPS // 2
HEADS_PER_GROUP = HEADS // GROUPS
GROUP_X = HEADS_PER_GROUP * HEAD_DIM
STATE = 128
SSM_CONV_WIDTH = 4
D_FF = 4 * D_MODEL
LANES = 128
SUBLANES = 8
F32_TINY = float(np.finfo(np.float32).tiny)
LOG2E = float(np.log2(np.e))

OFF_CB = 2 * D_MODEL
OFF_CC = OFF_CB + CONV_DIM
OFF_CX = OFF_CC + CONV_DIM
OFF_Z = OFF_CX + CONV_DIM
OFF_XS = OFF_Z + D_SSM
OFF_BS = OFF_XS + D_SSM
OFF_CS = OFF_BS + GROUPS * STATE
OFF_DT = OFF_CS + GROUPS * STATE

SCAN_CHUNK = 128
SLABS = SCAN_CHUNK // SUBLANES
GROUP_CONV = GROUP_X + 2 * STATE
SHORT_PER_GROUP = CONV_DIM // GROUPS
GATE_PER_GROUP = 2 * D_MODEL // GROUPS
G_XBC = 0
G_Z = G_XBC + GROUP_CONV
G_CB = G_Z + GROUP_X
G_CC = G_CB + SHORT_PER_GROUP
G_CX = G_CC + SHORT_PER_GROUP
G_GATE = G_CX + SHORT_PER_GROUP
GROUP_COLS = G_GATE + GATE_PER_GROUP
PROJ_PIECE = 256
PIECES_WHILE_ODD = {0: (0,), 1: (1,), 2: (2,), 4: (3,), 5: (4,), 7: (5,)}
PIECES_WHILE_EVEN = {0: (6,), 1: (7,), 3: (8,), 4: (9,), 6: (10,)}

N_SPLIT = 3
PART_ROWS = 32
BETA_ROW0 = N_SPLIT * HEADS_PER_GROUP

VMEM_LIMIT = 48 * 1024 * 1024
VMEM_LIMIT_MIXER = 56 * 1024 * 1024


def _cparams(n_axes, limit=VMEM_LIMIT):
    return pltpu.CompilerParams(
        dimension_semantics=("arbitrary",) * n_axes,
        vmem_limit_bytes=limit)


def _dot(a, b):
    return jnp.dot(a, b, preferred_element_type=jnp.float32)


def _dot_t(a, b_t):
    return lax.dot_general(a, b_t, (((1,), (1,)), ((), ())), preferred_element_type=jnp.float32)


def _split3(v):
    hi = v.astype(jnp.bfloat16)
    r1 = v - hi.astype(jnp.float32)
    mid = r1.astype(jnp.bfloat16)
    lo = (r1 - mid.astype(jnp.float32)).astype(jnp.bfloat16)
    return hi, mid, lo


def _silu(v):
    return v / (1.0 + jnp.exp2(v * (-LOG2E)))


def _chunk_time(p):
    return (p % SUBLANES) * SLABS + p // SUBLANES


def _ada_kernel(c_ref, w_ref, b_ref, o_ref):
    c = c_ref[...]
    o_ref[0] = _dot(_silu(c), w_ref[0]) + b_ref[0]


def _ada_call(c_pad, w_ada, b_ada):
    n_out = N_MOD * D_MODEL
    tn = D_MODEL
    return pl.pallas_call(
        _ada_kernel,
        grid=(DEPTH, n_out // tn),
        in_specs=[
            pl.BlockSpec((SUBLANES, D_MODEL), lambda l, j: (0, 0)),
            pl.BlockSpec((1, D_MODEL, tn), lambda l, j: (l, 0, j)),
            pl.BlockSpec((1, 1, tn), lambda l, j: (l, 0, j)),
        ],
        out_specs=pl.BlockSpec((1, SUBLANES, tn), lambda l, j: (l, 0, j)),
        out_shape=jax.ShapeDtypeStruct((DEPTH, SUBLANES, n_out), jnp.float32),
        compiler_params=_cparams(2),
        name="ada_mod",
    )(c_pad, w_ada, b_ada.reshape(DEPTH, 1, n_out))


def _modnorm_to(x_ref, ln_ref, scale, shift, u_ref, rows, chunk=256):
    w = ln_ref[...] * (1.0 + scale)

    def body(r, carry):
        r0 = pl.multiple_of(r * chunk, chunk)
        xf = x_ref[pl.ds(r0, chunk), :]
        ms = jnp.mean(xf * xf, axis=-1, keepdims=True)
        y = xf * lax.rsqrt(ms + EPS)
        u_ref[pl.ds(r0, chunk), :] = (y * w + shift).astype(u_ref.dtype)
        return carry

    lax.fori_loop(0, rows // chunk, body, 0)


def _permute_rows(src_ref, dst_ref, onehot_b, rows):
    def body(c, carry):
        r0 = pl.multiple_of(c * SCAN_CHUNK, SCAN_CHUNK)
        hi, mid, lo = _split3(src_ref[pl.ds(r0, SCAN_CHUNK), :])
        dst_ref[pl.ds(r0, SCAN_CHUNK), :] = (_dot(onehot_b, hi) + _dot(onehot_b, mid)
                                             + _dot(onehot_b, lo)).astype(dst_ref.dtype)
        return carry

    lax.fori_loop(0, rows // SCAN_CHUNK, body, 0)


def _pre_kernel(*refs, tp, permute_in):
    if permute_in:
        (xnat_ref, mod_ref, ln_ref, wdt_ref, hp_ref, cum_ref, perm_ref, toperm_ref,
         x_ref, u_ref, parts_ref) = refs
        _permute_rows(xnat_ref, x_ref, toperm_ref[...], tp)
    else:
        x_ref, mod_ref, ln_ref, wdt_ref, hp_ref, cum_ref, perm_ref, u_ref, parts_ref = refs
    shift = mod_ref[0, 0:1, :]
    scale = mod_ref[0, 1:2, :]
    _modnorm_to(x_ref, ln_ref, scale, shift, u_ref, tp)
    dt_raw = _dot_t(u_ref[...], wdt_ref[...])
    dt_bias = hp_ref[0:1, :]
    a_neg = -jnp.exp(hp_ref[1:2, :])
    cum_b = cum_ref[...]
    perm = perm_ref[...]
    q = SCAN_CHUNK
    for c in range(tp // q):
        r0 = c * q
        dtr = dt_raw[r0:r0 + q, :] + dt_bias
        dt = jnp.maximum(dtr, 0.0) + jnp.log1p(jnp.exp(-jnp.abs(dtr)))
        a_parts = _split3(dt * a_neg)
        acum = _dot(cum_b, a_parts[0]) + _dot(cum_b, a_parts[1]) + _dot(cum_b, a_parts[2])
        beta = acum - jnp.log(jnp.maximum(dt, F32_TINY))
        acum_t = (acum * LOG2E).T[0:HEADS, :]
        beta_t = (beta * LOG2E).T[0:HEADS, :]
        terms = jnp.concatenate(_split3(acum_t) + _split3(beta_t), axis=0)
        parts_ref[:, r0:r0 + q] = _dot(perm, terms).astype(parts_ref.dtype)


def _pre_call(x2, mod_l, ln, w_dt, hp, cum, perm, seq, to_perm=None, tp=1024):
    t_total = x2.shape[0]
    tiles_per_batch = seq // tp
    permute_in = to_perm is not None
    square = pl.BlockSpec((SCAN_CHUNK, SCAN_CHUNK), lambda i: (0, 0))
    rows = pl.BlockSpec((tp, D_MODEL), lambda i: (i, 0))
    return pl.pallas_call(
        functools.partial(_pre_kernel, tp=tp, permute_in=permute_in),
        grid=(t_total // tp,),
        in_specs=[
            rows,
            pl.BlockSpec((1, MOD_ROWS, D_MODEL), lambda i: (i // tiles_per_batch, 0, 0)),
            pl.BlockSpec((1, D_MODEL), lambda i: (0, 0)),
            pl.BlockSpec((LANES, D_MODEL), lambda i: (0, 0)),
            pl.BlockSpec((SUBLANES, LANES), lambda i: (0, 0)),
            square,
            pl.BlockSpec((GROUPS * PART_ROWS, 2 * N_SPLIT * HEADS), lambda i: (0, 0)),
        ] + ([square] if permute_in else []),
        out_specs=([rows] if permute_in else []) + [
            rows,
            pl.BlockSpec((GROUPS * PART_ROWS, tp), lambda i: (0, i)),
        ],
        out_shape=([jax.ShapeDtypeStruct((t_total, D_MODEL), jnp.float32)] if permute_in else []) + [
            jax.ShapeDtypeStruct((t_total, D_MODEL), jnp.bfloat16),
            jax.ShapeDtypeStruct((GROUPS * PART_ROWS, t_total), jnp.bfloat16),
        ],
        compiler_params=_cparams(1),
        name="pre",
    )(*((x2, mod_l, ln, w_dt, hp, cum, perm) + ((to_perm,) if permute_in else ())))


def _shifted_taps(cur, tail, width):
    n_wrap = width - 1
    cur_tail = cur[SLABS - n_wrap:]
    wrapped = jnp.concatenate([tail[:, SUBLANES - 1:, :], cur_tail[:, :SUBLANES - 1, :]], axis=1)
    taps = [cur]
    for s in range(1, width):
        taps.append(jnp.concatenate([wrapped[n_wrap - s:], cur[:SLABS - s]], axis=0))
    return taps, cur_tail


def _project_piece(u_ref, w_ref, pa_ref, pb_ref, piece, tb):
    lo = piece * PROJ_PIECE
    res = _dot_t(u_ref[...], w_ref[0, lo:lo + PROJ_PIECE, :]
                 ).reshape(tb // SUBLANES, SUBLANES, PROJ_PIECE)
    n_a = min(max(GROUP_COLS - lo, 0), PROJ_PIECE)
    if n_a:
        pa_ref[:, :, lo:lo + n_a] = res[:, :, 0:n_a]
    if n_a < PROJ_PIECE:
        b0 = lo + n_a - GROUP_COLS
        pb_ref[:, :, b0:b0 + PROJ_PIECE - n_a] = res[:, :, n_a:]


def _mix_group(p_ref, parts_ref, g, first, cw_ref, cbias_ref, scw_ref, dskip_ref, nw_ref, e_t,
               causal, ys_ref, yc_ref, gate_ref, state_ref, tail_ref, stail_ref, tb, after_chunk,
               hist_src, keep_start):
    q = SCAN_CHUNK
    nt_dims = (((1,), (1,)), ((), ()))
    tn_dims = (((0,), (0,)), ((), ()))
    cw = cw_ref[g]
    cbias = cbias_ref[g]
    scw = scw_ref[g]
    dskip = dskip_ref[g]
    nw = nw_ref[g]

    gate_ref[...] = p_ref[:, :, G_GATE:GROUP_COLS].reshape(tb, GATE_PER_GROUP).astype(gate_ref.dtype)

    state_in = jnp.where(first, 0.0, state_ref[hist_src])
    tail = jnp.where(first, 0.0, tail_ref[hist_src])
    stail = jnp.where(first, 0.0, stail_ref[hist_src])
    if keep_start:
        state_ref[GROUPS] = state_in
        tail_ref[GROUPS] = tail
        stail_ref[GROUPS] = stail
    state_ref[g] = state_in

    for c in range(tb // q):
        r0 = c * q
        b0 = c * SLABS
        sv = p_ref[b0:b0 + SLABS, :, G_CC:G_CX] * p_ref[b0:b0 + SLABS, :, G_CX:G_GATE]
        staps, stail = _shifted_taps(sv, stail, CONV_WIDTH)
        sconv = scw[CONV_WIDTH - 1:CONV_WIDTH, :] * staps[0]
        for s in range(1, CONV_WIDTH):
            sconv += scw[CONV_WIDTH - 1 - s:CONV_WIDTH - s, :] * staps[s]
        yc = p_ref[b0:b0 + SLABS, :, G_CB:G_CC] * sconv
        yc_ref[r0:r0 + q, :] = yc.reshape(q, SHORT_PER_GROUP).astype(yc_ref.dtype)

        taps, tail = _shifted_taps(p_ref[b0:b0 + SLABS, :, G_XBC:G_Z], tail, SSM_CONV_WIDTH)
        acc = cbias + cw[SSM_CONV_WIDTH - 1:SSM_CONV_WIDTH, :] * taps[0]
        for s in range(1, SSM_CONV_WIDTH):
            acc += cw[SSM_CONV_WIDTH - 1 - s:SSM_CONV_WIDTH - s, :] * taps[s]
        xbc = _silu(acc).reshape(q, GROUP_CONV)
        xs = xbc[:, 0:GROUP_X]
        xt_b = xs.astype(jnp.bfloat16).T
        b_b = xbc[:, GROUP_X:GROUP_X + STATE].astype(jnp.bfloat16)
        c_b = xbc[:, GROUP_X + STATE:GROUP_CONV].astype(jnp.bfloat16)

        mc = parts_ref[:, r0:r0 + q]
        acum_l = lax.dot_general(mc, e_t, tn_dims, preferred_element_type=jnp.float32)
        acum_last = acum_l[q - 1:q, :]
        mf = mc.astype(jnp.float32)

        scores_b = lax.dot_general(c_b, b_b, nt_dims, preferred_element_type=jnp.float32
                                   ).astype(jnp.bfloat16)
        ys = []
        for r in range(HEADS_PER_GROUP):
            al = acum_l[:, r * LANES:(r + 1) * LANES]
            last = acum_last[:, r * LANES:(r + 1) * LANES]
            beta_row = (mf[BETA_ROW0 + r:BETA_ROW0 + r + 1, :]
                        + mf[BETA_ROW0 + HEADS_PER_GROUP + r:BETA_ROW0 + HEADS_PER_GROUP + r + 1, :]
                        + mf[BETA_ROW0 + 2 * HEADS_PER_GROUP + r:
                             BETA_ROW0 + 2 * HEADS_PER_GROUP + r + 1, :])
            seg = (al - beta_row).astype(jnp.bfloat16)
            w_r = scores_b * jnp.exp2(jnp.where(causal, seg, -jnp.inf))
            c_r = c_b * jnp.exp2(al).astype(jnp.bfloat16)
            xt_r = xt_b[r * HEAD_DIM:(r + 1) * HEAD_DIM, :]
            xw_r = xt_r * jnp.exp2(last - beta_row).astype(jnp.bfloat16)
            s_r = state_ref[g, r]
            ys.append(lax.dot_general(
                jnp.concatenate([w_r, c_r], axis=1),
                jnp.concatenate([xt_r, s_r.astype(jnp.bfloat16)], axis=1),
                nt_dims, preferred_element_type=jnp.float32))
            state_ref[g, r] = s_r * jnp.exp2(last) + _dot(xw_r, b_b)
        y = jnp.concatenate(ys, axis=-1)

        y = y + dskip * xs
        y = y * _silu(p_ref[b0:b0 + SLABS, :, G_Z:G_CB].reshape(q, GROUP_X))
        y = y * lax.rsqrt(jnp.mean(y * y, axis=-1, keepdims=True) + EPS)
        ys_ref[r0:r0 + q, :] = (y * nw).astype(ys_ref.dtype)
        after_chunk(c)

    tail_ref[g] = tail
    stail_ref[g] = stail


def _mixer_kernel(u_ref, w_ref, parts_a_ref, parts_b_ref,
                  cw_ref, cbias_ref, scw_ref, dskip_ref, nw_ref, et_ref,
                  ys_a_ref, yc_a_ref, gate_a_ref, ys_b_ref, yc_b_ref, gate_b_ref,
                  pa_ref, pb_ref, state_ref, tail_ref, stail_ref, *, tb, n_items, nt):
    s = pl.program_id(0)
    q = SCAN_CHUNK

    @pl.when(s == 0)
    def _():
        pb_ref[...] = jnp.zeros_like(pb_ref)
        state_ref[...] = jnp.zeros_like(state_ref)
        tail_ref[...] = jnp.zeros_like(tail_ref)
        stail_ref[...] = jnp.zeros_like(stail_ref)

    item_a = jnp.minimum(s, n_items - 1)
    item_b = jnp.maximum(s - 1, 0)
    g_a = 2 * (item_a % PAIRS)
    g_b = 2 * (item_b % PAIRS) + 1
    first_a = (item_a // PAIRS) % nt == 0
    first_b = (item_b // PAIRS) % nt == 0

    e_t = et_ref[...]
    t_row = _chunk_time(lax.broadcasted_iota(jnp.int32, (q, q), 0))
    t_col = _chunk_time(lax.broadcasted_iota(jnp.int32, (q, q), 1))
    causal = t_row >= t_col
    common = (cw_ref, cbias_ref, scw_ref, dskip_ref, nw_ref, e_t, causal)
    hist = (state_ref, tail_ref, stail_ref, tb)

    def project_after(schedule):
        def after_chunk(c):
            for piece in schedule.get(c, ()):
                _project_piece(u_ref, w_ref, pa_ref, pb_ref, piece, tb)
        return after_chunk

    _mix_group(pb_ref, parts_b_ref, g_b, first_b, *common, ys_b_ref, yc_b_ref, gate_b_ref, *hist,
               project_after(PIECES_WHILE_ODD), g_b, False)
    drain = s == n_items
    _mix_group(pa_ref, parts_a_ref, g_a, first_a, *common, ys_a_ref, yc_a_ref, gate_a_ref, *hist,
               project_after(PIECES_WHILE_EVEN), jnp.where(drain, GROUPS, g_a), True)


def _mixer_call(u, w_pairs, layer, parts, cw, cbias, scw, dskip_e, nw, e_t, bsz, seq, tb=1024):
    t_total = u.shape[0]
    nt = seq // tb
    n_blocks = t_total // tb
    n_items = n_blocks * PAIRS
    item_a = lambda s: jnp.minimum(s, n_items - 1)
    item_b = lambda s: jnp.maximum(s - 1, 0)
    out_a = lambda s: (item_a(s) // PAIRS, item_a(s) % PAIRS)
    out_b = lambda s: (item_b(s) // PAIRS, item_b(s) % PAIRS)

    def full(arr):
        nd = arr.ndim
        return pl.BlockSpec(arr.shape, lambda s: (0,) * nd)

    half = GROUPS // 2
    out_shapes = [
        jax.ShapeDtypeStruct((t_total, half *GROUP_X), jnp.bfloat16),
        jax.ShapeDtypeStruct((t_total, half *SHORT_PER_GROUP), jnp.bfloat16),
        jax.ShapeDtypeStruct((t_total, half *GATE_PER_GROUP), jnp.bfloat16),
    ]
    widths = (GROUP_X, SHORT_PER_GROUP, GATE_PER_GROUP)
    return pl.pallas_call(
        functools.partial(_mixer_kernel, tb=tb, n_items=n_items, nt=nt),
        grid=(n_items + 1,),
        in_specs=[
            pl.BlockSpec((tb, D_MODEL), lambda s: (item_a(s) // PAIRS, 0)),
            pl.BlockSpec((1, 2 * GROUP_COLS, D_MODEL), lambda s: (layer, item_a(s) % PAIRS, 0)),
            pl.BlockSpec((PART_ROWS, tb), lambda s: (2 * (item_a(s) % PAIRS), item_a(s) // PAIRS)),
            pl.BlockSpec((PART_ROWS, tb),
                         lambda s: (2 * (item_b(s) % PAIRS) + 1, item_b(s) // PAIRS)),
            full(cw), full(cbias), full(scw), full(dskip_e), full(nw), full(e_t),
        ],
        out_specs=([pl.BlockSpec((tb, w), out_a) for w in widths]
                   + [pl.BlockSpec((tb, w), out_b) for w in widths]),
        out_shape=out_shapes + out_shapes,
        scratch_shapes=[
            pltpu.VMEM((tb // SUBLANES, SUBLANES, GROUP_COLS), jnp.float32),
            pltpu.VMEM((tb // SUBLANES, SUBLANES, GROUP_COLS), jnp.float32),
            pltpu.VMEM((GROUPS + 1, HEADS_PER_GROUP, HEAD_DIM, STATE), jnp.float32),
            pltpu.VMEM((GROUPS + 1, SSM_CONV_WIDTH - 1, SUBLANES, GROUP_CONV), jnp.float32),
            pltpu.VMEM((GROUPS + 1, CONV_WIDTH - 1, SUBLANES, SHORT_PER_GROUP), jnp.float32),
        ],
        compiler_params=_cparams(1, VMEM_LIMIT_MIXER),
        name="mixer",
    )(u, w_pairs, parts, parts, cw, cbias, scw, dskip_e, nw, e_t)


def _out_kernel(yca_ref, ycb_ref, ysa_ref, ysb_ref, ga_ref, gb_ref, x_ref, mod_ref,
                wc_ref, ws_ref, wo_ref, o_ref):
    p_conv = _dot(jnp.concatenate([yca_ref[...], ycb_ref[...]], axis=1), wc_ref[0])
    y_ssm = jnp.concatenate(
        [ref[:, i * GROUP_X:(i + 1) * GROUP_X] for i in range(PAIRS) for ref in (ysa_ref, ysb_ref)],
        axis=1)
    p_ssm = _dot(y_ssm, ws_ref[0])
    g_conv = 1.0 / (1.0 + jnp.exp(-ga_ref[...].astype(jnp.float32)))
    g_ssm = 1.0 / (1.0 + jnp.exp(-gb_ref[...].astype(jnp.float32)))
    merged = g_conv * p_conv + g_ssm * p_ssm
    mix = _dot(merged.astype(jnp.bfloat16), wo_ref[0])
    gate1 = mod_ref[0, 2:3, :]
    o_ref[...] = x_ref[...] + gate1 * mix


def _out_call(mixed, x2, mod_l, wc, ws, wo, layer, seq, tm=512):
    ys_a, yc_a, g_a, ys_b, yc_b, g_b = mixed
    t_total = x2.shape[0]
    tiles_per_batch = seq // tm
    row = lambda width: pl.BlockSpec((tm, width), lambda i: (i, 0))
    const = lambda shape: pl.BlockSpec((1,) + shape, lambda i: (layer, 0, 0))
    return pl.pallas_call(
        _out_kernel,
        grid=(t_total // tm,),
        in_specs=[
            row(CONV_DIM // 2), row(CONV_DIM // 2), row(D_SSM // 2), row(D_SSM // 2),
            row(D_MODEL), row(D_MODEL), row(D_MODEL),
            pl.BlockSpec((1, MOD_ROWS, D_MODEL), lambda i: (i // tiles_per_batch, 0, 0)),
            const((CONV_DIM, D_MODEL)), const((D_SSM, D_MODEL)), const((D_MODEL, D_MODEL)),
        ],
        out_specs=pl.BlockSpec((tm, D_MODEL), lambda i: (i, 0)),
        out_shape=jax.ShapeDtypeStruct((t_total, D_MODEL), jnp.float32),
        compiler_params=_cparams(1),
        name="mix_out",
    )(yc_a, yc_b, ys_a, ys_b, g_a, g_b, x2, mod_l, wc, ws, wo)


def _mlp_kernel(x_ref, mod_ref, ln_ref, wup_ref, wdn_ref, fn_ref, fromperm_ref, o_ref, u_ref,
                acc_ref, *, tm, final):
    j = pl.program_id(1)

    @pl.when(j == 0)
    def _():
        shift = mod_ref[0, 3:4, :]
        scale = mod_ref[0, 4:5, :]
        _modnorm_to(x_ref, ln_ref, scale, shift, u_ref, tm)
        acc_ref[...] = jnp.zeros_like(acc_ref)

    h = jnp.maximum(_dot(u_ref[...], wup_ref[0]), 0.0)
    acc_ref[...] += _dot((h * h).astype(jnp.bfloat16), wdn_ref[0])

    @pl.when(j == pl.num_programs(1) - 1)
    def _():
        gate2 = mod_ref[0, 5:6, :]
        xn = x_ref[...] + gate2 * acc_ref[...]
        if final:
            ms = jnp.mean(xn * xn, axis=-1, keepdims=True)
            acc_ref[...] = xn * lax.rsqrt(ms + EPS) * fn_ref[...]
            _permute_rows(acc_ref, o_ref, fromperm_ref[...], tm)
        else:
            o_ref[...] = xn


def _mlp_call(x2, mod_l, ln, w_up, w_dn, layer, fnorm, from_perm, seq, final, tm=1024, tf=1024):
    t_total = x2.shape[0]
    tiles_per_batch = seq // tm
    return pl.pallas_call(
        functools.partial(_mlp_kernel, tm=tm, final=final),
        grid=(t_total // tm, D_FF // tf),
        in_specs=[
            pl.BlockSpec((tm, D_MODEL), lambda i, j: (i, 0)),
            pl.BlockSpec((1, MOD_ROWS, D_MODEL), lambda i, j: (i // tiles_per_batch, 0, 0)),
            pl.BlockSpec((1, D_MODEL), lambda i, j: (0, 0)),
            pl.BlockSpec((1, D_MODEL, tf), lambda i, j: (layer, 0, j)),
            pl.BlockSpec((1, tf, D_MODEL), lambda i, j: (layer, j, 0)),
            pl.BlockSpec((1, D_MODEL), lambda i, j: (0, 0)),
            pl.BlockSpec((SCAN_CHUNK, SCAN_CHUNK), lambda i, j: (0, 0)),
        ],
        out_specs=pl.BlockSpec((tm, D_MODEL), lambda i, j: (i, 0)),
        out_shape=jax.ShapeDtypeStruct((t_total, D_MODEL), jnp.float32),
        scratch_shapes=[pltpu.VMEM((tm, D_MODEL), jnp.bfloat16),
                        pltpu.VMEM((tm, D_MODEL), jnp.float32)],
        compiler_params=_cparams(2),
        name="mlp",
    )(x2, mod_l, ln, w_up, w_dn, fnorm, from_perm)


def _head_constants():
    e_t = np.zeros((PART_ROWS, HEADS_PER_GROUP, LANES), np.float32)
    perm = np.zeros((GROUPS, PART_ROWS, 2 * N_SPLIT, HEADS), np.float32)
    for k in range(2 * N_SPLIT):
        for r in range(HEADS_PER_GROUP):
            if k < N_SPLIT:
                e_t[k * HEADS_PER_GROUP + r, r, :] = 1.0
            for g in range(GROUPS):
                perm[g, k * HEADS_PER_GROUP + r, k, g * HEADS_PER_GROUP + r] = 1.0
    e_t = e_t.reshape(PART_ROWS, HEADS_PER_GROUP * LANES)
    perm = perm.reshape(GROUPS * PART_ROWS, 2 * N_SPLIT * HEADS)
    tm = _chunk_time(np.arange(SCAN_CHUNK))
    cum = (tm[None, :] <= tm[:, None]).astype(np.float32)
    to_perm = (np.arange(SCAN_CHUNK)[None, :] == tm[:, None]).astype(np.float32)
    bf = jnp.bfloat16
    return (jnp.asarray(e_t, bf), jnp.asarray(perm, bf), jnp.asarray(cum, bf),
            jnp.asarray(to_perm, bf), jnp.asarray(to_perm.T, bf))


def _regroup_kernel(xs_ref, bs_ref, cs_ref, z_ref, cb_ref, cc_ref, cx_ref, gate_ref, dt_ref,
                    o_ref, odt_ref):
    for ref, lo, hi in ((xs_ref, G_XBC, G_XBC + GROUP_X), (bs_ref, G_XBC + GROUP_X, G_Z - STATE),
                        (cs_ref, G_Z - STATE, G_Z), (z_ref, G_Z, G_CB), (cb_ref, G_CB, G_CC),
                        (cc_ref, G_CC, G_CX), (cx_ref, G_CX, G_GATE), (gate_ref, G_GATE, GROUP_COLS)):
        o_ref[0, lo:hi, :] = ref[0].astype(o_ref.dtype)

    @pl.when(pl.program_id(1) == 0)
    def _():
        row = lax.broadcasted_iota(jnp.int32, (LANES, D_MODEL), 0)
        odt_ref[0] = jnp.where(row < HEADS, dt_ref[0], 0.0).astype(odt_ref.dtype)


def _regroup_w_in(w_t):
    def rows(height, off, paired):
        base = off // height
        if paired:
            return pl.BlockSpec((1, height, D_MODEL),
                                lambda l, g: (l, base + (g % 2) * PAIRS + g // 2, 0))
        return pl.BlockSpec((1, height, D_MODEL), lambda l, g: (l, base + g, 0))

    return pl.pallas_call(
        _regroup_kernel,
        grid=(DEPTH, GROUPS),
        in_specs=[
            rows(GROUP_X, OFF_XS, False), rows(STATE, OFF_BS, False), rows(STATE, OFF_CS, False),
            rows(GROUP_X, OFF_Z, False),
            rows(SHORT_PER_GROUP, OFF_CB, True), rows(SHORT_PER_GROUP, OFF_CC, True),
            rows(SHORT_PER_GROUP, OFF_CX, True), rows(GATE_PER_GROUP, 0, True),
            pl.BlockSpec((1, LANES, D_MODEL), lambda l, g: (l, OFF_DT // LANES, 0)),
        ],
        out_specs=[
            pl.BlockSpec((1, GROUP_COLS, D_MODEL), lambda l, g: (l, g, 0)),
            pl.BlockSpec((1, LANES, D_MODEL), lambda l, g: (l, 0, 0)),
        ],
        out_shape=[
            jax.ShapeDtypeStruct((DEPTH, GROUPS * GROUP_COLS, D_MODEL), jnp.bfloat16),
            jax.ShapeDtypeStruct((DEPTH, LANES, D_MODEL), jnp.bfloat16),
        ],
        compiler_params=_cparams(2),
        name="w_in_regroup",
    )(*([w_t] * 9))


def _group_conv_params(p):
    k = p.shape[1]
    xs = p[..., :D_SSM].reshape(DEPTH, k, GROUPS, GROUP_X)
    bs = p[..., D_SSM:D_SSM + GROUPS * STATE].reshape(DEPTH, k, GROUPS, STATE)
    cs = p[..., D_SSM + GROUPS * STATE:].reshape(DEPTH, k, GROUPS, STATE)
    return jnp.swapaxes(jnp.concatenate([xs, bs, cs], axis=-1), 1, 2)


def kernel(x, c, w_ada, b_ada, ln1, ln2, w_in, conv_w, ssm_conv_w, ssm_conv_b, dt_bias, a_log,
           d_skip, ssm_norm_w, w_conv_out, w_ssm_out, w_o, w_up, w_down, final_norm):
    bsz, seq, _ = x.shape
    bf = jnp.bfloat16
    f32 = jnp.float32

    c_pad = jnp.zeros((SUBLANES, D_MODEL), f32).at[:bsz].set(c)
    mod_all = _ada_call(c_pad, w_ada, b_ada)
    mod_all = mod_all[:, :bsz].reshape(DEPTH, bsz, N_MOD, D_MODEL)
    mod_all = jnp.pad(mod_all, ((0, 0), (0, 0), (0, MOD_ROWS - N_MOD), (0, 0)))

    w_pairs, w_dt = _regroup_w_in(jnp.swapaxes(w_in, 1, 2))
    hp = jnp.pad(jnp.stack([dt_bias, a_log], axis=1),
                 ((0, 0), (0, SUBLANES - 2), (0, LANES - HEADS)))
    cw = _group_conv_params(ssm_conv_w)
    cbias = _group_conv_params(ssm_conv_b[:, None, :])
    scw = conv_w.reshape(DEPTH, CONV_WIDTH, 2, PAIRS, SHORT_PER_GROUP)
    scw = jnp.transpose(scw, (0, 3, 2, 1, 4)).reshape(DEPTH, GROUPS, CONV_WIDTH, SHORT_PER_GROUP)
    dskip_e = jnp.repeat(d_skip, HEAD_DIM, axis=-1).reshape(DEPTH, GROUPS, 1, GROUP_X)
    nw = ssm_norm_w.reshape(DEPTH, GROUPS, 1, GROUP_X)
    ws = w_ssm_out.astype(bf)
    wc = w_conv_out.astype(bf)
    wo = w_o.astype(bf)
    wup = w_up.astype(bf)
    wdn = w_down.astype(bf)

    e_t, perm, cum, to_perm, from_perm = _head_constants()
    fnorm = final_norm.reshape(1, D_MODEL)
    x2 = x.reshape(bsz * seq, D_MODEL)

    for l in range(DEPTH):
        mod_l = mod_all[l]
        pre_args = (mod_l, ln1[l].reshape(1, D_MODEL), w_dt[l], hp[l], cum, perm, seq)
        if l == 0:
            x2, u, parts = _pre_call(x2, *pre_args, to_perm=to_perm)
        else:
            u, parts = _pre_call(x2, *pre_args)
        mixed = _mixer_call(u, w_pairs, l, parts, cw[l], cbias[l], scw[l], dskip_e[l], nw[l], e_t,
                            bsz, seq)
        x2 = _out_call(mixed, x2, mod_l, wc, ws, wo, l, seq)
        x2 = _mlp_call(x2, mod_l, ln2[l].reshape(1, D_MODEL), wup, wdn, l, fnorm, from_perm, seq,
                       final=(l == DEPTH - 1))
    return x2.reshape(bsz, seq, D_MODEL)
```

```python
import functools

import numpy as np
import jax
import jax.numpy as jnp
from jax import lax
from jax.experimental import pallas as pl
from jax.experimental.pallas import tpu as pltpu

D_MODEL = 1024
DEPTH = 4
EPS = 1e-6
N_MOD = 6
MOD_ROWS = 8
CONV_DIM = D_MODEL
CONV_WIDTH = 3
D_SSM = 2 * D_MODEL
HEAD_DIM = 64
HEADS = D_SSM // HEAD_DIM
GROUPS = 8
PAIRS = GROUPS // 2
HEADS_PER_GROUP = HEADS // GROUPS
GROUP_X = HEADS_PER_GROUP * HEAD_DIM
STATE = 128
SSM_CONV_WIDTH = 4
D_FF = 4 * D_MODEL
LANES = 128
SUBLANES = 8
F32_TINY = float(np.finfo(np.float32).tiny)
LOG2E = float(np.log2(np.e))

OFF_CB = 2 * D_MODEL
OFF_CC = OFF_CB + CONV_DIM
OFF_CX = OFF_CC + CONV_DIM
OFF_Z = OFF_CX + CONV_DIM
OFF_XS = OFF_Z + D_SSM
OFF_BS = OFF_XS + D_SSM
OFF_CS = OFF_BS + GROUPS * STATE
OFF_DT = OFF_CS + GROUPS * STATE

SCAN_CHUNK = 128
SLABS = SCAN_CHUNK // SUBLANES
GROUP_CONV = GROUP_X + 2 * STATE
SHORT_PER_GROUP = CONV_DIM // GROUPS
GATE_PER_GROUP = 2 * D_MODEL // GROUPS
G_XBC = 0
G_Z = G_XBC + GROUP_CONV
G_CB = G_Z + GROUP_X
G_CC = G_CB + SHORT_PER_GROUP
G_CX = G_CC + SHORT_PER_GROUP
G_GATE = G_CX + SHORT_PER_GROUP
GROUP_COLS = G_GATE + GATE_PER_GROUP
PROJ_PIECE = 256
PIECES_WHILE_ODD = {0: (0,), 1: (1,), 2: (2,), 4: (3,), 5: (4,), 7: (5,)}
PIECES_WHILE_EVEN = {0: (6,), 1: (7,), 3: (8,), 4: (9,), 6: (10,)}

N_SPLIT = 3
PART_ROWS = 32
BETA_ROW0 = N_SPLIT * HEADS_PER_GROUP

VMEM_LIMIT = 48 * 1024 * 1024
VMEM_LIMIT_MIXER = 56 * 1024 * 1024


def _cparams(n_axes, limit=VMEM_LIMIT):
    return pltpu.CompilerParams(
        dimension_semantics=("arbitrary",) * n_axes,
        vmem_limit_bytes=limit)


def _dot(a, b):
    return jnp.dot(a, b, preferred_element_type=jnp.float32)


def _dot_t(a, b_t):
    return lax.dot_general(a, b_t, (((1,), (1,)), ((), ())), preferred_element_type=jnp.float32)


def _split3(v):
    hi = v.astype(jnp.bfloat16)
    r1 = v - hi.astype(jnp.float32)
    mid = r1.astype(jnp.bfloat16)
    lo = (r1 - mid.astype(jnp.float32)).astype(jnp.bfloat16)
    return hi, mid, lo


def _silu(v):
    return v / (1.0 + jnp.exp2(v * (-LOG2E)))


def _chunk_time(p):
    return (p % SUBLANES) * SLABS + p // SUBLANES


def _ada_kernel(c_ref, w_ref, b_ref, o_ref):
    c = c_ref[...]
    o_ref[0] = _dot(_silu(c), w_ref[0]) + b_ref[0]


def _ada_call(c_pad, w_ada, b_ada):
    n_out = N_MOD * D_MODEL
    tn = D_MODEL
    return pl.pallas_call(
        _ada_kernel,
        grid=(DEPTH, n_out // tn),
        in_specs=[
            pl.BlockSpec((SUBLANES, D_MODEL), lambda l, j: (0, 0)),
            pl.BlockSpec((1, D_MODEL, tn), lambda l, j: (l, 0, j)),
            pl.BlockSpec((1, 1, tn), lambda l, j: (l, 0, j)),
        ],
        out_specs=pl.BlockSpec((1, SUBLANES, tn), lambda l, j: (l, 0, j)),
        out_shape=jax.ShapeDtypeStruct((DEPTH, SUBLANES, n_out), jnp.float32),
        compiler_params=_cparams(2),
        name="ada_mod",
    )(c_pad, w_ada, b_ada.reshape(DEPTH, 1, n_out))


def _modnorm_to(x_ref, ln_ref, scale, shift, u_ref, rows, chunk=256):
    w = ln_ref[...] * (1.0 + scale)

    def body(r, carry):
        r0 = pl.multiple_of(r * chunk, chunk)
        xf = x_ref[pl.ds(r0, chunk), :]
        ms = jnp.mean(xf * xf, axis=-1, keepdims=True)
        y = xf * lax.rsqrt(ms + EPS)
        u_ref[pl.ds(r0, chunk), :] = (y * w + shift).astype(u_ref.dtype)
        return carry

    lax.fori_loop(0, rows // chunk, body, 0)


def _permute_rows(src_ref, dst_ref, onehot_b, rows):
    def body(c, carry):
        r0 = pl.multiple_of(c * SCAN_CHUNK, SCAN_CHUNK)
        hi, mid, lo = _split3(src_ref[pl.ds(r0, SCAN_CHUNK), :])
        dst_ref[pl.ds(r0, SCAN_CHUNK), :] = (_dot(onehot_b, hi) + _dot(onehot_b, mid)
                                             + _dot(onehot_b, lo)).astype(dst_ref.dtype)
        return carry

    lax.fori_loop(0, rows // SCAN_CHUNK, body, 0)


def _pre_kernel(*refs, tp, permute_in):
    if permute_in:
        (xnat_ref, mod_ref, ln_ref, wdt_ref, hp_ref, cum_ref, perm_ref, toperm_ref,
         x_ref, u_ref, parts_ref) = refs
        _permute_rows(xnat_ref, x_ref, toperm_ref[...], tp)
    else:
        x_ref, mod_ref, ln_ref, wdt_ref, hp_ref, cum_ref, perm_ref, u_ref, parts_ref = refs
    shift = mod_ref[0, 0:1, :]
    scale = mod_ref[0, 1:2, :]
    _modnorm_to(x_ref, ln_ref, scale, shift, u_ref, tp)
    dt_raw = _dot_t(u_ref[...], wdt_ref[...])
    dt_bias = hp_ref[0:1, :]
    a_neg = -jnp.exp(hp_ref[1:2, :])
    cum_b = cum_ref[...]
    perm = perm_ref[...]
    q = SCAN_CHUNK
    for c in range(tp // q):
        r0 = c * q
        dtr = dt_raw[r0:r0 + q, :] + dt_bias
        dt = jnp.maximum(dtr, 0.0) + jnp.log1p(jnp.exp(-jnp.abs(dtr)))
        a_parts = _split3(dt * a_neg)
        acum = _dot(cum_b, a_parts[0]) + _dot(cum_b, a_parts[1]) + _dot(cum_b, a_parts[2])
        beta = acum - jnp.log(jnp.maximum(dt, F32_TINY))
        acum_t = (acum * LOG2E).T[0:HEADS, :]
        beta_t = (beta * LOG2E).T[0:HEADS, :]
        terms = jnp.concatenate(_split3(acum_t) + _split3(beta_t), axis=0)
        parts_ref[:, r0:r0 + q] = _dot(perm, terms).astype(parts_ref.dtype)


def _pre_call(x2, mod_l, ln, w_dt, hp, cum, perm, seq, to_perm=None, tp=1024):
    t_total = x2.shape[0]
    tiles_per_batch = seq // tp
    permute_in = to_perm is not None
    square = pl.BlockSpec((SCAN_CHUNK, SCAN_CHUNK), lambda i: (0, 0))
    rows = pl.BlockSpec((tp, D_MODEL), lambda i: (i, 0))
    return pl.pallas_call(
        functools.partial(_pre_kernel, tp=tp, permute_in=permute_in),
        grid=(t_total // tp,),
        in_specs=[
            rows,
            pl.BlockSpec((1, MOD_ROWS, D_MODEL), lambda i: (i // tiles_per_batch, 0, 0)),
            pl.BlockSpec((1, D_MODEL), lambda i: (0, 0)),
            pl.BlockSpec((LANES, D_MODEL), lambda i: (0, 0)),
            pl.BlockSpec((SUBLANES, LANES), lambda i: (0, 0)),
            square,
            pl.BlockSpec((GROUPS * PART_ROWS, 2 * N_SPLIT * HEADS), lambda i: (0, 0)),
        ] + ([square] if permute_in else []),
        out_specs=([rows] if permute_in else []) + [
            rows,
            pl.BlockSpec((GROUPS * PART_ROWS, tp), lambda i: (0, i)),
        ],
        out_shape=([jax.ShapeDtypeStruct((t_total, D_MODEL), jnp.float32)] if permute_in else []) + [
            jax.ShapeDtypeStruct((t_total, D_MODEL), jnp.bfloat16),
            jax.ShapeDtypeStruct((GROUPS * PART_ROWS, t_total), jnp.bfloat16),
        ],
        compiler_params=_cparams(1),
        name="pre",
    )(*((x2, mod_l, ln, w_dt, hp, cum, perm) + ((to_perm,) if permute_in else ())))


def _shifted_taps(cur, tail, width):
    n_wrap = width - 1
    cur_tail = cur[SLABS - n_wrap:]
    wrapped = jnp.concatenate([tail[:, SUBLANES - 1:, :], cur_tail[:, :SUBLANES - 1, :]], axis=1)
    taps = [cur]
    for s in range(1, width):
        taps.append(jnp.concatenate([wrapped[n_wrap - s:], cur[:SLABS - s]], axis=0))
    return taps, cur_tail


def _project_piece(u_ref, w_ref, pa_ref, pb_ref, piece, tb):
    lo = piece * PROJ_PIECE
    res = _dot(u_ref[...], w_ref[0, :, lo:lo + PROJ_PIECE]
               ).reshape(tb // SUBLANES, SUBLANES, PROJ_PIECE)
    n_a = min(max(GROUP_COLS - lo, 0), PROJ_PIECE)
    if n_a:
        pa_ref[:, :, lo:lo + n_a] = res[:, :, 0:n_a]
    if n_a < PROJ_PIECE:
        b0 = lo + n_a - GROUP_COLS
        pb_ref[:, :, b0:b0 + PROJ_PIECE - n_a] = res[:, :, n_a:]


def _mix_group(p_ref, parts_ref, g, first, cw_ref, cbias_ref, scw_ref, dskip_ref, nw_ref, e_t,
               causal, ys_ref, yc_ref, gate_ref, state_ref, tail_ref, stail_ref, tb, after_chunk,
               hist_src, keep_start):
    q = SCAN_CHUNK
    nt_dims = (((1,), (1,)), ((), ()))
    tn_dims = (((0,), (0,)), ((), ()))
    cw = cw_ref[g]
    cbias = cbias_ref[g]
    scw = scw_ref[g]
    dskip = dskip_ref[g]
    nw = nw_ref[g]

    gate_ref[...] = p_ref[:, :, G_GATE:GROUP_COLS].reshape(tb, GATE_PER_GROUP).astype(gate_ref.dtype)

    state_in = jnp.where(first, 0.0, state_ref[hist_src])
    tail = jnp.where(first, 0.0, tail_ref[hist_src])
    stail = jnp.where(first, 0.0, stail_ref[hist_src])
    if keep_start:
        state_ref[GROUPS] = state_in
        tail_ref[GROUPS] = tail
        stail_ref[GROUPS] = stail
    state_ref[g] = state_in

    for c in range(tb // q):
        r0 = c * q
        b0 = c * SLABS
        sv = p_ref[b0:b0 + SLABS, :, G_CC:G_CX] * p_ref[b0:b0 + SLABS, :, G_CX:G_GATE]
        staps, stail = _shifted_taps(sv, stail, CONV_WIDTH)
        sconv = scw[CONV_WIDTH - 1:CONV_WIDTH, :] * staps[0]
        for s in range(1, CONV_WIDTH):
            sconv += scw[CONV_WIDTH - 1 - s:CONV_WIDTH - s, :] * staps[s]
        yc = p_ref[b0:b0 + SLABS, :, G_CB:G_CC] * sconv
        yc_ref[r0:r0 + q, :] = yc.reshape(q, SHORT_PER_GROUP).astype(yc_ref.dtype)

        taps, tail = _shifted_taps(p_ref[b0:b0 + SLABS, :, G_XBC:G_Z], tail, SSM_CONV_WIDTH)
        acc = cbias + cw[SSM_CONV_WIDTH - 1:SSM_CONV_WIDTH, :] * taps[0]
        for s in range(1, SSM_CONV_WIDTH):
            acc += cw[SSM_CONV_WIDTH - 1 - s:SSM_CONV_WIDTH - s, :] * taps[s]
        xbc = _silu(acc).reshape(q, GROUP_CONV)
        xs = xbc[:, 0:GROUP_X]
        xt_b = xs.astype(jnp.bfloat16).T
        b_b = xbc[:, GROUP_X:GROUP_X + STATE].astype(jnp.bfloat16)
        c_b = xbc[:, GROUP_X + STATE:GROUP_CONV].astype(jnp.bfloat16)

        mc = parts_ref[:, r0:r0 + q]
        acum_l = lax.dot_general(mc, e_t, tn_dims, preferred_element_type=jnp.float32)
        acum_last = acum_l[q - 1:q, :]
        mf = mc.astype(jnp.float32)

        scores_b = lax.dot_general(c_b, b_b, nt_dims, preferred_element_type=jnp.float32
                                   ).astype(jnp.bfloat16)
        ys = []
        for r in range(HEADS_PER_GROUP):
            al = acum_l[:, r * LANES:(r + 1) * LANES]
            last = acum_last[:, r * LANES:(r + 1) * LANES]
            beta_row = (mf[BETA_ROW0 + r:BETA_ROW0 + r + 1, :]
                        + mf[BETA_ROW0 + HEADS_PER_GROUP + r:BETA_ROW0 + HEADS_PER_GROUP + r + 1, :]
                        + mf[BETA_ROW0 + 2 * HEADS_PER_GROUP + r:
                             BETA_ROW0 + 2 * HEADS_PER_GROUP + r + 1, :])
            seg = (al - beta_row).astype(jnp.bfloat16)
            w_r = scores_b * jnp.exp2(jnp.where(causal, seg, -jnp.inf))
            c_r = c_b * jnp.exp2(al).astype(jnp.bfloat16)
            xt_r = xt_b[r * HEAD_DIM:(r + 1) * HEAD_DIM, :]
            xw_r = xt_r * jnp.exp2(last - beta_row).astype(jnp.bfloat16)
            s_r = state_ref[g, r]
            ys.append(lax.dot_general(
                jnp.concatenate([w_r, c_r], axis=1),
                jnp.concatenate([xt_r, s_r.astype(jnp.bfloat16)], axis=1),
                nt_dims, preferred_element_type=jnp.float32))
            state_ref[g, r] = s_r * jnp.exp2(last) + _dot(xw_r, b_b)
        y = jnp.concatenate(ys, axis=-1)

        y = y + dskip * xs
        y = y * _silu(p_ref[b0:b0 + SLABS, :, G_Z:G_CB].reshape(q, GROUP_X))
        y = y * lax.rsqrt(jnp.mean(y * y, axis=-1, keepdims=True) + EPS)
        ys_ref[r0:r0 + q, :] = (y * nw).astype(ys_ref.dtype)
        after_chunk(c)

    tail_ref[g] = tail
    stail_ref[g] = stail


def _mixer_kernel(u_ref, w_ref, parts_a_ref, parts_b_ref,
                  cw_ref, cbias_ref, scw_ref, dskip_ref, nw_ref, et_ref,
                  ys_a_ref, yc_a_ref, gate_a_ref, ys_b_ref, yc_b_ref, gate_b_ref,
                  pa_ref, pb_ref, state_ref, tail_ref, stail_ref, *, tb, n_items, nt):
    s = pl.program_id(0)
    q = SCAN_CHUNK

    @pl.when(s == 0)
    def _():
        pb_ref[...] = jnp.zeros_like(pb_ref)
        state_ref[...] = jnp.zeros_like(state_ref)
        tail_ref[...] = jnp.zeros_like(tail_ref)
        stail_ref[...] = jnp.zeros_like(stail_ref)

    item_a = jnp.minimum(s, n_items - 1)
    item_b = jnp.maximum(s - 1, 0)
    g_a = 2 * (item_a % PAIRS)
    g_b = 2 * (item_b % PAIRS) + 1
    first_a = (item_a // PAIRS) % nt == 0
    first_b = (item_b // PAIRS) % nt == 0

    e_t = et_ref[...]
    t_row = _chunk_time(lax.broadcasted_iota(jnp.int32, (q, q), 0))
    t_col = _chunk_time(lax.broadcasted_iota(jnp.int32, (q, q), 1))
    causal = t_row >= t_col
    common = (cw_ref, cbias_ref, scw_ref, dskip_ref, nw_ref, e_t, causal)
    hist = (state_ref, tail_ref, stail_ref, tb)

    def project_after(schedule):
        def after_chunk(c):
            for piece in schedule.get(c, ()):
                _project_piece(u_ref, w_ref, pa_ref, pb_ref, piece, tb)
        return after_chunk

    _mix_group(pb_ref, parts_b_ref, g_b, first_b, *common, ys_b_ref, yc_b_ref, gate_b_ref, *hist,
               project_after(PIECES_WHILE_ODD), g_b, False)
    drain = s == n_items
    _mix_group(pa_ref, parts_a_ref, g_a, first_a, *common, ys_a_ref, yc_a_ref, gate_a_ref, *hist,
               project_after(PIECES_WHILE_EVEN), jnp.where(drain, GROUPS, g_a), True)


def _mixer_call(u, w_pairs, layer, parts, cw, cbias, scw, dskip_e, nw, e_t, bsz, seq, tb=1024):
    t_total = u.shape[0]
    nt = seq // tb
    n_blocks = t_total // tb
    n_items = n_blocks * PAIRS
    item_a = lambda s: jnp.minimum(s, n_items - 1)
    item_b = lambda s: jnp.maximum(s - 1, 0)
    out_a = lambda s: (item_a(s) // PAIRS, item_a(s) % PAIRS)
    out_b = lambda s: (item_b(s) // PAIRS, item_b(s) % PAIRS)

    def full(arr):
        nd = arr.ndim
        return pl.BlockSpec(arr.shape, lambda s: (0,) * nd)

    half = GROUPS // 2
    out_shapes = [
        jax.ShapeDtypeStruct((t_total, half *GROUP_X), jnp.bfloat16),
        jax.ShapeDtypeStruct((t_total, half *SHORT_PER_GROUP), jnp.bfloat16),
        jax.ShapeDtypeStruct((t_total, half *GATE_PER_GROUP), jnp.bfloat16),
    ]
    widths = (GROUP_X, SHORT_PER_GROUP, GATE_PER_GROUP)
    return pl.pallas_call(
        functools.partial(_mixer_kernel, tb=tb, n_items=n_items, nt=nt),
        grid=(n_items + 1,),
        in_specs=[
            pl.BlockSpec((tb, D_MODEL), lambda s: (item_a(s) // PAIRS, 0)),
            pl.BlockSpec((1, D_MODEL, 2 * GROUP_COLS), lambda s: (layer, 0, item_a(s) % PAIRS)),
            pl.BlockSpec((PART_ROWS, tb), lambda s: (2 * (item_a(s) % PAIRS), item_a(s) // PAIRS)),
            pl.BlockSpec((PART_ROWS, tb),
                         lambda s: (2 * (item_b(s) % PAIRS) + 1, item_b(s) // PAIRS)),
            full(cw), full(cbias), full(scw), full(dskip_e), full(nw), full(e_t),
        ],
        out_specs=([pl.BlockSpec((tb, w), out_a) for w in widths]
                   + [pl.BlockSpec((tb, w), out_b) for w in widths]),
        out_shape=out_shapes + out_shapes,
        scratch_shapes=[
            pltpu.VMEM((tb // SUBLANES, SUBLANES, GROUP_COLS), jnp.float32),
            pltpu.VMEM((tb // SUBLANES, SUBLANES, GROUP_COLS), jnp.float32),
            pltpu.VMEM((GROUPS + 1, HEADS_PER_GROUP, HEAD_DIM, STATE), jnp.float32),
            pltpu.VMEM((GROUPS + 1, SSM_CONV_WIDTH - 1, SUBLANES, GROUP_CONV), jnp.float32),
            pltpu.VMEM((GROUPS + 1, CONV_WIDTH - 1, SUBLANES, SHORT_PER_GROUP), jnp.float32),
        ],
        compiler_params=_cparams(1, VMEM_LIMIT_MIXER),
        name="mixer",
    )(u, w_pairs, parts, parts, cw, cbias, scw, dskip_e, nw, e_t)


def _out_kernel(yca_ref, ycb_ref, ysa_ref, ysb_ref, ga_ref, gb_ref, x_ref, mod_ref,
                wc_ref, ws_ref, wo_ref, o_ref):
    p_conv = _dot(jnp.concatenate([yca_ref[...], ycb_ref[...]], axis=1), wc_ref[0])
    y_ssm = jnp.concatenate(
        [ref[:, i * GROUP_X:(i + 1) * GROUP_X] for i in range(PAIRS) for ref in (ysa_ref, ysb_ref)],
        axis=1)
    p_ssm = _dot(y_ssm, ws_ref[0])
    g_conv = 1.0 / (1.0 + jnp.exp(-ga_ref[...].astype(jnp.float32)))
    g_ssm = 1.0 / (1.0 + jnp.exp(-gb_ref[...].astype(jnp.float32)))
    merged = g_conv * p_conv + g_ssm * p_ssm
    mix = _dot(merged.astype(jnp.bfloat16), wo_ref[0])
    gate1 = mod_ref[0, 2:3, :]
    o_ref[...] = x_ref[...] + gate1 * mix


def _out_call(mixed, x2, mod_l, wc, ws, wo, layer, seq, tm=512):
    ys_a, yc_a, g_a, ys_b, yc_b, g_b = mixed
    t_total = x2.shape[0]
    tiles_per_batch = seq // tm
    row = lambda width: pl.BlockSpec((tm, width), lambda i: (i, 0))
    const = lambda shape: pl.BlockSpec((1,) + shape, lambda i: (layer, 0, 0))
    return pl.pallas_call(
        _out_kernel,
        grid=(t_total // tm,),
        in_specs=[
            row(CONV_DIM // 2), row(CONV_DIM // 2), row(D_SSM // 2), row(D_SSM // 2),
            row(D_MODEL), row(D_MODEL), row(D_MODEL),
            pl.BlockSpec((1, MOD_ROWS, D_MODEL), lambda i: (i // tiles_per_batch, 0, 0)),
            const((CONV_DIM, D_MODEL)), const((D_SSM, D_MODEL)), const((D_MODEL, D_MODEL)),
        ],
        out_specs=pl.BlockSpec((tm, D_MODEL), lambda i: (i, 0)),
        out_shape=jax.ShapeDtypeStruct((t_total, D_MODEL), jnp.float32),
        compiler_params=_cparams(1),
        name="mix_out",
    )(yc_a, yc_b, ys_a, ys_b, g_a, g_b, x2, mod_l, wc, ws, wo)


def _mlp_kernel(x_ref, mod_ref, ln_ref, wup_ref, wdn_ref, fn_ref, fromperm_ref, o_ref, u_ref,
                acc_ref, *, tm, final):
    j = pl.program_id(1)

    @pl.when(j == 0)
    def _():
        shift = mod_ref[0, 3:4, :]
        scale = mod_ref[0, 4:5, :]
        _modnorm_to(x_ref, ln_ref, scale, shift, u_ref, tm)
        acc_ref[...] = jnp.zeros_like(acc_ref)

    h = jnp.maximum(_dot(u_ref[...], wup_ref[0]), 0.0)
    acc_ref[...] += _dot((h * h).astype(jnp.bfloat16), wdn_ref[0])

    @pl.when(j == pl.num_programs(1) - 1)
    def _():
        gate2 = mod_ref[0, 5:6, :]
        xn = x_ref[...] + gate2 * acc_ref[...]
        if final:
            ms = jnp.mean(xn * xn, axis=-1, keepdims=True)
            acc_ref[...] = xn * lax.rsqrt(ms + EPS) * fn_ref[...]
            _permute_rows(acc_ref, o_ref, fromperm_ref[...], tm)
        else:
            o_ref[...] = xn


def _mlp_call(x2, mod_l, ln, w_up, w_dn, layer, fnorm, from_perm, seq, final, tm=1024, tf=1024):
    t_total = x2.shape[0]
    tiles_per_batch = seq // tm
    return pl.pallas_call(
        functools.partial(_mlp_kernel, tm=tm, final=final),
        grid=(t_total // tm, D_FF // tf),
        in_specs=[
            pl.BlockSpec((tm, D_MODEL), lambda i, j: (i, 0)),
            pl.BlockSpec((1, MOD_ROWS, D_MODEL), lambda i, j: (i // tiles_per_batch, 0, 0)),
            pl.BlockSpec((1, D_MODEL), lambda i, j: (0, 0)),
            pl.BlockSpec((1, D_MODEL, tf), lambda i, j: (layer, 0, j)),
            pl.BlockSpec((1, tf, D_MODEL), lambda i, j: (layer, j, 0)),
            pl.BlockSpec((1, D_MODEL), lambda i, j: (0, 0)),
            pl.BlockSpec((SCAN_CHUNK, SCAN_CHUNK), lambda i, j: (0, 0)),
        ],
        out_specs=pl.BlockSpec((tm, D_MODEL), lambda i, j: (i, 0)),
        out_shape=jax.ShapeDtypeStruct((t_total, D_MODEL), jnp.float32),
        scratch_shapes=[pltpu.VMEM((tm, D_MODEL), jnp.bfloat16),
                        pltpu.VMEM((tm, D_MODEL), jnp.float32)],
        compiler_params=_cparams(2),
        name="mlp",
    )(x2, mod_l, ln, w_up, w_dn, fnorm, from_perm)


def _head_constants():
    e_t = np.zeros((PART_ROWS, HEADS_PER_GROUP, LANES), np.float32)
    perm = np.zeros((GROUPS, PART_ROWS, 2 * N_SPLIT, HEADS), np.float32)
    for k in range(2 * N_SPLIT):
        for r in range(HEADS_PER_GROUP):
            if k < N_SPLIT:
                e_t[k * HEADS_PER_GROUP + r, r, :] = 1.0
            for g in range(GROUPS):
                perm[g, k * HEADS_PER_GROUP + r, k, g * HEADS_PER_GROUP + r] = 1.0
    e_t = e_t.reshape(PART_ROWS, HEADS_PER_GROUP * LANES)
    perm = perm.reshape(GROUPS * PART_ROWS, 2 * N_SPLIT * HEADS)
    tm = _chunk_time(np.arange(SCAN_CHUNK))
    cum = (tm[None, :] <= tm[:, None]).astype(np.float32)
    to_perm = (np.arange(SCAN_CHUNK)[None, :] == tm[:, None]).astype(np.float32)
    bf = jnp.bfloat16
    return (jnp.asarray(e_t, bf), jnp.asarray(perm, bf), jnp.asarray(cum, bf),
            jnp.asarray(to_perm, bf), jnp.asarray(to_perm.T, bf))


def _regroup_kernel(xs_ref, bs_ref, cs_ref, z_ref, cb_ref, cc_ref, cx_ref, gate_ref, dt_ref,
                    o_ref, odt_ref):
    for ref, lo, hi in ((xs_ref, G_XBC, G_XBC + GROUP_X), (bs_ref, G_XBC + GROUP_X, G_Z - STATE),
                        (cs_ref, G_Z - STATE, G_Z), (z_ref, G_Z, G_CB), (cb_ref, G_CB, G_CC),
                        (cc_ref, G_CC, G_CX), (cx_ref, G_CX, G_GATE), (gate_ref, G_GATE, GROUP_COLS)):
        o_ref[0, :, lo:hi] = ref[0].T.astype(o_ref.dtype)

    @pl.when(pl.program_id(1) == 0)
    def _():
        row = lax.broadcasted_iota(jnp.int32, (LANES, D_MODEL), 0)
        odt_ref[0] = jnp.where(row < HEADS, dt_ref[0], 0.0).astype(odt_ref.dtype)


def _regroup_w_in(w_t):
    def rows(height, off, paired):
        base = off // height
        if paired:
            return pl.BlockSpec((1, height, D_MODEL),
                                lambda l, g: (l, base + (g % 2) * PAIRS + g // 2, 0))
        return pl.BlockSpec((1, height, D_MODEL), lambda l, g: (l, base + g, 0))

    return pl.pallas_call(
        _regroup_kernel,
        grid=(DEPTH, GROUPS),
        in_specs=[
            rows(GROUP_X, OFF_XS, False), rows(STATE, OFF_BS, False), rows(STATE, OFF_CS, False),
            rows(GROUP_X, OFF_Z, False),
            rows(SHORT_PER_GROUP, OFF_CB, True), rows(SHORT_PER_GROUP, OFF_CC, True),
            rows(SHORT_PER_GROUP, OFF_CX, True), rows(GATE_PER_GROUP, 0, True),
            pl.BlockSpec((1, LANES, D_MODEL), lambda l, g: (l, OFF_DT // LANES, 0)),
        ],
        out_specs=[
            pl.BlockSpec((1, D_MODEL, GROUP_COLS), lambda l, g: (l, 0, g)),
            pl.BlockSpec((1, LANES, D_MODEL), lambda l, g: (l, 0, 0)),
        ],
        out_shape=[
            jax.ShapeDtypeStruct((DEPTH, D_MODEL, GROUPS * GROUP_COLS), jnp.bfloat16),
            jax.ShapeDtypeStruct((DEPTH, LANES, D_MODEL), jnp.bfloat16),
        ],
        compiler_params=_cparams(2),
        name="w_in_regroup",
    )(*([w_t] * 9))


def _group_conv_params(p):
    k = p.shape[1]
    xs = p[..., :D_SSM].reshape(DEPTH, k, GROUPS, GROUP_X)
    bs = p[..., D_SSM:D_SSM + GROUPS * STATE].reshape(DEPTH, k, GROUPS, STATE)
    cs = p[..., D_SSM + GROUPS * STATE:].reshape(DEPTH, k, GROUPS, STATE)
    return jnp.swapaxes(jnp.concatenate([xs, bs, cs], axis=-1), 1, 2)


def kernel(x, c, w_ada, b_ada, ln1, ln2, w_in, conv_w, ssm_conv_w, ssm_conv_b, dt_bias, a_log,
           d_skip, ssm_norm_w, w_conv_out, w_ssm_out, w_o, w_up, w_down, final_norm):
    bsz, seq, _ = x.shape
    bf = jnp.bfloat16
    f32 = jnp.float32

    c_pad = jnp.zeros((SUBLANES, D_MODEL), f32).at[:bsz].set(c)
    mod_all = _ada_call(c_pad, w_ada, b_ada)
    mod_all = mod_all[:, :bsz].reshape(DEPTH, bsz, N_MOD, D_MODEL)
    mod_all = jnp.pad(mod_all, ((0, 0), (0, 0), (0, MOD_ROWS - N_MOD), (0, 0)))

    w_pairs, w_dt = _regroup_w_in(jnp.swapaxes(w_in, 1, 2))
    hp = jnp.pad(jnp.stack([dt_bias, a_log], axis=1),
                 ((0, 0), (0, SUBLANES - 2), (0, LANES - HEADS)))
    cw = _group_conv_params(ssm_conv_w)
    cbias = _group_conv_params(ssm_conv_b[:, None, :])
    scw = conv_w.reshape(DEPTH, CONV_WIDTH, 2, PAIRS, SHORT_PER_GROUP)
    scw = jnp.transpose(scw, (0, 3, 2, 1, 4)).reshape(DEPTH, GROUPS, CONV_WIDTH, SHORT_PER_GROUP)
    dskip_e = jnp.repeat(d_skip, HEAD_DIM, axis=-1).reshape(DEPTH, GROUPS, 1, GROUP_X)
    nw = ssm_norm_w.reshape(DEPTH, GROUPS, 1, GROUP_X)
    ws = w_ssm_out.astype(bf)
    wc = w_conv_out.astype(bf)
    wo = w_o.astype(bf)
    wup = w_up.astype(bf)
    wdn = w_down.astype(bf)

    e_t, perm, cum, to_perm, from_perm = _head_constants()
    fnorm = final_norm.reshape(1, D_MODEL)
    x2 = x.reshape(bsz * seq, D_MODEL)

    for l in range(DEPTH):
        mod_l = mod_all[l]
        pre_args = (mod_l, ln1[l].reshape(1, D_MODEL), w_dt[l], hp[l], cum, perm, seq)
        if l == 0:
            x2, u, parts = _pre_call(x2, *pre_args, to_perm=to_perm)
        else:
            u, parts = _pre_call(x2, *pre_args)
        mixed = _mixer_call(u, w_pairs, l, parts, cw[l], cbias[l], scw[l], dskip_e[l], nw[l], e_t,
                            bsz, seq)
        x2 = _out_call(mixed, x2, mod_l, wc, ws, wo, l, seq)
        x2 = _mlp_call(x2, mod_l, ln2[l].reshape(1, D_MODEL), wup, wdn, l, fnorm, from_perm, seq,
                       final=(l == DEPTH - 1))
    return x2.reshape(bsz, seq, D_MODEL)
```

```python
import functools

import numpy as np
import jax
import jax.numpy as jnp
from jax import lax
from jax.experimental import pallas as pl
from jax.experimental.pallas import tpu as pltpu

D_MODEL = 1024
DEPTH = 4
EPS = 1e-6
N_MOD = 6
MOD_ROWS = 8
CONV_DIM = D_MODEL
CONV_WIDTH = 3
D_SSM = 2 * D_MODEL
HEAD_DIM = 64
HEADS = D_SSM // HEAD_DIM
GROUPS = 8
PAIRS = GROUPS // 2
HEADS_PER_GROUP = HEADS // GROUPS
GROUP_X = HEADS_PER_GROUP * HEAD_DIM
STATE = 128
SSM_CONV_WIDTH = 4
D_FF = 4 * D_MODEL
LANES = 128
SUBLANES = 8
F32_TINY = float(np.finfo(np.float32).tiny)
LOG2E = float(np.log2(np.e))

OFF_CB = 2 * D_MODEL
OFF_CC = OFF_CB + CONV_DIM
OFF_CX = OFF_CC + CONV_DIM
OFF_Z = OFF_CX + CONV_DIM
OFF_XS = OFF_Z + D_SSM
OFF_BS = OFF_XS + D_SSM
OFF_CS = OFF_BS + GROUPS * STATE
OFF_DT = OFF_CS + GROUPS * STATE

SCAN_CHUNK = 128
SLABS = SCAN_CHUNK // SUBLANES
GROUP_CONV = GROUP_X + 2 * STATE
SHORT_PER_GROUP = CONV_DIM // GROUPS
GATE_PER_GROUP = 2 * D_MODEL // GROUPS
G_XBC = 0
G_Z = G_XBC + GROUP_CONV
G_CB = G_Z + GROUP_X
G_CC = G_CB + SHORT_PER_GROUP
G_CX = G_CC + SHORT_PER_GROUP
G_GATE = G_CX + SHORT_PER_GROUP
GROUP_COLS = G_GATE + GATE_PER_GROUP
PROJ_PIECE = 256
PIECES_WHILE_ODD = {0: (0,), 1: (1,), 2: (2,), 4: (3,), 5: (4,), 7: (5,)}
PIECES_WHILE_EVEN = {0: (6,), 1: (7,), 3: (8,), 4: (9,), 6: (10,)}

N_SPLIT = 3
PART_ROWS = 32
BETA_ROW0 = N_SPLIT * HEADS_PER_GROUP

VMEM_LIMIT = 48 * 1024 * 1024
VMEM_LIMIT_LARGE = 56 * 1024 * 1024


def _cparams(n_axes, limit=VMEM_LIMIT):
    return pltpu.CompilerParams(
        dimension_semantics=("arbitrary",) * n_axes,
        vmem_limit_bytes=limit)


def _dot(a, b):
    return jnp.dot(a, b, preferred_element_type=jnp.float32)


def _dot_t(a, b_t):
    return lax.dot_general(a, b_t, (((1,), (1,)), ((), ())), preferred_element_type=jnp.float32)


def _split3(v):
    hi = v.astype(jnp.bfloat16)
    r1 = v - hi.astype(jnp.float32)
    mid = r1.astype(jnp.bfloat16)
    lo = (r1 - mid.astype(jnp.float32)).astype(jnp.bfloat16)
    return hi, mid, lo


def _silu(v):
    return v / (1.0 + jnp.exp2(v * (-LOG2E)))


def _chunk_time(p):
    return (p % SUBLANES) * SLABS + p // SUBLANES


def _ada_kernel(c_ref, w_ref, b_ref, o_ref):
    c = c_ref[...]
    o_ref[0] = _dot(_silu(c), w_ref[0]) + b_ref[0]


def _ada_call(c_pad, w_ada, b_ada):
    n_out = N_MOD * D_MODEL
    tn = D_MODEL
    return pl.pallas_call(
        _ada_kernel,
        grid=(DEPTH, n_out // tn),
        in_specs=[
            pl.BlockSpec((SUBLANES, D_MODEL), lambda l, j: (0, 0)),
            pl.BlockSpec((1, D_MODEL, tn), lambda l, j: (l, 0, j)),
            pl.BlockSpec((1, 1, tn), lambda l, j: (l, 0, j)),
        ],
        out_specs=pl.BlockSpec((1, SUBLANES, tn), lambda l, j: (l, 0, j)),
        out_shape=jax.ShapeDtypeStruct((DEPTH, SUBLANES, n_out), jnp.float32),
        compiler_params=_cparams(2),
        name="ada_mod",
    )(c_pad, w_ada, b_ada.reshape(DEPTH, 1, n_out))


def _modnorm_to(x_ref, ln_ref, scale, shift, u_ref, rows, chunk=256):
    w = ln_ref[...] * (1.0 + scale)

    def body(r, carry):
        r0 = pl.multiple_of(r * chunk, chunk)
        xf = x_ref[pl.ds(r0, chunk), :]
        ms = jnp.mean(xf * xf, axis=-1, keepdims=True)
        y = xf * lax.rsqrt(ms + EPS)
        u_ref[pl.ds(r0, chunk), :] = (y * w + shift).astype(u_ref.dtype)
        return carry

    lax.fori_loop(0, rows // chunk, body, 0)


def _permute_rows(src_ref, dst_ref, onehot_b, rows):
    def body(c, carry):
        r0 = pl.multiple_of(c * SCAN_CHUNK, SCAN_CHUNK)
        hi, mid, lo = _split3(src_ref[pl.ds(r0, SCAN_CHUNK), :])
        dst_ref[pl.ds(r0, SCAN_CHUNK), :] = (_dot(onehot_b, hi) + _dot(onehot_b, mid)
                                             + _dot(onehot_b, lo)).astype(dst_ref.dtype)
        return carry

    lax.fori_loop(0, rows // SCAN_CHUNK, body, 0)


def _pre_kernel(*refs, tp, permute_in):
    if permute_in:
        (xnat_ref, mod_ref, ln_ref, wdt_ref, hp_ref, cum_ref, perm_ref, toperm_ref,
         x_ref, u_ref, parts_ref) = refs
        _permute_rows(xnat_ref, x_ref, toperm_ref[...], tp)
    else:
        x_ref, mod_ref, ln_ref, wdt_ref, hp_ref, cum_ref, perm_ref, u_ref, parts_ref = refs
    shift = mod_ref[0, 0:1, :]
    scale = mod_ref[0, 1:2, :]
    _modnorm_to(x_ref, ln_ref, scale, shift, u_ref, tp)
    dt_raw_t = _dot_t(wdt_ref[...], u_ref[...])[0:HEADS, :]
    dt_bias = hp_ref[0]
    a_neg = -jnp.exp(hp_ref[1])
    cum_t = cum_ref[...]
    perm = perm_ref[...]
    q = SCAN_CHUNK
    for c in range(tp // q):
        r0 = c * q
        dtr = dt_raw_t[:, r0:r0 + q] + dt_bias
        dt = jnp.maximum(dtr, 0.0) + jnp.log1p(jnp.exp(-jnp.abs(dtr)))
        a_parts = _split3(dt * a_neg)
        acum = _dot(a_parts[0], cum_t) + _dot(a_parts[1], cum_t) + _dot(a_parts[2], cum_t)
        beta = acum - jnp.log(jnp.maximum(dt, F32_TINY))
        terms = jnp.concatenate(_split3(acum * LOG2E) + _split3(beta * LOG2E), axis=0)
        parts_ref[:, r0:r0 + q] = _dot(perm, terms).astype(parts_ref.dtype)


def _pre_call(x2, mod_l, ln, w_dt, hp, cum, perm, seq, to_perm=None, tp=1024):
    t_total = x2.shape[0]
    tiles_per_batch = seq // tp
    permute_in = to_perm is not None
    square = pl.BlockSpec((SCAN_CHUNK, SCAN_CHUNK), lambda i: (0, 0))
    rows = pl.BlockSpec((tp, D_MODEL), lambda i: (i, 0))
    return pl.pallas_call(
        functools.partial(_pre_kernel, tp=tp, permute_in=permute_in),
        grid=(t_total // tp,),
        in_specs=[
            rows,
            pl.BlockSpec((1, MOD_ROWS, D_MODEL), lambda i: (i // tiles_per_batch, 0, 0)),
            pl.BlockSpec((1, D_MODEL), lambda i: (0, 0)),
            pl.BlockSpec((LANES, D_MODEL), lambda i: (0, 0)),
            pl.BlockSpec((2, HEADS, LANES), lambda i: (0, 0, 0)),
            square,
            pl.BlockSpec((GROUPS * PART_ROWS, 2 * N_SPLIT * HEADS), lambda i: (0, 0)),
        ] + ([square] if permute_in else []),
        out_specs=([rows] if permute_in else []) + [
            rows,
            pl.BlockSpec((GROUPS * PART_ROWS, tp), lambda i: (0, i)),
        ],
        out_shape=([jax.ShapeDtypeStruct((t_total, D_MODEL), jnp.float32)] if permute_in else []) + [
            jax.ShapeDtypeStruct((t_total, D_MODEL), jnp.bfloat16),
            jax.ShapeDtypeStruct((GROUPS * PART_ROWS, t_total), jnp.bfloat16),
        ],
        compiler_params=_cparams(1),
        name="pre",
    )(*((x2, mod_l, ln, w_dt, hp, cum, perm) + ((to_perm,) if permute_in else ())))


def _shifted_taps(cur, tail, width):
    n_wrap = width - 1
    cur_tail = cur[SLABS - n_wrap:]
    wrapped = jnp.concatenate([tail[:, SUBLANES - 1:, :], cur_tail[:, :SUBLANES - 1, :]], axis=1)
    taps = [cur]
    for s in range(1, width):
        taps.append(jnp.concatenate([wrapped[n_wrap - s:], cur[:SLABS - s]], axis=0))
    return taps, cur_tail


def _project_piece(u_ref, w_ref, pa_ref, pb_ref, piece, tb):
    lo = piece * PROJ_PIECE
    res = _dot(u_ref[...], w_ref[0, :, lo:lo + PROJ_PIECE]
               ).reshape(tb // SUBLANES, SUBLANES, PROJ_PIECE)
    n_a = min(max(GROUP_COLS - lo, 0), PROJ_PIECE)
    if n_a:
        pa_ref[:, :, lo:lo + n_a] = res[:, :, 0:n_a]
    if n_a < PROJ_PIECE:
        b0 = lo + n_a - GROUP_COLS
        pb_ref[:, :, b0:b0 + PROJ_PIECE - n_a] = res[:, :, n_a:]


def _mix_group(p_ref, parts_ref, g, first, cw_ref, cbias_ref, scw_ref, dskip_ref, nw_ref, e_t,
               causal, ys_ref, yc_ref, gate_ref, state_ref, tail_ref, stail_ref, tb, after_chunk,
               hist_src, keep_start):
    q = SCAN_CHUNK
    nt_dims = (((1,), (1,)), ((), ()))
    tn_dims = (((0,), (0,)), ((), ()))
    cw = cw_ref[g]
    cbias = cbias_ref[g]
    scw = scw_ref[g]
    dskip = dskip_ref[g]
    nw = nw_ref[g]

    gate_ref[...] = p_ref[:, :, G_GATE:GROUP_COLS].reshape(tb, GATE_PER_GROUP).astype(gate_ref.dtype)

    state_in = jnp.where(first, 0.0, state_ref[hist_src])
    tail = jnp.where(first, 0.0, tail_ref[hist_src])
    stail = jnp.where(first, 0.0, stail_ref[hist_src])
    if keep_start:
        state_ref[GROUPS] = state_in
        tail_ref[GROUPS] = tail
        stail_ref[GROUPS] = stail
    state_ref[g] = state_in

    for c in range(tb // q):
        r0 = c * q
        b0 = c * SLABS
        sv = p_ref[b0:b0 + SLABS, :, G_CC:G_CX] * p_ref[b0:b0 + SLABS, :, G_CX:G_GATE]
        staps, stail = _shifted_taps(sv, stail, CONV_WIDTH)
        sconv = scw[CONV_WIDTH - 1:CONV_WIDTH, :] * staps[0]
        for s in range(1, CONV_WIDTH):
            sconv += scw[CONV_WIDTH - 1 - s:CONV_WIDTH - s, :] * staps[s]
        yc = p_ref[b0:b0 + SLABS, :, G_CB:G_CC] * sconv
        yc_ref[r0:r0 + q, :] = yc.reshape(q, SHORT_PER_GROUP).astype(yc_ref.dtype)

        taps, tail = _shifted_taps(p_ref[b0:b0 + SLABS, :, G_XBC:G_Z], tail, SSM_CONV_WIDTH)
        acc = cbias + cw[SSM_CONV_WIDTH - 1:SSM_CONV_WIDTH, :] * taps[0]
        for s in range(1, SSM_CONV_WIDTH):
            acc += cw[SSM_CONV_WIDTH - 1 - s:SSM_CONV_WIDTH - s, :] * taps[s]
        xbc = _silu(acc).reshape(q, GROUP_CONV)
        xs = xbc[:, 0:GROUP_X]
        xt_b = xs.astype(jnp.bfloat16).T
        b_b = xbc[:, GROUP_X:GROUP_X + STATE].astype(jnp.bfloat16)
        c_b = xbc[:, GROUP_X + STATE:GROUP_CONV].astype(jnp.bfloat16)

        mc = parts_ref[:, r0:r0 + q]
        acum_l = lax.dot_general(mc, e_t, tn_dims, preferred_element_type=jnp.float32)
        acum_last = acum_l[q - 1:q, :]
        mf = mc.astype(jnp.float32)

        scores_b = lax.dot_general(c_b, b_b, nt_dims, preferred_element_type=jnp.float32
                                   ).astype(jnp.bfloat16)
        ys = []
        for r in range(HEADS_PER_GROUP):
            al = acum_l[:, r * LANES:(r + 1) * LANES]
            last = acum_last[:, r * LANES:(r + 1) * LANES]
            beta_row = (mf[BETA_ROW0 + r:BETA_ROW0 + r + 1, :]
                        + mf[BETA_ROW0 + HEADS_PER_GROUP + r:BETA_ROW0 + HEADS_PER_GROUP + r + 1, :]
                        + mf[BETA_ROW0 + 2 * HEADS_PER_GROUP + r:
                             BETA_ROW0 + 2 * HEADS_PER_GROUP + r + 1, :])
            seg = (al - beta_row).astype(jnp.bfloat16)
            w_r = scores_b * jnp.exp2(jnp.where(causal, seg, -jnp.inf))
            c_r = c_b * jnp.exp2(al).astype(jnp.bfloat16)
            xt_r = xt_b[r * HEAD_DIM:(r + 1) * HEAD_DIM, :]
            xw_r = xt_r * jnp.exp2(last - beta_row).astype(jnp.bfloat16)
            s_r = state_ref[g, r]
            ys.append(lax.dot_general(
                jnp.concatenate([w_r, c_r], axis=1),
                jnp.concatenate([xt_r, s_r.astype(jnp.bfloat16)], axis=1),
                nt_dims, preferred_element_type=jnp.float32))
            state_ref[g, r] = s_r * jnp.exp2(last) + _dot(xw_r, b_b)
        y = jnp.concatenate(ys, axis=-1)

        y = y + dskip * xs
        y = y * _silu(p_ref[b0:b0 + SLABS, :, G_Z:G_CB].reshape(q, GROUP_X))
        y = y * lax.rsqrt(jnp.mean(y * y, axis=-1, keepdims=True) + EPS)
        ys_ref[r0:r0 + q, :] = (y * nw).astype(ys_ref.dtype)
        after_chunk(c)

    tail_ref[g] = tail
    stail_ref[g] = stail


def _mixer_kernel(u_ref, w_ref, parts_a_ref, parts_b_ref,
                  cw_ref, cbias_ref, scw_ref, dskip_ref, nw_ref, et_ref,
                  ys_a_ref, yc_a_ref, gate_a_ref, ys_b_ref, yc_b_ref, gate_b_ref,
                  pa_ref, pb_ref, state_ref, tail_ref, stail_ref, *, tb, n_items, nt):
    s = pl.program_id(0)
    q = SCAN_CHUNK

    @pl.when(s == 0)
    def _():
        pb_ref[...] = jnp.zeros_like(pb_ref)
        state_ref[...] = jnp.zeros_like(state_ref)
        tail_ref[...] = jnp.zeros_like(tail_ref)
        stail_ref[...] = jnp.zeros_like(stail_ref)

    item_a = jnp.minimum(s, n_items - 1)
    item_b = jnp.maximum(s - 1, 0)
    g_a = 2 * (item_a % PAIRS)
    g_b = 2 * (item_b % PAIRS) + 1
    first_a = (item_a // PAIRS) % nt == 0
    first_b = (item_b // PAIRS) % nt == 0

    e_t = et_ref[...]
    t_row = _chunk_time(lax.broadcasted_iota(jnp.int32, (q, q), 0))
    t_col = _chunk_time(lax.broadcasted_iota(jnp.int32, (q, q), 1))
    causal = t_row >= t_col
    common = (cw_ref, cbias_ref, scw_ref, dskip_ref, nw_ref, e_t, causal)
    hist = (state_ref, tail_ref, stail_ref, tb)

    def project_after(schedule):
        def after_chunk(c):
            for piece in schedule.get(c, ()):
                _project_piece(u_ref, w_ref, pa_ref, pb_ref, piece, tb)
        return after_chunk

    _mix_group(pb_ref, parts_b_ref, g_b, first_b, *common, ys_b_ref, yc_b_ref, gate_b_ref, *hist,
               project_after(PIECES_WHILE_ODD), g_b, False)
    drain = s == n_items
    _mix_group(pa_ref, parts_a_ref, g_a, first_a, *common, ys_a_ref, yc_a_ref, gate_a_ref, *hist,
               project_after(PIECES_WHILE_EVEN), jnp.where(drain, GROUPS, g_a), True)


def _mixer_call(u, w_pairs, layer, parts, cw, cbias, scw, dskip_e, nw, e_t, bsz, seq, tb=1024):
    t_total = u.shape[0]
    nt = seq // tb
    n_blocks = t_total // tb
    n_items = n_blocks * PAIRS
    item_a = lambda s: jnp.minimum(s, n_items - 1)
    item_b = lambda s: jnp.maximum(s - 1, 0)
    out_a = lambda s: (item_a(s) // PAIRS, item_a(s) % PAIRS)
    out_b = lambda s: (item_b(s) // PAIRS, item_b(s) % PAIRS)

    def full(arr):
        nd = arr.ndim
        return pl.BlockSpec(arr.shape, lambda s: (0,) * nd)

    half = GROUPS // 2
    out_shapes = [
        jax.ShapeDtypeStruct((t_total, half *GROUP_X), jnp.bfloat16),
        jax.ShapeDtypeStruct((t_total, half *SHORT_PER_GROUP), jnp.bfloat16),
        jax.ShapeDtypeStruct((t_total, half *GATE_PER_GROUP), jnp.bfloat16),
    ]
    widths = (GROUP_X, SHORT_PER_GROUP, GATE_PER_GROUP)
    return pl.pallas_call(
        functools.partial(_mixer_kernel, tb=tb, n_items=n_items, nt=nt),
        grid=(n_items + 1,),
        in_specs=[
            pl.BlockSpec((tb, D_MODEL), lambda s: (item_a(s) // PAIRS, 0)),
            pl.BlockSpec((1, D_MODEL, 2 * GROUP_COLS), lambda s: (layer, 0, item_a(s) % PAIRS)),
            pl.BlockSpec((PART_ROWS, tb), lambda s: (2 * (item_a(s) % PAIRS), item_a(s) // PAIRS)),
            pl.BlockSpec((PART_ROWS, tb),
                         lambda s: (2 * (item_b(s) % PAIRS) + 1, item_b(s) // PAIRS)),
            full(cw), full(cbias), full(scw), full(dskip_e), full(nw), full(e_t),
        ],
        out_specs=([pl.BlockSpec((tb, w), out_a) for w in widths]
                   + [pl.BlockSpec((tb, w), out_b) for w in widths]),
        out_shape=out_shapes + out_shapes,
        scratch_shapes=[
            pltpu.VMEM((tb // SUBLANES, SUBLANES, GROUP_COLS), jnp.float32),
            pltpu.VMEM((tb // SUBLANES, SUBLANES, GROUP_COLS), jnp.float32),
            pltpu.VMEM((GROUPS + 1, HEADS_PER_GROUP, HEAD_DIM, STATE), jnp.float32),
            pltpu.VMEM((GROUPS + 1, SSM_CONV_WIDTH - 1, SUBLANES, GROUP_CONV), jnp.float32),
            pltpu.VMEM((GROUPS + 1, CONV_WIDTH - 1, SUBLANES, SHORT_PER_GROUP), jnp.float32),
        ],
        compiler_params=_cparams(1, VMEM_LIMIT_LARGE),
        name="mixer",
    )(u, w_pairs, parts, parts, cw, cbias, scw, dskip_e, nw, e_t)


def _out_kernel(yca_ref, ycb_ref, ysa_ref, ysb_ref, ga_ref, gb_ref, x_ref, mod_ref,
                wc_ref, ws_ref, wo_ref, o_ref):
    p_conv = _dot(jnp.concatenate([yca_ref[...], ycb_ref[...]], axis=1), wc_ref[0])
    y_ssm = jnp.concatenate(
        [ref[:, i * GROUP_X:(i + 1) * GROUP_X] for i in range(PAIRS) for ref in (ysa_ref, ysb_ref)],
        axis=1)
    p_ssm = _dot(y_ssm, ws_ref[0])
    g_conv = 1.0 / (1.0 + jnp.exp(-ga_ref[...].astype(jnp.float32)))
    g_ssm = 1.0 / (1.0 + jnp.exp(-gb_ref[...].astype(jnp.float32)))
    merged = g_conv * p_conv + g_ssm * p_ssm
    mix = _dot(merged.astype(jnp.bfloat16), wo_ref[0])
    gate1 = mod_ref[0, 2:3, :]
    o_ref[...] = x_ref[...] + gate1 * mix


def _out_call(mixed, x2, mod_l, wc, ws, wo, layer, seq, tm=512):
    ys_a, yc_a, g_a, ys_b, yc_b, g_b = mixed
    t_total = x2.shape[0]
    tiles_per_batch = seq // tm
    row = lambda width: pl.BlockSpec((tm, width), lambda i: (i, 0))
    const = lambda shape: pl.BlockSpec((1,) + shape, lambda i: (layer, 0, 0))
    return pl.pallas_call(
        _out_kernel,
        grid=(t_total // tm,),
        in_specs=[
            row(CONV_DIM // 2), row(CONV_DIM // 2), row(D_SSM // 2), row(D_SSM // 2),
            row(D_MODEL), row(D_MODEL), row(D_MODEL),
            pl.BlockSpec((1, MOD_ROWS, D_MODEL), lambda i: (i // tiles_per_batch, 0, 0)),
            const((CONV_DIM, D_MODEL)), const((D_SSM, D_MODEL)), const((D_MODEL, D_MODEL)),
        ],
        out_specs=pl.BlockSpec((tm, D_MODEL), lambda i: (i, 0)),
        out_shape=jax.ShapeDtypeStruct((t_total, D_MODEL), jnp.float32),
        compiler_params=_cparams(1),
        name="mix_out",
    )(yc_a, yc_b, ys_a, ys_b, g_a, g_b, x2, mod_l, wc, ws, wo)


def _mlp_kernel(x_ref, mod_ref, ln_ref, wup_ref, wdn_ref, fn_ref, fromperm_ref, o_ref, u_ref,
                acc_ref, *, tm, final):
    j = pl.program_id(1)

    @pl.when(j == 0)
    def _():
        shift = mod_ref[0, 3:4, :]
        scale = mod_ref[0, 4:5, :]
        _modnorm_to(x_ref, ln_ref, scale, shift, u_ref, tm)
        acc_ref[...] = jnp.zeros_like(acc_ref)

    h = jnp.maximum(_dot(u_ref[...], wup_ref[0]), 0.0)
    acc_ref[...] += _dot((h * h).astype(jnp.bfloat16), wdn_ref[0])

    @pl.when(j == pl.num_programs(1) - 1)
    def _():
        gate2 = mod_ref[0, 5:6, :]
        xn = x_ref[...] + gate2 * acc_ref[...]
        if final:
            ms = jnp.mean(xn * xn, axis=-1, keepdims=True)
            acc_ref[...] = xn * lax.rsqrt(ms + EPS) * fn_ref[...]
            _permute_rows(acc_ref, o_ref, fromperm_ref[...], tm)
        else:
            o_ref[...] = xn


def _mlp_call(x2, mod_l, ln, w_up, w_dn, layer, fnorm, from_perm, seq, final, tm=1024, tf=2048):
    t_total = x2.shape[0]
    tiles_per_batch = seq // tm
    return pl.pallas_call(
        functools.partial(_mlp_kernel, tm=tm, final=final),
        grid=(t_total // tm, D_FF // tf),
        in_specs=[
            pl.BlockSpec((tm, D_MODEL), lambda i, j: (i, 0)),
            pl.BlockSpec((1, MOD_ROWS, D_MODEL), lambda i, j: (i // tiles_per_batch, 0, 0)),
            pl.BlockSpec((1, D_MODEL), lambda i, j: (0, 0)),
            pl.BlockSpec((1, D_MODEL, tf), lambda i, j: (layer, 0, j)),
            pl.BlockSpec((1, tf, D_MODEL), lambda i, j: (layer, j, 0)),
            pl.BlockSpec((1, D_MODEL), lambda i, j: (0, 0)),
            pl.BlockSpec((SCAN_CHUNK, SCAN_CHUNK), lambda i, j: (0, 0)),
        ],
        out_specs=pl.BlockSpec((tm, D_MODEL), lambda i, j: (i, 0)),
        out_shape=jax.ShapeDtypeStruct((t_total, D_MODEL), jnp.float32),
        scratch_shapes=[pltpu.VMEM((tm, D_MODEL), jnp.bfloat16),
                        pltpu.VMEM((tm, D_MODEL), jnp.float32)],
        compiler_params=_cparams(2, VMEM_LIMIT_LARGE),
        name="mlp",
    )(x2, mod_l, ln, w_up, w_dn, fnorm, from_perm)


def _head_constants():
    e_t = np.zeros((PART_ROWS, HEADS_PER_GROUP, LANES), np.float32)
    perm = np.zeros((GROUPS, PART_ROWS, 2 * N_SPLIT, HEADS), np.float32)
    for k in range(2 * N_SPLIT):
        for r in range(HEADS_PER_GROUP):
            if k < N_SPLIT:
                e_t[k * HEADS_PER_GROUP + r, r, :] = 1.0
            for g in range(GROUPS):
                perm[g, k * HEADS_PER_GROUP + r, k, g * HEADS_PER_GROUP + r] = 1.0
    e_t = e_t.reshape(PART_ROWS, HEADS_PER_GROUP * LANES)
    perm = perm.reshape(GROUPS * PART_ROWS, 2 * N_SPLIT * HEADS)
    tm = _chunk_time(np.arange(SCAN_CHUNK))
    cum = (tm[:, None] <= tm[None, :]).astype(np.float32)
    to_perm = (np.arange(SCAN_CHUNK)[None, :] == tm[:, None]).astype(np.float32)
    bf = jnp.bfloat16
    return (jnp.asarray(e_t, bf), jnp.asarray(perm, bf), jnp.asarray(cum, bf),
            jnp.asarray(to_perm, bf), jnp.asarray(to_perm.T, bf))


def _regroup_kernel(xs_ref, bs_ref, cs_ref, z_ref, cb_ref, cc_ref, cx_ref, gate_ref, dt_ref,
                    o_ref, odt_ref):
    for ref, lo, hi in ((xs_ref, G_XBC, G_XBC + GROUP_X), (bs_ref, G_XBC + GROUP_X, G_Z - STATE),
                        (cs_ref, G_Z - STATE, G_Z), (z_ref, G_Z, G_CB), (cb_ref, G_CB, G_CC),
                        (cc_ref, G_CC, G_CX), (cx_ref, G_CX, G_GATE), (gate_ref, G_GATE, GROUP_COLS)):
        o_ref[0, :, lo:hi] = ref[0].T.astype(o_ref.dtype)

    @pl.when(pl.program_id(1) == 0)
    def _():
        row = lax.broadcasted_iota(jnp.int32, (LANES, D_MODEL), 0)
        odt_ref[0] = jnp.where(row < HEADS, dt_ref[0], 0.0).astype(odt_ref.dtype)


def _regroup_w_in(w_t):
    def rows(height, off, paired):
        base = off // height
        if paired:
            return pl.BlockSpec((1, height, D_MODEL),
                                lambda l, g: (l, base + (g % 2) * PAIRS + g // 2, 0))
        return pl.BlockSpec((1, height, D_MODEL), lambda l, g: (l, base + g, 0))

    return pl.pallas_call(
        _regroup_kernel,
        grid=(DEPTH, GROUPS),
        in_specs=[
            rows(GROUP_X, OFF_XS, False), rows(STATE, OFF_BS, False), rows(STATE, OFF_CS, False),
            rows(GROUP_X, OFF_Z, False),
            rows(SHORT_PER_GROUP, OFF_CB, True), rows(SHORT_PER_GROUP, OFF_CC, True),
            rows(SHORT_PER_GROUP, OFF_CX, True), rows(GATE_PER_GROUP, 0, True),
            pl.BlockSpec((1, LANES, D_MODEL), lambda l, g: (l, OFF_DT // LANES, 0)),
        ],
        out_specs=[
            pl.BlockSpec((1, D_MODEL, GROUP_COLS), lambda l, g: (l, 0, g)),
            pl.BlockSpec((1, LANES, D_MODEL), lambda l, g: (l, 0, 0)),
        ],
        out_shape=[
            jax.ShapeDtypeStruct((DEPTH, D_MODEL, GROUPS * GROUP_COLS), jnp.bfloat16),
            jax.ShapeDtypeStruct((DEPTH, LANES, D_MODEL), jnp.bfloat16),
        ],
        compiler_params=_cparams(2),
        name="w_in_regroup",
    )(*([w_t] * 9))


def _group_conv_params(p):
    k = p.shape[1]
    xs = p[..., :D_SSM].reshape(DEPTH, k, GROUPS, GROUP_X)
    bs = p[..., D_SSM:D_SSM + GROUPS * STATE].reshape(DEPTH, k, GROUPS, STATE)
    cs = p[..., D_SSM + GROUPS * STATE:].reshape(DEPTH, k, GROUPS, STATE)
    return jnp.swapaxes(jnp.concatenate([xs, bs, cs], axis=-1), 1, 2)


def kernel(x, c, w_ada, b_ada, ln1, ln2, w_in, conv_w, ssm_conv_w, ssm_conv_b, dt_bias, a_log,
           d_skip, ssm_norm_w, w_conv_out, w_ssm_out, w_o, w_up, w_down, final_norm):
    bsz, seq, _ = x.shape
    bf = jnp.bfloat16
    f32 = jnp.float32

    c_pad = jnp.zeros((SUBLANES, D_MODEL), f32).at[:bsz].set(c)
    mod_all = _ada_call(c_pad, w_ada, b_ada)
    mod_all = mod_all[:, :bsz].reshape(DEPTH, bsz, N_MOD, D_MODEL)
    mod_all = jnp.pad(mod_all, ((0, 0), (0, 0), (0, MOD_ROWS - N_MOD), (0, 0)))

    w_pairs, w_dt = _regroup_w_in(jnp.swapaxes(w_in, 1, 2))
    hp = jnp.broadcast_to(jnp.stack([dt_bias, a_log], axis=1)[..., None],
                          (DEPTH, 2, HEADS, LANES))
    cw = _group_conv_params(ssm_conv_w)
    cbias = _group_conv_params(ssm_conv_b[:, None, :])
    scw = conv_w.reshape(DEPTH, CONV_WIDTH, 2, PAIRS, SHORT_PER_GROUP)
    scw = jnp.transpose(scw, (0, 3, 2, 1, 4)).reshape(DEPTH, GROUPS, CONV_WIDTH, SHORT_PER_GROUP)
    dskip_e = jnp.repeat(d_skip, HEAD_DIM, axis=-1).reshape(DEPTH, GROUPS, 1, GROUP_X)
    nw = ssm_norm_w.reshape(DEPTH, GROUPS, 1, GROUP_X)
    ws = w_ssm_out.astype(bf)
    wc = w_conv_out.astype(bf)
    wo = w_o.astype(bf)
    wup = w_up.astype(bf)
    wdn = w_down.astype(bf)

    e_t, perm, cum, to_perm, from_perm = _head_constants()
    fnorm = final_norm.reshape(1, D_MODEL)
    x2 = x.reshape(bsz * seq, D_MODEL)

    for l in range(DEPTH):
        mod_l = mod_all[l]
        pre_args = (mod_l, ln1[l].reshape(1, D_MODEL), w_dt[l], hp[l], cum, perm, seq)
        if l == 0:
            x2, u, parts = _pre_call(x2, *pre_args, to_perm=to_perm)
        else:
            u, parts = _pre_call(x2, *pre_args)
        mixed = _mixer_call(u, w_pairs, l, parts, cw[l], cbias[l], scw[l], dskip_e[l], nw[l], e_t,
                            bsz, seq)
        x2 = _out_call(mixed, x2, mod_l, wc, ws, wo, l, seq)
        x2 = _mlp_call(x2, mod_l, ln2[l].reshape(1, D_MODEL), wup, wdn, l, fnorm, from_perm, seq,
                       final=(l == DEPTH - 1))
    return x2.reshape(bsz, seq, D_MODEL)
```

```python
import functools

import numpy as np
import jax
import jax.numpy as jnp
from jax import lax
from jax.experimental import pallas as pl
from jax.experimental.pallas import tpu as pltpu

D_MODEL = 1024
DEPTH = 4
EPS = 1e-6
N_MOD = 6
MOD_ROWS = 8
CONV_DIM = D_MODEL
CONV_WIDTH = 3
D_SSM = 2 * D_MODEL
HEAD_DIM = 64
HEADS = D_SSM // HEAD_DIM
GROUPS = 8
PAIRS = GROUPS // 2
HEADS_PER_GROUP = HEADS // GROUPS
GROUP_X = HEADS_PER_GROUP * HEAD_DIM
STATE = 128
SSM_CONV_WIDTH = 4
D_FF = 4 * D_MODEL
LANES = 128
SUBLANES = 8
F32_TINY = float(np.finfo(np.float32).tiny)
LOG2E = float(np.log2(np.e))

OFF_CB = 2 * D_MODEL
OFF_CC = OFF_CB + CONV_DIM
OFF_CX = OFF_CC + CONV_DIM
OFF_Z = OFF_CX + CONV_DIM
OFF_XS = OFF_Z + D_SSM
OFF_BS = OFF_XS + D_SSM
OFF_CS = OFF_BS + GROUPS * STATE
OFF_DT = OFF_CS + GROUPS * STATE

SCAN_CHUNK = 128
SLABS = SCAN_CHUNK // SUBLANES
GROUP_CONV = GROUP_X + 2 * STATE
SHORT_PER_STEP = CONV_DIM // GROUPS
GATE_PER_STEP = 2 * D_MODEL // GROUPS
G_XBC = 0
G_Z = G_XBC + GROUP_CONV
GROUP_COLS = G_Z + GROUP_X
PROJ_PIECE = 256
PIECES_WHILE_ODD = {c: ((p * PROJ_PIECE, (p + 1) * PROJ_PIECE),) for c, p in ((0, 0), (2, 1), (4, 2))}
PIECES_WHILE_EVEN = {c: ((p * PROJ_PIECE, (p + 1) * PROJ_PIECE),) for c, p in ((0, 3), (2, 4), (4, 5))}

N_SPLIT = 3
PART_ROWS = 32
BETA_ROW0 = N_SPLIT * HEADS_PER_GROUP

VMEM_LIMIT = 48 * 1024 * 1024
VMEM_LIMIT_LARGE = 56 * 1024 * 1024


def _cparams(n_axes, limit=VMEM_LIMIT):
    return pltpu.CompilerParams(
        dimension_semantics=("arbitrary",) * n_axes,
        vmem_limit_bytes=limit)


def _dot(a, b):
    return jnp.dot(a, b, preferred_element_type=jnp.float32)


def _dot_t(a, b_t):
    return lax.dot_general(a, b_t, (((1,), (1,)), ((), ())), preferred_element_type=jnp.float32)


def _split3(v):
    hi = v.astype(jnp.bfloat16)
    r1 = v - hi.astype(jnp.float32)
    mid = r1.astype(jnp.bfloat16)
    lo = (r1 - mid.astype(jnp.float32)).astype(jnp.bfloat16)
    return hi, mid, lo


def _silu(v):
    return v / (1.0 + jnp.exp2(v * (-LOG2E)))


def _chunk_time(p):
    return (p % SUBLANES) * SLABS + p // SUBLANES


def _ada_kernel(c_ref, w_ref, b_ref, o_ref):
    c = c_ref[...]
    o_ref[0] = _dot(_silu(c), w_ref[0]) + b_ref[0]


def _ada_call(c_pad, w_ada, b_ada):
    n_out = N_MOD * D_MODEL
    tn = D_MODEL
    return pl.pallas_call(
        _ada_kernel,
        grid=(DEPTH, n_out // tn),
        in_specs=[
            pl.BlockSpec((SUBLANES, D_MODEL), lambda l, j: (0, 0)),
            pl.BlockSpec((1, D_MODEL, tn), lambda l, j: (l, 0, j)),
            pl.BlockSpec((1, 1, tn), lambda l, j: (l, 0, j)),
        ],
        out_specs=pl.BlockSpec((1, SUBLANES, tn), lambda l, j: (l, 0, j)),
        out_shape=jax.ShapeDtypeStruct((DEPTH, SUBLANES, n_out), jnp.float32),
        compiler_params=_cparams(2),
        name="ada_mod",
    )(c_pad, w_ada, b_ada.reshape(DEPTH, 1, n_out))


def _modnorm_to(x_ref, ln_ref, scale, shift, u_ref, rows, chunk=256):
    w = ln_ref[...] * (1.0 + scale)

    def body(r, carry):
        r0 = pl.multiple_of(r * chunk, chunk)
        xf = x_ref[pl.ds(r0, chunk), :]
        ms = jnp.mean(xf * xf, axis=-1, keepdims=True)
        y = xf * lax.rsqrt(ms + EPS)
        u_ref[pl.ds(r0, chunk), :] = (y * w + shift).astype(u_ref.dtype)
        return carry

    lax.fori_loop(0, rows // chunk, body, 0)


def _permute_rows(src_ref, dst_ref, onehot_b, rows):
    def body(c, carry):
        r0 = pl.multiple_of(c * SCAN_CHUNK, SCAN_CHUNK)
        hi, mid, lo = _split3(src_ref[pl.ds(r0, SCAN_CHUNK), :])
        dst_ref[pl.ds(r0, SCAN_CHUNK), :] = (_dot(onehot_b, hi) + _dot(onehot_b, mid)
                                             + _dot(onehot_b, lo)).astype(dst_ref.dtype)
        return carry

    lax.fori_loop(0, rows // SCAN_CHUNK, body, 0)


def _pre_kernel(*refs, tp, permute_in):
    if permute_in:
        (xnat_ref, mod_ref, ln_ref, wdt_ref, hp_ref, cum_ref, perm_ref, toperm_ref,
         x_ref, u_ref, parts_ref) = refs
        _permute_rows(xnat_ref, x_ref, toperm_ref[...], tp)
    else:
        x_ref, mod_ref, ln_ref, wdt_ref, hp_ref, cum_ref, perm_ref, u_ref, parts_ref = refs
    shift = mod_ref[0, 0:1, :]
    scale = mod_ref[0, 1:2, :]
    _modnorm_to(x_ref, ln_ref, scale, shift, u_ref, tp)
    dt_raw_t = _dot_t(wdt_ref[...], u_ref[...])[0:HEADS, :]
    dt_bias = hp_ref[0]
    a_neg = -jnp.exp(hp_ref[1])
    cum_t = cum_ref[...]
    perm = perm_ref[...]
    q = SCAN_CHUNK
    for c in range(tp // q):
        r0 = c * q
        dtr = dt_raw_t[:, r0:r0 + q] + dt_bias
        dt = jnp.maximum(dtr, 0.0) + jnp.log1p(jnp.exp(-jnp.abs(dtr)))
        a_parts = _split3(dt * a_neg)
        acum = _dot(a_parts[0], cum_t) + _dot(a_parts[1], cum_t) + _dot(a_parts[2], cum_t)
        beta = acum - jnp.log(jnp.maximum(dt, F32_TINY))
        terms = jnp.concatenate(_split3(acum * LOG2E) + _split3(beta * LOG2E), axis=0)
        parts_ref[:, r0:r0 + q] = _dot(perm, terms).astype(parts_ref.dtype)


def _pre_call(x2, mod_l, ln, w_dt, hp, cum, perm, seq, to_perm=None, tp=1024):
    t_total = x2.shape[0]
    tiles_per_batch = seq // tp
    permute_in = to_perm is not None
    square = pl.BlockSpec((SCAN_CHUNK, SCAN_CHUNK), lambda i: (0, 0))
    rows = pl.BlockSpec((tp, D_MODEL), lambda i: (i, 0))
    return pl.pallas_call(
        functools.partial(_pre_kernel, tp=tp, permute_in=permute_in),
        grid=(t_total // tp,),
        in_specs=[
            rows,
            pl.BlockSpec((1, MOD_ROWS, D_MODEL), lambda i: (i // tiles_per_batch, 0, 0)),
            pl.BlockSpec((1, D_MODEL), lambda i: (0, 0)),
            pl.BlockSpec((LANES, D_MODEL), lambda i: (0, 0)),
            pl.BlockSpec((2, HEADS, LANES), lambda i: (0, 0, 0)),
            square,
            pl.BlockSpec((GROUPS * PART_ROWS, 2 * N_SPLIT * HEADS), lambda i: (0, 0)),
        ] + ([square] if permute_in else []),
        out_specs=([rows] if permute_in else []) + [
            rows,
            pl.BlockSpec((GROUPS * PART_ROWS, tp), lambda i: (0, i)),
        ],
        out_shape=([jax.ShapeDtypeStruct((t_total, D_MODEL), jnp.float32)] if permute_in else []) + [
            jax.ShapeDtypeStruct((t_total, D_MODEL), jnp.bfloat16),
            jax.ShapeDtypeStruct((GROUPS * PART_ROWS, t_total), jnp.bfloat16),
        ],
        compiler_params=_cparams(1),
        name="pre",
    )(*((x2, mod_l, ln, w_dt, hp, cum, perm) + ((to_perm,) if permute_in else ())))


def _shifted_taps(cur, tail, width):
    n_wrap = width - 1
    cur_tail = cur[SLABS - n_wrap:]
    wrapped = jnp.concatenate([tail[:, SUBLANES - 1:, :], cur_tail[:, :SUBLANES - 1, :]], axis=1)
    taps = [cur]
    for s in range(1, width):
        taps.append(jnp.concatenate([wrapped[n_wrap - s:], cur[:SLABS - s]], axis=0))
    return taps, cur_tail


def _project_piece(u_ref, w_ref, pa_ref, pb_ref, piece, tb):
    lo, hi = piece
    res = _dot(u_ref[...], w_ref[0, :, lo:hi]).reshape(tb // SUBLANES, SUBLANES, hi - lo)
    n_a = min(max(GROUP_COLS - lo, 0), hi - lo)
    if n_a:
        pa_ref[:, :, lo:lo + n_a] = res[:, :, 0:n_a]
    if n_a < hi - lo:
        b0 = lo + n_a - GROUP_COLS
        pb_ref[:, :, b0:b0 + hi - lo - n_a] = res[:, :, n_a:]


def _mix_group(p_ref, parts_ref, g, first, cw_ref, cbias_ref, dskip_ref, nw_ref, e_t,
               causal, ys_ref, state_ref, tail_ref, tb, after_chunk, hist_src, keep_start):
    q = SCAN_CHUNK
    nt_dims = (((1,), (1,)), ((), ()))
    tn_dims = (((0,), (0,)), ((), ()))
    cw = cw_ref[g]
    cbias = cbias_ref[g]
    dskip = dskip_ref[g]
    nw = nw_ref[g]

    state_in = jnp.where(first, 0.0, state_ref[hist_src])
    tail = jnp.where(first, 0.0, tail_ref[hist_src])
    if keep_start:
        state_ref[GROUPS] = state_in
        tail_ref[GROUPS] = tail
    state_ref[g] = state_in

    for c in range(tb // q):
        r0 = c * q
        b0 = c * SLABS
        taps, tail = _shifted_taps(p_ref[b0:b0 + SLABS, :, G_XBC:G_Z], tail, SSM_CONV_WIDTH)
        acc = cbias + cw[SSM_CONV_WIDTH - 1:SSM_CONV_WIDTH, :] * taps[0]
        for s in range(1, SSM_CONV_WIDTH):
            acc += cw[SSM_CONV_WIDTH - 1 - s:SSM_CONV_WIDTH - s, :] * taps[s]
        xbc = _silu(acc).reshape(q, GROUP_CONV)
        xs = xbc[:, 0:GROUP_X]
        xt_b = xs.astype(jnp.bfloat16).T
        b_b = xbc[:, GROUP_X:GROUP_X + STATE].astype(jnp.bfloat16)
        c_b = xbc[:, GROUP_X + STATE:GROUP_CONV].astype(jnp.bfloat16)

        mc = parts_ref[:, r0:r0 + q]
        acum_l = lax.dot_general(mc, e_t, tn_dims, preferred_element_type=jnp.float32)
        acum_last = acum_l[q - 1:q, :]
        mf = mc.astype(jnp.float32)

        scores_b = lax.dot_general(c_b, b_b, nt_dims, preferred_element_type=jnp.float32
                                   ).astype(jnp.bfloat16)
        ys = []
        for r in range(HEADS_PER_GROUP):
            al = acum_l[:, r * LANES:(r + 1) * LANES]
            last = acum_last[:, r * LANES:(r + 1) * LANES]
            beta_row = (mf[BETA_ROW0 + r:BETA_ROW0 + r + 1, :]
                        + mf[BETA_ROW0 + HEADS_PER_GROUP + r:BETA_ROW0 + HEADS_PER_GROUP + r + 1, :]
                        + mf[BETA_ROW0 + 2 * HEADS_PER_GROUP + r:
                             BETA_ROW0 + 2 * HEADS_PER_GROUP + r + 1, :])
            seg = (al - beta_row).astype(jnp.bfloat16)
            w_r = scores_b * jnp.exp2(jnp.where(causal, seg, -jnp.inf))
            c_r = c_b * jnp.exp2(al).astype(jnp.bfloat16)
            xt_r = xt_b[r * HEAD_DIM:(r + 1) * HEAD_DIM, :]
            xw_r = xt_r * jnp.exp2(last - beta_row).astype(jnp.bfloat16)
            s_r = state_ref[g, r]
            ys.append(lax.dot_general(
                jnp.concatenate([w_r, c_r], axis=1),
                jnp.concatenate([xt_r, s_r.astype(jnp.bfloat16)], axis=1),
                nt_dims, preferred_element_type=jnp.float32))
            state_ref[g, r] = s_r * jnp.exp2(last) + _dot(xw_r, b_b)
        y = jnp.concatenate(ys, axis=-1)

        y = y + dskip * xs
        y = y * _silu(p_ref[b0:b0 + SLABS, :, G_Z:GROUP_COLS].reshape(q, GROUP_X))
        y = y * lax.rsqrt(jnp.mean(y * y, axis=-1, keepdims=True) + EPS)
        ys_ref[r0:r0 + q, :] = (y * nw).astype(ys_ref.dtype)
        after_chunk(c)

    tail_ref[g] = tail


def _mixer_kernel(u_ref, w_ref, parts_a_ref, parts_b_ref,
                  cw_ref, cbias_ref, dskip_ref, nw_ref, et_ref,
                  ys_a_ref, ys_b_ref,
                  pa_ref, pb_ref, state_ref, tail_ref, *, tb, n_items, nt):
    s = pl.program_id(0)
    q = SCAN_CHUNK

    @pl.when(s == 0)
    def _():
        pb_ref[...] = jnp.zeros_like(pb_ref)
        state_ref[...] = jnp.zeros_like(state_ref)
        tail_ref[...] = jnp.zeros_like(tail_ref)

    item_a = jnp.minimum(s, n_items - 1)
    item_b = jnp.maximum(s - 1, 0)
    g_a = 2 * (item_a % PAIRS)
    g_b = 2 * (item_b % PAIRS) + 1
    first_a = (item_a // PAIRS) % nt == 0
    first_b = (item_b // PAIRS) % nt == 0

    e_t = et_ref[...]
    t_row = _chunk_time(lax.broadcasted_iota(jnp.int32, (q, q), 0))
    t_col = _chunk_time(lax.broadcasted_iota(jnp.int32, (q, q), 1))
    causal = t_row >= t_col
    common = (cw_ref, cbias_ref, dskip_ref, nw_ref, e_t, causal)
    hist = (state_ref, tail_ref, tb)

    def project_after(schedule):
        def after_chunk(c):
            for piece in schedule.get(c, ()):
                _project_piece(u_ref, w_ref, pa_ref, pb_ref, piece, tb)
        return after_chunk

    _mix_group(pb_ref, parts_b_ref, g_b, first_b, *common, ys_b_ref, *hist,
               project_after(PIECES_WHILE_ODD), g_b, False)
    drain = s == n_items
    _mix_group(pa_ref, parts_a_ref, g_a, first_a, *common, ys_a_ref, *hist,
               project_after(PIECES_WHILE_EVEN), jnp.where(drain, GROUPS, g_a), True)


def _mixer_call(u, w_pairs, layer, parts, cw, cbias, dskip_e, nw, e_t, bsz, seq, tb=1024):
    t_total = u.shape[0]
    nt = seq // tb
    n_blocks = t_total // tb
    n_items = n_blocks * PAIRS
    item_a = lambda s: jnp.minimum(s, n_items - 1)
    item_b = lambda s: jnp.maximum(s - 1, 0)
    out_a = lambda s: (item_a(s) // PAIRS, item_a(s) % PAIRS)
    out_b = lambda s: (item_b(s) // PAIRS, item_b(s) % PAIRS)

    def full(arr):
        nd = arr.ndim
        return pl.BlockSpec(arr.shape, lambda s: (0,) * nd)

    out_shape = jax.ShapeDtypeStruct((t_total, PAIRS * GROUP_X), jnp.bfloat16)
    return pl.pallas_call(
        functools.partial(_mixer_kernel, tb=tb, n_items=n_items, nt=nt),
        grid=(n_items + 1,),
        in_specs=[
            pl.BlockSpec((tb, D_MODEL), lambda s: (item_a(s) // PAIRS, 0)),
            pl.BlockSpec((1, D_MODEL, 2 * GROUP_COLS), lambda s: (layer, 0, item_a(s) % PAIRS)),
            pl.BlockSpec((PART_ROWS, tb), lambda s: (2 * (item_a(s) % PAIRS), item_a(s) // PAIRS)),
            pl.BlockSpec((PART_ROWS, tb),
                         lambda s: (2 * (item_b(s) % PAIRS) + 1, item_b(s) // PAIRS)),
            full(cw), full(cbias), full(dskip_e), full(nw), full(e_t),
        ],
        out_specs=[pl.BlockSpec((tb, GROUP_X), out_a), pl.BlockSpec((tb, GROUP_X), out_b)],
        out_shape=[out_shape, out_shape],
        scratch_shapes=[
            pltpu.VMEM((tb // SUBLANES, SUBLANES, GROUP_COLS), jnp.float32),
            pltpu.VMEM((tb // SUBLANES, SUBLANES, GROUP_COLS), jnp.float32),
            pltpu.VMEM((GROUPS + 1, HEADS_PER_GROUP, HEAD_DIM, STATE), jnp.float32),
            pltpu.VMEM((GROUPS + 1, SSM_CONV_WIDTH - 1, SUBLANES, GROUP_CONV), jnp.float32),
        ],
        compiler_params=_cparams(1, VMEM_LIMIT_LARGE),
        name="mixer",
    )(u, w_pairs, parts, parts, cw, cbias, dskip_e, nw, e_t)


def _sconv_kernel(u_ref, wcb_ref, wcc_ref, wcx_ref, scw_ref, yc_ref, stail_ref, *, ts,
                  tiles_per_batch):
    first = pl.program_id(0) % tiles_per_batch == 0
    u = u_ref[...]
    slabs = ts // SUBLANES
    cb = _dot(u, wcb_ref[0]).reshape(slabs, SUBLANES, CONV_DIM)
    sv = (_dot(u, wcc_ref[0]) * _dot(u, wcx_ref[0])).reshape(slabs, SUBLANES, CONV_DIM)
    scw = scw_ref[0]
    stail = jnp.where(first, 0.0, stail_ref[...])
    for c in range(ts // SCAN_CHUNK):
        b0 = c * SLABS
        staps, stail = _shifted_taps(sv[b0:b0 + SLABS], stail, CONV_WIDTH)
        sconv = scw[CONV_WIDTH - 1:CONV_WIDTH, :] * staps[0]
        for s in range(1, CONV_WIDTH):
            sconv += scw[CONV_WIDTH - 1 - s:CONV_WIDTH - s, :] * staps[s]
        yc = cb[b0:b0 + SLABS] * sconv
        yc_ref[c * SCAN_CHUNK:(c + 1) * SCAN_CHUNK, :] = yc.reshape(SCAN_CHUNK, CONV_DIM
                                                                    ).astype(yc_ref.dtype)
    stail_ref[...] = stail


def _sconv_call(u, w_cb, w_cc, w_cx, conv_w, layer, seq, ts=512):
    t_total = u.shape[0]
    weight = pl.BlockSpec((1, D_MODEL, CONV_DIM), lambda i: (layer, 0, 0))
    return pl.pallas_call(
        functools.partial(_sconv_kernel, ts=ts, tiles_per_batch=seq // ts),
        grid=(t_total // ts,),
        in_specs=[
            pl.BlockSpec((ts, D_MODEL), lambda i: (i, 0)),
            weight, weight, weight,
            pl.BlockSpec((1, CONV_WIDTH, CONV_DIM), lambda i: (layer, 0, 0)),
        ],
        out_specs=pl.BlockSpec((ts, CONV_DIM), lambda i: (i, 0)),
        out_shape=jax.ShapeDtypeStruct((t_total, CONV_DIM), jnp.bfloat16),
        scratch_shapes=[pltpu.VMEM((CONV_WIDTH - 1, SUBLANES, CONV_DIM), jnp.float32)],
        compiler_params=_cparams(1),
        name="short_conv",
    )(u, w_cb, w_cc, w_cx, conv_w)


def _out_kernel(yc_ref, ysa_ref, ysb_ref, u_ref, x_ref, mod_ref,
                wg_ref, wc_ref, ws_ref, wo_ref, o_ref):
    gl = _dot(u_ref[...], wg_ref[0])
    p_conv = _dot(yc_ref[...], wc_ref[0])
    y_ssm = jnp.concatenate(
        [ref[:, i * GROUP_X:(i + 1) * GROUP_X] for i in range(PAIRS) for ref in (ysa_ref, ysb_ref)],
        axis=1)
    p_ssm = _dot(y_ssm, ws_ref[0])
    g_conv = 1.0 / (1.0 + jnp.exp(-gl[:, :D_MODEL]))
    g_ssm = 1.0 / (1.0 + jnp.exp(-gl[:, D_MODEL:]))
    merged = g_conv * p_conv + g_ssm * p_ssm
    mix = _dot(merged.astype(jnp.bfloat16), wo_ref[0])
    gate1 = mod_ref[0, 2:3, :]
    o_ref[...] = x_ref[...] + gate1 * mix


def _out_call(y_conv, ys_a, ys_b, u, x2, mod_l, wg, wc, ws, wo, layer, seq, tm=512):
    t_total = x2.shape[0]
    tiles_per_batch = seq // tm
    row = lambda width: pl.BlockSpec((tm, width), lambda i: (i, 0))
    const = lambda shape: pl.BlockSpec((1,) + shape, lambda i: (layer, 0, 0))
    return pl.pallas_call(
        _out_kernel,
        grid=(t_total // tm,),
        in_specs=[
            row(CONV_DIM), row(D_SSM // 2), row(D_SSM // 2), row(D_MODEL), row(D_MODEL),
            pl.BlockSpec((1, MOD_ROWS, D_MODEL), lambda i: (i // tiles_per_batch, 0, 0)),
            const((D_MODEL, 2 * D_MODEL)), const((CONV_DIM, D_MODEL)), const((D_SSM, D_MODEL)),
            const((D_MODEL, D_MODEL)),
        ],
        out_specs=pl.BlockSpec((tm, D_MODEL), lambda i: (i, 0)),
        out_shape=jax.ShapeDtypeStruct((t_total, D_MODEL), jnp.float32),
        compiler_params=_cparams(1, VMEM_LIMIT_LARGE),
        name="mix_out",
    )(y_conv, ys_a, ys_b, u, x2, mod_l, wg, wc, ws, wo)


def _mlp_kernel(x_ref, mod_ref, ln_ref, wup_ref, wdn_ref, fn_ref, fromperm_ref, o_ref, u_ref,
                acc_ref, *, tm, final):
    j = pl.program_id(1)

    @pl.when(j == 0)
    def _():
        shift = mod_ref[0, 3:4, :]
        scale = mod_ref[0, 4:5, :]
        _modnorm_to(x_ref, ln_ref, scale, shift, u_ref, tm)
        acc_ref[...] = jnp.zeros_like(acc_ref)

    h = jnp.maximum(_dot(u_ref[...], wup_ref[0]), 0.0)
    acc_ref[...] += _dot((h * h).astype(jnp.bfloat16), wdn_ref[0])

    @pl.when(j == pl.num_programs(1) - 1)
    def _():
        gate2 = mod_ref[0, 5:6, :]
        xn = x_ref[...] + gate2 * acc_ref[...]
        if final:
            ms = jnp.mean(xn * xn, axis=-1, keepdims=True)
            acc_ref[...] = xn * lax.rsqrt(ms + EPS) * fn_ref[...]
            _permute_rows(acc_ref, o_ref, fromperm_ref[...], tm)
        else:
            o_ref[...] = xn


def _mlp_call(x2, mod_l, ln, w_up, w_dn, layer, fnorm, from_perm, seq, final, tm=1024, tf=2048):
    t_total = x2.shape[0]
    tiles_per_batch = seq // tm
    return pl.pallas_call(
        functools.partial(_mlp_kernel, tm=tm, final=final),
        grid=(t_total // tm, D_FF // tf),
        in_specs=[
            pl.BlockSpec((tm, D_MODEL), lambda i, j: (i, 0)),
            pl.BlockSpec((1, MOD_ROWS, D_MODEL), lambda i, j: (i // tiles_per_batch, 0, 0)),
            pl.BlockSpec((1, D_MODEL), lambda i, j: (0, 0)),
            pl.BlockSpec((1, D_MODEL, tf), lambda i, j: (layer, 0, j)),
            pl.BlockSpec((1, tf, D_MODEL), lambda i, j: (layer, j, 0)),
            pl.BlockSpec((1, D_MODEL), lambda i, j: (0, 0)),
            pl.BlockSpec((SCAN_CHUNK, SCAN_CHUNK), lambda i, j: (0, 0)),
        ],
        out_specs=pl.BlockSpec((tm, D_MODEL), lambda i, j: (i, 0)),
        out_shape=jax.ShapeDtypeStruct((t_total, D_MODEL), jnp.float32),
        scratch_shapes=[pltpu.VMEM((tm, D_MODEL), jnp.bfloat16),
                        pltpu.VMEM((tm, D_MODEL), jnp.float32)],
        compiler_params=_cparams(2, VMEM_LIMIT_LARGE),
        name="mlp",
    )(x2, mod_l, ln, w_up, w_dn, fnorm, from_perm)


def _head_constants():
    e_t = np.zeros((PART_ROWS, HEADS_PER_GROUP, LANES), np.float32)
    perm = np.zeros((GROUPS, PART_ROWS, 2 * N_SPLIT, HEADS), np.float32)
    for k in range(2 * N_SPLIT):
        for r in range(HEADS_PER_GROUP):
            if k < N_SPLIT:
                e_t[k * HEADS_PER_GROUP + r, r, :] = 1.0
            for g in range(GROUPS):
                perm[g, k * HEADS_PER_GROUP + r, k, g * HEADS_PER_GROUP + r] = 1.0
    e_t = e_t.reshape(PART_ROWS, HEADS_PER_GROUP * LANES)
    perm = perm.reshape(GROUPS * PART_ROWS, 2 * N_SPLIT * HEADS)
    tm = _chunk_time(np.arange(SCAN_CHUNK))
    cum = (tm[:, None] <= tm[None, :]).astype(np.float32)
    to_perm = (np.arange(SCAN_CHUNK)[None, :] == tm[:, None]).astype(np.float32)
    bf = jnp.bfloat16
    return (jnp.asarray(e_t, bf), jnp.asarray(perm, bf), jnp.asarray(cum, bf),
            jnp.asarray(to_perm, bf), jnp.asarray(to_perm.T, bf))


def _regroup_kernel(xs_ref, bs_ref, cs_ref, z_ref, cb_ref, cc_ref, cx_ref, gate_ref, dt_ref,
                    o_ref, ocb_ref, occ_ref, ocx_ref, ogate_ref, odt_ref):
    for ref, lo, hi in ((xs_ref, G_XBC, G_XBC + GROUP_X), (bs_ref, G_XBC + GROUP_X, G_Z - STATE),
                        (cs_ref, G_Z - STATE, G_Z), (z_ref, G_Z, GROUP_COLS)):
        o_ref[0, :, lo:hi] = ref[0].T.astype(o_ref.dtype)
    for ref, out in ((cb_ref, ocb_ref), (cc_ref, occ_ref), (cx_ref, ocx_ref), (gate_ref, ogate_ref)):
        out[0] = ref[0].T.astype(out.dtype)

    @pl.when(pl.program_id(1) == 0)
    def _():
        row = lax.broadcasted_iota(jnp.int32, (LANES, D_MODEL), 0)
        odt_ref[0] = jnp.where(row < HEADS, dt_ref[0], 0.0).astype(odt_ref.dtype)


def _regroup_w_in(w_t):
    def rows(height, off):
        base = off // height
        return pl.BlockSpec((1, height, D_MODEL), lambda l, g: (l, base + g, 0))

    def cols(width):
        return pl.BlockSpec((1, D_MODEL, width), lambda l, g: (l, 0, g))

    bf = jnp.bfloat16
    return pl.pallas_call(
        _regroup_kernel,
        grid=(DEPTH, GROUPS),
        in_specs=[
            rows(GROUP_X, OFF_XS), rows(STATE, OFF_BS), rows(STATE, OFF_CS), rows(GROUP_X, OFF_Z),
            rows(SHORT_PER_STEP, OFF_CB), rows(SHORT_PER_STEP, OFF_CC), rows(SHORT_PER_STEP, OFF_CX),
            rows(GATE_PER_STEP, 0),
            pl.BlockSpec((1, LANES, D_MODEL), lambda l, g: (l, OFF_DT // LANES, 0)),
        ],
        out_specs=[
            cols(GROUP_COLS), cols(SHORT_PER_STEP), cols(SHORT_PER_STEP), cols(SHORT_PER_STEP),
            cols(GATE_PER_STEP),
            pl.BlockSpec((1, LANES, D_MODEL), lambda l, g: (l, 0, 0)),
        ],
        out_shape=[
            jax.ShapeDtypeStruct((DEPTH, D_MODEL, GROUPS * GROUP_COLS), bf),
            jax.ShapeDtypeStruct((DEPTH, D_MODEL, CONV_DIM), bf),
            jax.ShapeDtypeStruct((DEPTH, D_MODEL, CONV_DIM), bf),
            jax.ShapeDtypeStruct((DEPTH, D_MODEL, CONV_DIM), bf),
            jax.ShapeDtypeStruct((DEPTH, D_MODEL, 2 * D_MODEL), bf),
            jax.ShapeDtypeStruct((DEPTH, LANES, D_MODEL), bf),
        ],
        compiler_params=_cparams(2),
        name="w_in_regroup",
    )(*([w_t] * 9))


def _group_conv_params(p):
    k = p.shape[1]
    xs = p[..., :D_SSM].reshape(DEPTH, k, GROUPS, GROUP_X)
    bs = p[..., D_SSM:D_SSM + GROUPS * STATE].reshape(DEPTH, k, GROUPS, STATE)
    cs = p[..., D_SSM + GROUPS * STATE:].reshape(DEPTH, k, GROUPS, STATE)
    return jnp.swapaxes(jnp.concatenate([xs, bs, cs], axis=-1), 1, 2)


def kernel(x, c, w_ada, b_ada, ln1, ln2, w_in, conv_w, ssm_conv_w, ssm_conv_b, dt_bias, a_log,
           d_skip, ssm_norm_w, w_conv_out, w_ssm_out, w_o, w_up, w_down, final_norm):
    bsz, seq, _ = x.shape
    bf = jnp.bfloat16
    f32 = jnp.float32

    c_pad = jnp.zeros((SUBLANES, D_MODEL), f32).at[:bsz].set(c)
    mod_all = _ada_call(c_pad, w_ada, b_ada)
    mod_all = mod_all[:, :bsz].reshape(DEPTH, bsz, N_MOD, D_MODEL)
    mod_all = jnp.pad(mod_all, ((0, 0), (0, 0), (0, MOD_ROWS - N_MOD), (0, 0)))

    w_pairs, w_cb, w_cc, w_cx, w_gate, w_dt = _regroup_w_in(jnp.swapaxes(w_in, 1, 2))
    hp = jnp.broadcast_to(jnp.stack([dt_bias, a_log], axis=1)[..., None],
                          (DEPTH, 2, HEADS, LANES))
    cw = _group_conv_params(ssm_conv_w)
    cbias = _group_conv_params(ssm_conv_b[:, None, :])
    dskip_e = jnp.repeat(d_skip, HEAD_DIM, axis=-1).reshape(DEPTH, GROUPS, 1, GROUP_X)
    nw = ssm_norm_w.reshape(DEPTH, GROUPS, 1, GROUP_X)
    ws = w_ssm_out.astype(bf)
    wc = w_conv_out.astype(bf)
    wo = w_o.astype(bf)
    wup = w_up.astype(bf)
    wdn = w_down.astype(bf)

    e_t, perm, cum, to_perm, from_perm = _head_constants()
    fnorm = final_norm.reshape(1, D_MODEL)
    x2 = x.reshape(bsz * seq, D_MODEL)

    for l in range(DEPTH):
        mod_l = mod_all[l]
        pre_args = (mod_l, ln1[l].reshape(1, D_MODEL), w_dt[l], hp[l], cum, perm, seq)
        if l == 0:
            x2, u, parts = _pre_call(x2, *pre_args, to_perm=to_perm)
        else:
            u, parts = _pre_call(x2, *pre_args)
        y_conv = _sconv_call(u, w_cb, w_cc, w_cx, conv_w, l, seq)
        ys_a, ys_b = _mixer_call(u, w_pairs, l, parts, cw[l], cbias[l], dskip_e[l], nw[l], e_t,
                                 bsz, seq)
        x2 = _out_call(y_conv, ys_a, ys_b, u, x2, mod_l, w_gate, wc, ws, wo, l, seq)
        x2 = _mlp_call(x2, mod_l, ln2[l].reshape(1, D_MODEL), wup, wdn, l, fnorm, from_perm, seq,
                       final=(l == DEPTH - 1))
    return x2.reshape(bsz, seq, D_MODEL)
```

```python
import functools

import numpy as np
import jax
import jax.numpy as jnp
from jax import lax
from jax.experimental import pallas as pl
from jax.experimental.pallas import tpu as pltpu

D_MODEL = 1024
DEPTH = 4
EPS = 1e-6
N_MOD = 6
MOD_ROWS = 8
CONV_DIM = D_MODEL
CONV_WIDTH = 3
D_SSM = 2 * D_MODEL
HEAD_DIM = 64
HEADS = D_SSM // HEAD_DIM
GROUPS = 8
PAIRS = GROUPS // 2
HEADS_PER_GROUP = HEADS // GROUPS
GROUP_X = HEADS_PER_GROUP * HEAD_DIM
STATE = 128
SSM_CONV_WIDTH = 4
D_FF = 4 * D_MODEL
LANES = 128
SUBLANES = 8
F32_TINY = float(np.finfo(np.float32).tiny)
LOG2E = float(np.log2(np.e))

OFF_CB = 2 * D_MODEL
OFF_CC = OFF_CB + CONV_DIM
OFF_CX = OFF_CC + CONV_DIM
OFF_Z = OFF_CX + CONV_DIM
OFF_XS = OFF_Z + D_SSM
OFF_BS = OFF_XS + D_SSM
OFF_CS = OFF_BS + GROUPS * STATE
OFF_DT = OFF_CS + GROUPS * STATE

SCAN_CHUNK = 128
SLABS = SCAN_CHUNK // SUBLANES
GROUP_CONV = GROUP_X + 2 * STATE
SHORT_PER_GROUP = CONV_DIM // GROUPS
GATE_PER_GROUP = 2 * D_MODEL // GROUPS
G_XBC = 0
G_Z = G_XBC + GROUP_CONV
G_CB = G_Z + GROUP_X
G_CC = G_CB + SHORT_PER_GROUP
G_CX = G_CC + SHORT_PER_GROUP
G_GATE = G_CX + SHORT_PER_GROUP
GROUP_COLS = G_GATE + GATE_PER_GROUP
PROJ_PIECE = 256
PIECES_WHILE_ODD = {-1: (0,), 0: (1,), 1: (2,), 3: (3,), 4: (4,), 7: (5,)}
PIECES_WHILE_EVEN = {0: (6,), 2: (7,), 3: (8,), 4: (9,), 6: (10,)}

N_SPLIT = 3
PART_ROWS = 32
BETA_ROW0 = N_SPLIT * HEADS_PER_GROUP

VMEM_LIMIT = 48 * 1024 * 1024
VMEM_LIMIT_LARGE = 56 * 1024 * 1024


def _cparams(n_axes, limit=VMEM_LIMIT):
    return pltpu.CompilerParams(
        dimension_semantics=("arbitrary",) * n_axes,
        vmem_limit_bytes=limit)


def _dot(a, b):
    return jnp.dot(a, b, preferred_element_type=jnp.float32)


def _dot_t(a, b_t):
    return lax.dot_general(a, b_t, (((1,), (1,)), ((), ())), preferred_element_type=jnp.float32)


def _split3(v):
    hi = v.astype(jnp.bfloat16)
    r1 = v - hi.astype(jnp.float32)
    mid = r1.astype(jnp.bfloat16)
    lo = (r1 - mid.astype(jnp.float32)).astype(jnp.bfloat16)
    return hi, mid, lo


def _silu(v):
    return v / (1.0 + jnp.exp2(v * (-LOG2E)))


def _chunk_time(p):
    return (p % SUBLANES) * SLABS + p // SUBLANES


def _ada_kernel(c_ref, w_ref, b_ref, o_ref):
    c = c_ref[...]
    o_ref[0] = _dot(_silu(c), w_ref[0]) + b_ref[0]


def _ada_call(c_pad, w_ada, b_ada):
    n_out = N_MOD * D_MODEL
    tn = D_MODEL
    return pl.pallas_call(
        _ada_kernel,
        grid=(DEPTH, n_out // tn),
        in_specs=[
            pl.BlockSpec((SUBLANES, D_MODEL), lambda l, j: (0, 0)),
            pl.BlockSpec((1, D_MODEL, tn), lambda l, j: (l, 0, j)),
            pl.BlockSpec((1, 1, tn), lambda l, j: (l, 0, j)),
        ],
        out_specs=pl.BlockSpec((1, SUBLANES, tn), lambda l, j: (l, 0, j)),
        out_shape=jax.ShapeDtypeStruct((DEPTH, SUBLANES, n_out), jnp.float32),
        compiler_params=_cparams(2),
        name="ada_mod",
    )(c_pad, w_ada, b_ada.reshape(DEPTH, 1, n_out))


def _modnorm_to(x_ref, ln_ref, scale, shift, u_ref, rows, chunk=256):
    w = ln_ref[...] * (1.0 + scale)

    def body(r, carry):
        r0 = pl.multiple_of(r * chunk, chunk)
        xf = x_ref[pl.ds(r0, chunk), :]
        ms = jnp.mean(xf * xf, axis=-1, keepdims=True)
        y = xf * lax.rsqrt(ms + EPS)
        u_ref[pl.ds(r0, chunk), :] = (y * w + shift).astype(u_ref.dtype)
        return carry

    lax.fori_loop(0, rows // chunk, body, 0)


def _permute_rows(src_ref, dst_ref, onehot_b, rows):
    def body(c, carry):
        r0 = pl.multiple_of(c * SCAN_CHUNK, SCAN_CHUNK)
        hi, mid, lo = _split3(src_ref[pl.ds(r0, SCAN_CHUNK), :])
        dst_ref[pl.ds(r0, SCAN_CHUNK), :] = (_dot(onehot_b, hi) + _dot(onehot_b, mid)
                                             + _dot(onehot_b, lo)).astype(dst_ref.dtype)
        return carry

    lax.fori_loop(0, rows // SCAN_CHUNK, body, 0)


def _pre_kernel(*refs, tp, permute_in):
    if permute_in:
        (xnat_ref, mod_ref, ln_ref, wdt_ref, hp_ref, cum_ref, perm_ref, toperm_ref,
         x_ref, u_ref, parts_ref) = refs
        _permute_rows(xnat_ref, x_ref, toperm_ref[...], tp)
    else:
        x_ref, mod_ref, ln_ref, wdt_ref, hp_ref, cum_ref, perm_ref, u_ref, parts_ref = refs
    shift = mod_ref[0, 0:1, :]
    scale = mod_ref[0, 1:2, :]
    _modnorm_to(x_ref, ln_ref, scale, shift, u_ref, tp)
    dt_raw_t = _dot_t(wdt_ref[...], u_ref[...])[0:HEADS, :]
    dt_bias = hp_ref[0]
    a_neg = -jnp.exp(hp_ref[1])
    cum_t = cum_ref[...]
    perm = perm_ref[...]
    q = SCAN_CHUNK
    for c in range(tp // q):
        r0 = c * q
        dtr = dt_raw_t[:, r0:r0 + q] + dt_bias
        dt = jnp.maximum(dtr, 0.0) + jnp.log1p(jnp.exp(-jnp.abs(dtr)))
        a_parts = _split3(dt * a_neg)
        acum = _dot(a_parts[0], cum_t) + _dot(a_parts[1], cum_t) + _dot(a_parts[2], cum_t)
        beta = acum - jnp.log(jnp.maximum(dt, F32_TINY))
        terms = jnp.concatenate(_split3(acum * LOG2E) + _split3(beta * LOG2E), axis=0)
        parts_ref[:, r0:r0 + q] = _dot(perm, terms).astype(parts_ref.dtype)


def _pre_call(x2, mod_l, ln, w_dt, hp, cum, perm, seq, to_perm=None, tp=1024):
    t_total = x2.shape[0]
    tiles_per_batch = seq // tp
    permute_in = to_perm is not None
    square = pl.BlockSpec((SCAN_CHUNK, SCAN_CHUNK), lambda i: (0, 0))
    rows = pl.BlockSpec((tp, D_MODEL), lambda i: (i, 0))
    return pl.pallas_call(
        functools.partial(_pre_kernel, tp=tp, permute_in=permute_in),
        grid=(t_total // tp,),
        in_specs=[
            rows,
            pl.BlockSpec((1, MOD_ROWS, D_MODEL), lambda i: (i // tiles_per_batch, 0, 0)),
            pl.BlockSpec((1, D_MODEL), lambda i: (0, 0)),
            pl.BlockSpec((LANES, D_MODEL), lambda i: (0, 0)),
            pl.BlockSpec((2, HEADS, LANES), lambda i: (0, 0, 0)),
            square,
            pl.BlockSpec((GROUPS * PART_ROWS, 2 * N_SPLIT * HEADS), lambda i: (0, 0)),
        ] + ([square] if permute_in else []),
        out_specs=([rows] if permute_in else []) + [
            rows,
            pl.BlockSpec((GROUPS * PART_ROWS, tp), lambda i: (0, i)),
        ],
        out_shape=([jax.ShapeDtypeStruct((t_total, D_MODEL), jnp.float32)] if permute_in else []) + [
            jax.ShapeDtypeStruct((t_total, D_MODEL), jnp.bfloat16),
            jax.ShapeDtypeStruct((GROUPS * PART_ROWS, t_total), jnp.bfloat16),
        ],
        compiler_params=_cparams(1),
        name="pre",
    )(*((x2, mod_l, ln, w_dt, hp, cum, perm) + ((to_perm,) if permute_in else ())))


def _shifted_taps(cur, tail, width):
    n_wrap = width - 1
    cur_tail = cur[SLABS - n_wrap:]
    wrapped = jnp.concatenate([tail[:, SUBLANES - 1:, :], cur_tail[:, :SUBLANES - 1, :]], axis=1)
    taps = [cur]
    for s in range(1, width):
        taps.append(jnp.concatenate([wrapped[n_wrap - s:], cur[:SLABS - s]], axis=0))
    return taps, cur_tail


def _project_piece(u_ref, w_ref, pa_ref, pb_ref, piece, tb):
    lo = piece * PROJ_PIECE
    res = _dot(u_ref[...], w_ref[0, :, lo:lo + PROJ_PIECE]
               ).reshape(tb // SUBLANES, SUBLANES, PROJ_PIECE)
    n_a = min(max(GROUP_COLS - lo, 0), PROJ_PIECE)
    if n_a:
        pa_ref[:, :, lo:lo + n_a] = res[:, :, 0:n_a]
    if n_a < PROJ_PIECE:
        b0 = lo + n_a - GROUP_COLS
        pb_ref[:, :, b0:b0 + PROJ_PIECE - n_a] = res[:, :, n_a:]


def _mix_group(p_ref, parts_ref, g, first, cw_ref, cbias_ref, scw_ref, dskip_ref, nw_ref, e_t,
               causal, ys_ref, yc_ref, gate_ref, state_ref, tail_ref, stail_ref, tb, after_chunk,
               hist_src, keep_start):
    q = SCAN_CHUNK
    nt_dims = (((1,), (1,)), ((), ()))
    tn_dims = (((0,), (0,)), ((), ()))
    cw = cw_ref[g]
    cbias = cbias_ref[g]
    scw = scw_ref[g]
    dskip = dskip_ref[g]
    nw = nw_ref[g]

    gate_ref[...] = p_ref[:, :, G_GATE:GROUP_COLS].reshape(tb, GATE_PER_GROUP).astype(gate_ref.dtype)

    state_in = jnp.where(first, 0.0, state_ref[hist_src])
    tail = jnp.where(first, 0.0, tail_ref[hist_src])
    stail = jnp.where(first, 0.0, stail_ref[hist_src])
    if keep_start:
        state_ref[GROUPS] = state_in
        tail_ref[GROUPS] = tail
        stail_ref[GROUPS] = stail
    state_ref[g] = state_in

    after_chunk(-1)
    for c in range(tb // q):
        r0 = c * q
        b0 = c * SLABS
        sv = p_ref[b0:b0 + SLABS, :, G_CC:G_CX] * p_ref[b0:b0 + SLABS, :, G_CX:G_GATE]
        staps, stail = _shifted_taps(sv, stail, CONV_WIDTH)
        sconv = scw[CONV_WIDTH - 1:CONV_WIDTH, :] * staps[0]
        for s in range(1, CONV_WIDTH):
            sconv += scw[CONV_WIDTH - 1 - s:CONV_WIDTH - s, :] * staps[s]
        yc = p_ref[b0:b0 + SLABS, :, G_CB:G_CC] * sconv
        yc_ref[r0:r0 + q, :] = yc.reshape(q, SHORT_PER_GROUP).astype(yc_ref.dtype)

        taps, tail = _shifted_taps(p_ref[b0:b0 + SLABS, :, G_XBC:G_Z], tail, SSM_CONV_WIDTH)
        acc = cbias + cw[SSM_CONV_WIDTH - 1:SSM_CONV_WIDTH, :] * taps[0]
        for s in range(1, SSM_CONV_WIDTH):
            acc += cw[SSM_CONV_WIDTH - 1 - s:SSM_CONV_WIDTH - s, :] * taps[s]
        xbc = _silu(acc).reshape(q, GROUP_CONV)
        xs = xbc[:, 0:GROUP_X]
        xt_b = xs.astype(jnp.bfloat16).T
        b_b = xbc[:, GROUP_X:GROUP_X + STATE].astype(jnp.bfloat16)
        c_b = xbc[:, GROUP_X + STATE:GROUP_CONV].astype(jnp.bfloat16)

        mc = parts_ref[:, r0:r0 + q]
        acum_l = lax.dot_general(mc, e_t, tn_dims, preferred_element_type=jnp.float32)
        acum_last = acum_l[q - 1:q, :]
        mf = mc.astype(jnp.float32)

        scores_b = lax.dot_general(c_b, b_b, nt_dims, preferred_element_type=jnp.float32
                                   ).astype(jnp.bfloat16)
        ys = []
        for r in range(HEADS_PER_GROUP):
            al = acum_l[:, r * LANES:(r + 1) * LANES]
            last = acum_last[:, r * LANES:(r + 1) * LANES]
            beta_row = (mf[BETA_ROW0 + r:BETA_ROW0 + r + 1, :]
                        + mf[BETA_ROW0 + HEADS_PER_GROUP + r:BETA_ROW0 + HEADS_PER_GROUP + r + 1, :]
                        + mf[BETA_ROW0 + 2 * HEADS_PER_GROUP + r:
                             BETA_ROW0 + 2 * HEADS_PER_GROUP + r + 1, :])
            seg = (al - beta_row).astype(jnp.bfloat16)
            w_r = scores_b * jnp.exp2(jnp.where(causal, seg, -jnp.inf))
            c_r = c_b * jnp.exp2(al).astype(jnp.bfloat16)
            xt_r = xt_b[r * HEAD_DIM:(r + 1) * HEAD_DIM, :]
            xw_r = xt_r * jnp.exp2(last - beta_row).astype(jnp.bfloat16)
            s_r = state_ref[g, r]
            ys.append(lax.dot_general(
                jnp.concatenate([w_r, c_r], axis=1),
                jnp.concatenate([xt_r, s_r.astype(jnp.bfloat16)], axis=1),
                nt_dims, preferred_element_type=jnp.float32))
            state_ref[g, r] = s_r * jnp.exp2(last) + _dot(xw_r, b_b)
        y = jnp.concatenate(ys, axis=-1)

        y = y + dskip * xs
        y = y * _silu(p_ref[b0:b0 + SLABS, :, G_Z:G_CB].reshape(q, GROUP_X))
        y = y * lax.rsqrt(jnp.mean(y * y, axis=-1, keepdims=True) + EPS)
        ys_ref[r0:r0 + q, :] = (y * nw).astype(ys_ref.dtype)
        after_chunk(c)

    tail_ref[g] = tail
    stail_ref[g] = stail


def _mixer_kernel(u_ref, w_ref, parts_a_ref, parts_b_ref,
                  cw_ref, cbias_ref, scw_ref, dskip_ref, nw_ref, et_ref,
                  ys_a_ref, yc_a_ref, gate_a_ref, ys_b_ref, yc_b_ref, gate_b_ref,
                  pa_ref, pb_ref, state_ref, tail_ref, stail_ref, *, tb, n_items, nt):
    s = pl.program_id(0)
    q = SCAN_CHUNK

    @pl.when(s == 0)
    def _():
        pb_ref[...] = jnp.zeros_like(pb_ref)
        state_ref[...] = jnp.zeros_like(state_ref)
        tail_ref[...] = jnp.zeros_like(tail_ref)
        stail_ref[...] = jnp.zeros_like(stail_ref)

    item_a = jnp.minimum(s, n_items - 1)
    item_b = jnp.maximum(s - 1, 0)
    g_a = 2 * (item_a % PAIRS)
    g_b = 2 * (item_b % PAIRS) + 1
    first_a = (item_a // PAIRS) % nt == 0
    first_b = (item_b // PAIRS) % nt == 0

    e_t = et_ref[...]
    t_row = _chunk_time(lax.broadcasted_iota(jnp.int32, (q, q), 0))
    t_col = _chunk_time(lax.broadcasted_iota(jnp.int32, (q, q), 1))
    causal = t_row >= t_col
    common = (cw_ref, cbias_ref, scw_ref, dskip_ref, nw_ref, e_t, causal)
    hist = (state_ref, tail_ref, stail_ref, tb)

    def project_after(schedule):
        def after_chunk(c):
            for piece in schedule.get(c, ()):
                _project_piece(u_ref, w_ref, pa_ref, pb_ref, piece, tb)
        return after_chunk

    _mix_group(pb_ref, parts_b_ref, g_b, first_b, *common, ys_b_ref, yc_b_ref, gate_b_ref, *hist,
               project_after(PIECES_WHILE_ODD), g_b, False)
    drain = s == n_items
    _mix_group(pa_ref, parts_a_ref, g_a, first_a, *common, ys_a_ref, yc_a_ref, gate_a_ref, *hist,
               project_after(PIECES_WHILE_EVEN), jnp.where(drain, GROUPS, g_a), True)


def _mixer_call(u, w_pairs, layer, parts, cw, cbias, scw, dskip_e, nw, e_t, bsz, seq, tb=1024):
    t_total = u.shape[0]
    nt = seq // tb
    n_blocks = t_total // tb
    n_items = n_blocks * PAIRS
    item_a = lambda s: jnp.minimum(s, n_items - 1)
    item_b = lambda s: jnp.maximum(s - 1, 0)
    out_a = lambda s: (item_a(s) // PAIRS, item_a(s) % PAIRS)
    out_b = lambda s: (item_b(s) // PAIRS, item_b(s) % PAIRS)

    def full(arr):
        nd = arr.ndim
        return pl.BlockSpec(arr.shape, lambda s: (0,) * nd)

    half = GROUPS // 2
    out_shapes = [
        jax.ShapeDtypeStruct((t_total, half *GROUP_X), jnp.bfloat16),
        jax.ShapeDtypeStruct((t_total, half *SHORT_PER_GROUP), jnp.bfloat16),
        jax.ShapeDtypeStruct((t_total, half *GATE_PER_GROUP), jnp.bfloat16),
    ]
    widths = (GROUP_X, SHORT_PER_GROUP, GATE_PER_GROUP)
    return pl.pallas_call(
        functools.partial(_mixer_kernel, tb=tb, n_items=n_items, nt=nt),
        grid=(n_items + 1,),
        in_specs=[
            pl.BlockSpec((tb, D_MODEL), lambda s: (item_a(s) // PAIRS, 0)),
            pl.BlockSpec((1, D_MODEL, 2 * GROUP_COLS), lambda s: (layer, 0, item_a(s) % PAIRS)),
            pl.BlockSpec((PART_ROWS, tb), lambda s: (2 * (item_a(s) % PAIRS), item_a(s) // PAIRS)),
            pl.BlockSpec((PART_ROWS, tb),
                         lambda s: (2 * (item_b(s) % PAIRS) + 1, item_b(s) // PAIRS)),
            full(cw), full(cbias), full(scw), full(dskip_e), full(nw), full(e_t),
        ],
        out_specs=([pl.BlockSpec((tb, w), out_a) for w in widths]
                   + [pl.BlockSpec((tb, w), out_b) for w in widths]),
        out_shape=out_shapes + out_shapes,
        scratch_shapes=[
            pltpu.VMEM((tb // SUBLANES, SUBLANES, GROUP_COLS), jnp.float32),
            pltpu.VMEM((tb // SUBLANES, SUBLANES, GROUP_COLS), jnp.float32),
            pltpu.VMEM((GROUPS + 1, HEADS_PER_GROUP, HEAD_DIM, STATE), jnp.float32),
            pltpu.VMEM((GROUPS + 1, SSM_CONV_WIDTH - 1, SUBLANES, GROUP_CONV), jnp.float32),
            pltpu.VMEM((GROUPS + 1, CONV_WIDTH - 1, SUBLANES, SHORT_PER_GROUP), jnp.float32),
        ],
        compiler_params=_cparams(1, VMEM_LIMIT_LARGE),
        name="mixer",
    )(u, w_pairs, parts, parts, cw, cbias, scw, dskip_e, nw, e_t)


def _out_kernel(yca_ref, ycb_ref, ysa_ref, ysb_ref, ga_ref, gb_ref, x_ref, mod_ref,
                wc_ref, ws_ref, wo_ref, o_ref):
    p_conv = _dot(jnp.concatenate([yca_ref[...], ycb_ref[...]], axis=1), wc_ref[0])
    y_ssm = jnp.concatenate(
        [ref[:, i * GROUP_X:(i + 1) * GROUP_X] for i in range(PAIRS) for ref in (ysa_ref, ysb_ref)],
        axis=1)
    p_ssm = _dot(y_ssm, ws_ref[0])
    g_conv = 1.0 / (1.0 + jnp.exp(-ga_ref[...].astype(jnp.float32)))
    g_ssm = 1.0 / (1.0 + jnp.exp(-gb_ref[...].astype(jnp.float32)))
    merged = g_conv * p_conv + g_ssm * p_ssm
    mix = _dot(merged.astype(jnp.bfloat16), wo_ref[0])
    gate1 = mod_ref[0, 2:3, :]
    o_ref[...] = x_ref[...] + gate1 * mix


def _out_call(mixed, x2, mod_l, wc, ws, wo, layer, seq, tm=512):
    ys_a, yc_a, g_a, ys_b, yc_b, g_b = mixed
    t_total = x2.shape[0]
    tiles_per_batch = seq // tm
    row = lambda width: pl.BlockSpec((tm, width), lambda i: (i, 0))
    const = lambda shape: pl.BlockSpec((1,) + shape, lambda i: (layer, 0, 0))
    return pl.pallas_call(
        _out_kernel,
        grid=(t_total // tm,),
        in_specs=[
            row(CONV_DIM // 2), row(CONV_DIM // 2), row(D_SSM // 2), row(D_SSM // 2),
            row(D_MODEL), row(D_MODEL), row(D_MODEL),
            pl.BlockSpec((1, MOD_ROWS, D_MODEL), lambda i: (i // tiles_per_batch, 0, 0)),
            const((CONV_DIM, D_MODEL)), const((D_SSM, D_MODEL)), const((D_MODEL, D_MODEL)),
        ],
        out_specs=pl.BlockSpec((tm, D_MODEL), lambda i: (i, 0)),
        out_shape=jax.ShapeDtypeStruct((t_total, D_MODEL), jnp.float32),
        compiler_params=_cparams(1),
        name="mix_out",
    )(yc_a, yc_b, ys_a, ys_b, g_a, g_b, x2, mod_l, wc, ws, wo)


MLP_NORM_ROWS = 128


def _mlp_kernel(x_ref, xnext_ref, mod_ref, modnext_ref, ln_ref, wup_ref, wdn_ref, fn_ref,
                fromperm_ref, o_ref, ua_ref, ub_ref, fin_ref, *, tm, tf, final):
    ln = ln_ref[...]

    def norm_rows(src_ref, src0, mod, u_ref, r0):
        w = ln * (1.0 + mod[0, 4:5, :])
        xf = src_ref[src0 + r0:src0 + r0 + MLP_NORM_ROWS, :]
        y = xf * lax.rsqrt(jnp.mean(xf * xf, axis=-1, keepdims=True) + EPS)
        u_ref[r0:r0 + MLP_NORM_ROWS, :] = (y * w + mod[0, 3:4, :]).astype(u_ref.dtype)

    @pl.when(pl.program_id(0) == 0)
    def _():
        for r0 in range(0, tm, MLP_NORM_ROWS):
            norm_rows(x_ref, 0, mod_ref, ua_ref, r0)

    def mlp_tile(u_ref, row0, side_work):
        acc = None
        for j in range(D_FF // tf):
            h = jnp.maximum(_dot(u_ref[...], wup_ref[0, :, j * tf:(j + 1) * tf]), 0.0)
            side_work()
            d = _dot((h * h).astype(jnp.bfloat16), wdn_ref[0, j * tf:(j + 1) * tf, :])
            side_work()
            acc = d if acc is None else acc + d
        xn = x_ref[row0:row0 + tm, :] + mod_ref[0, 5:6, :] * acc
        if final:
            ms = jnp.mean(xn * xn, axis=-1, keepdims=True)
            fin_ref[...] = xn * lax.rsqrt(ms + EPS) * fn_ref[...]
            _permute_rows(fin_ref, o_ref.at[row0:row0 + tm, :], fromperm_ref[...], tm)
        else:
            o_ref[row0:row0 + tm, :] = xn

    def norm_in_pieces(src_ref, src0, mod, u_ref):
        todo = list(range(0, tm, MLP_NORM_ROWS))
        per_call = -(-len(todo) // (2 * (D_FF // tf)))

        def side_work():
            for _ in range(per_call):
                if todo:
                    norm_rows(src_ref, src0, mod, u_ref, todo.pop(0))
        return side_work

    mlp_tile(ua_ref, 0, norm_in_pieces(x_ref, tm, mod_ref, ub_ref))
    mlp_tile(ub_ref, tm, norm_in_pieces(xnext_ref, 0, modnext_ref, ua_ref))


def _mlp_call(x2, mod_l, ln, w_up, w_dn, layer, fnorm, from_perm, seq, final, tm=512, tf=2048):
    t_total = x2.shape[0]
    n_blocks = t_total // (2 * tm)
    blocks_per_batch = seq // (2 * tm)
    nxt = lambda i: jnp.minimum(i + 1, n_blocks - 1)
    resident = dict(pipeline_mode=pl.Buffered(1))
    return pl.pallas_call(
        functools.partial(_mlp_kernel, tm=tm, tf=tf, final=final),
        grid=(n_blocks,),
        in_specs=[
            pl.BlockSpec((2 * tm, D_MODEL), lambda i: (i, 0)),
            pl.BlockSpec((tm, D_MODEL), lambda i: (2 * nxt(i), 0)),
            pl.BlockSpec((1, MOD_ROWS, D_MODEL), lambda i: (i // blocks_per_batch, 0, 0)),
            pl.BlockSpec((1, MOD_ROWS, D_MODEL), lambda i: (nxt(i) // blocks_per_batch, 0, 0)),
            pl.BlockSpec((1, D_MODEL), lambda i: (0, 0)),
            pl.BlockSpec((1, D_MODEL, D_FF), lambda i: (layer, 0, 0), **resident),
            pl.BlockSpec((1, D_FF, D_MODEL), lambda i: (layer, 0, 0), **resident),
            pl.BlockSpec((1, D_MODEL), lambda i: (0, 0)),
            pl.BlockSpec((SCAN_CHUNK, SCAN_CHUNK), lambda i: (0, 0)),
        ],
        out_specs=pl.BlockSpec((2 * tm, D_MODEL), lambda i: (i, 0)),
        out_shape=jax.ShapeDtypeStruct((t_total, D_MODEL), jnp.float32),
        scratch_shapes=[pltpu.VMEM((tm, D_MODEL), jnp.bfloat16),
                        pltpu.VMEM((tm, D_MODEL), jnp.bfloat16),
                        pltpu.VMEM((tm, D_MODEL), jnp.float32)],
        compiler_params=_cparams(1, VMEM_LIMIT_LARGE),
        name="mlp",
    )(x2, x2, mod_l, mod_l, ln, w_up, w_dn, fnorm, from_perm)


def _head_constants():
    e_t = np.zeros((PART_ROWS, HEADS_PER_GROUP, LANES), np.float32)
    perm = np.zeros((GROUPS, PART_ROWS, 2 * N_SPLIT, HEADS), np.float32)
    for k in range(2 * N_SPLIT):
        for r in range(HEADS_PER_GROUP):
            if k < N_SPLIT:
                e_t[k * HEADS_PER_GROUP + r, r, :] = 1.0
            for g in range(GROUPS):
                perm[g, k * HEADS_PER_GROUP + r, k, g * HEADS_PER_GROUP + r] = 1.0
    e_t = e_t.reshape(PART_ROWS, HEADS_PER_GROUP * LANES)
    perm = perm.reshape(GROUPS * PART_ROWS, 2 * N_SPLIT * HEADS)
    tm = _chunk_time(np.arange(SCAN_CHUNK))
    cum = (tm[:, None] <= tm[None, :]).astype(np.float32)
    to_perm = (np.arange(SCAN_CHUNK)[None, :] == tm[:, None]).astype(np.float32)
    bf = jnp.bfloat16
    return (jnp.asarray(e_t, bf), jnp.asarray(perm, bf), jnp.asarray(cum, bf),
            jnp.asarray(to_perm, bf), jnp.asarray(to_perm.T, bf))


def _regroup_kernel(xs_ref, bs_ref, cs_ref, z_ref, cb_ref, cc_ref, cx_ref, gate_ref, dt_ref,
                    o_ref, odt_ref):
    for ref, lo, hi in ((xs_ref, G_XBC, G_XBC + GROUP_X), (bs_ref, G_XBC + GROUP_X, G_Z - STATE),
                        (cs_ref, G_Z - STATE, G_Z), (z_ref, G_Z, G_CB), (cb_ref, G_CB, G_CC),
                        (cc_ref, G_CC, G_CX), (cx_ref, G_CX, G_GATE), (gate_ref, G_GATE, GROUP_COLS)):
        o_ref[0, :, lo:hi] = ref[0].T.astype(o_ref.dtype)

    @pl.when(pl.program_id(1) == 0)
    def _():
        row = lax.broadcasted_iota(jnp.int32, (LANES, D_MODEL), 0)
        odt_ref[0] = jnp.where(row < HEADS, dt_ref[0], 0.0).astype(odt_ref.dtype)


def _regroup_w_in(w_t):
    def rows(height, off, paired):
        base = off // height
        if paired:
            return pl.BlockSpec((1, height, D_MODEL),
                                lambda l, g: (l, base + (g % 2) * PAIRS + g // 2, 0))
        return pl.BlockSpec((1, height, D_MODEL), lambda l, g: (l, base + g, 0))

    return pl.pallas_call(
        _regroup_kernel,
        grid=(DEPTH, GROUPS),
        in_specs=[
            rows(GROUP_X, OFF_XS, False), rows(STATE, OFF_BS, False), rows(STATE, OFF_CS, False),
            rows(GROUP_X, OFF_Z, False),
            rows(SHORT_PER_GROUP, OFF_CB, True), rows(SHORT_PER_GROUP, OFF_CC, True),
            rows(SHORT_PER_GROUP, OFF_CX, True), rows(GATE_PER_GROUP, 0, True),
            pl.BlockSpec((1, LANES, D_MODEL), lambda l, g: (l, OFF_DT // LANES, 0)),
        ],
        out_specs=[
            pl.BlockSpec((1, D_MODEL, GROUP_COLS), lambda l, g: (l, 0, g)),
            pl.BlockSpec((1, LANES, D_MODEL), lambda l, g: (l, 0, 0)),
        ],
        out_shape=[
            jax.ShapeDtypeStruct((DEPTH, D_MODEL, GROUPS * GROUP_COLS), jnp.bfloat16),
            jax.ShapeDtypeStruct((DEPTH, LANES, D_MODEL), jnp.bfloat16),
        ],
        compiler_params=_cparams(2),
        name="w_in_regroup",
    )(*([w_t] * 9))


def _group_conv_params(p):
    k = p.shape[1]
    xs = p[..., :D_SSM].reshape(DEPTH, k, GROUPS, GROUP_X)
    bs = p[..., D_SSM:D_SSM + GROUPS * STATE].reshape(DEPTH, k, GROUPS, STATE)
    cs = p[..., D_SSM + GROUPS * STATE:].reshape(DEPTH, k, GROUPS, STATE)
    return jnp.swapaxes(jnp.concatenate([xs, bs, cs], axis=-1), 1, 2)


def kernel(x, c, w_ada, b_ada, ln1, ln2, w_in, conv_w, ssm_conv_w, ssm_conv_b, dt_bias, a_log,
           d_skip, ssm_norm_w, w_conv_out, w_ssm_out, w_o, w_up, w_down, final_norm):
    bsz, seq, _ = x.shape
    bf = jnp.bfloat16
    f32 = jnp.float32

    c_pad = jnp.zeros((SUBLANES, D_MODEL), f32).at[:bsz].set(c)
    mod_all = _ada_call(c_pad, w_ada, b_ada)
    mod_all = mod_all[:, :bsz].reshape(DEPTH, bsz, N_MOD, D_MODEL)
    mod_all = jnp.pad(mod_all, ((0, 0), (0, 0), (0, MOD_ROWS - N_MOD), (0, 0)))

    w_pairs, w_dt = _regroup_w_in(jnp.swapaxes(w_in, 1, 2))
    hp = jnp.broadcast_to(jnp.stack([dt_bias, a_log], axis=1)[..., None],
                          (DEPTH, 2, HEADS, LANES))
    cw = _group_conv_params(ssm_conv_w)
    cbias = _group_conv_params(ssm_conv_b[:, None, :])
    scw = conv_w.reshape(DEPTH, CONV_WIDTH, 2, PAIRS, SHORT_PER_GROUP)
    scw = jnp.transpose(scw, (0, 3, 2, 1, 4)).reshape(DEPTH, GROUPS, CONV_WIDTH, SHORT_PER_GROUP)
    dskip_e = jnp.repeat(d_skip, HEAD_DIM, axis=-1).reshape(DEPTH, GROUPS, 1, GROUP_X)
    nw = ssm_norm_w.reshape(DEPTH, GROUPS, 1, GROUP_X)
    ws = w_ssm_out.astype(bf)
    wc = w_conv_out.astype(bf)
    wo = w_o.astype(bf)
    wup = w_up.astype(bf)
    wdn = w_down.astype(bf)

    e_t, perm, cum, to_perm, from_perm = _head_constants()
    fnorm = final_norm.reshape(1, D_MODEL)
    x2 = x.reshape(bsz * seq, D_MODEL)

    for l in range(DEPTH):
        mod_l = mod_all[l]
        pre_args = (mod_l, ln1[l].reshape(1, D_MODEL), w_dt[l], hp[l], cum, perm, seq)
        if l == 0:
            x2, u, parts = _pre_call(x2, *pre_args, to_perm=to_perm)
        else:
            u, parts = _pre_call(x2, *pre_args)
        mixed = _mixer_call(u, w_pairs, l, parts, cw[l], cbias[l], scw[l], dskip_e[l], nw[l], e_t,
                            bsz, seq)
        x2 = _out_call(mixed, x2, mod_l, wc, ws, wo, l, seq)
        x2 = _mlp_call(x2, mod_l, ln2[l].reshape(1, D_MODEL), wup, wdn, l, fnorm, from_perm, seq,
                       final=(l == DEPTH - 1))
    return x2.reshape(bsz, seq, D_MODEL)
```

```python
import functools

import numpy as np
import jax
import jax.numpy as jnp
from jax import lax
from jax.experimental import pallas as pl
from jax.experimental.pallas import tpu as pltpu

D_MODEL = 1024
DEPTH = 4
EPS = 1e-6
N_MOD = 6
MOD_ROWS = 8
CONV_DIM = D_MODEL
CONV_WIDTH = 3
D_SSM = 2 * D_MODEL
HEAD_DIM = 64
HEADS = D_SSM // HEAD_DIM
GROUPS = 8
PAIRS = GROUPS // 2
HEADS_PER_GROUP = HEADS // GROUPS
GROUP_X = HEADS_PER_GROUP * HEAD_DIM
STATE = 128
SSM_CONV_WIDTH = 4
D_FF = 4 * D_MODEL
LANES = 128
SUBLANES = 8
F32_TINY = float(np.finfo(np.float32).tiny)
LOG2E = float(np.log2(np.e))

OFF_CB = 2 * D_MODEL
OFF_CC = OFF_CB + CONV_DIM
OFF_CX = OFF_CC + CONV_DIM
OFF_Z = OFF_CX + CONV_DIM
OFF_XS = OFF_Z + D_SSM
OFF_BS = OFF_XS + D_SSM
OFF_CS = OFF_BS + GROUPS * STATE
OFF_DT = OFF_CS + GROUPS * STATE

SCAN_CHUNK = 128
SLABS = SCAN_CHUNK // SUBLANES
GROUP_CONV = GROUP_X + 2 * STATE
SHORT_PER_GROUP = CONV_DIM // GROUPS
GATE_PER_GROUP = 2 * D_MODEL // GROUPS
G_XBC = 0
G_Z = G_XBC + GROUP_CONV
G_CB = G_Z + GROUP_X
G_CC = G_CB + SHORT_PER_GROUP
G_CX = G_CC + SHORT_PER_GROUP
G_GATE = G_CX + SHORT_PER_GROUP
GROUP_COLS = G_GATE + GATE_PER_GROUP
PROJ_PIECE = 256
PIECES_WHILE_ODD = {-1: (0,), 0: (1,), 1: (2,), 3: (3,), 4: (4,), 7: (5,)}
PIECES_WHILE_EVEN = {0: (6,), 2: (7,), 3: (8,), 4: (9,), 6: (10,)}

N_SPLIT = 3
PART_ROWS = 32
BETA_ROW0 = N_SPLIT * HEADS_PER_GROUP

VMEM_LIMIT = 48 * 1024 * 1024
VMEM_LIMIT_LARGE = 56 * 1024 * 1024


def _cparams(n_axes, limit=VMEM_LIMIT):
    return pltpu.CompilerParams(
        dimension_semantics=("arbitrary",) * n_axes,
        vmem_limit_bytes=limit)


def _dot(a, b):
    return jnp.dot(a, b, preferred_element_type=jnp.float32)


def _dot_t(a, b_t):
    return lax.dot_general(a, b_t, (((1,), (1,)), ((), ())), preferred_element_type=jnp.float32)


def _split3(v):
    hi = v.astype(jnp.bfloat16)
    r1 = v - hi.astype(jnp.float32)
    mid = r1.astype(jnp.bfloat16)
    lo = (r1 - mid.astype(jnp.float32)).astype(jnp.bfloat16)
    return hi, mid, lo


def _silu(v):
    return v / (1.0 + jnp.exp2(v * (-LOG2E)))


def _chunk_time(p):
    return (p % SUBLANES) * SLABS + p // SUBLANES


def _ada_kernel(c_ref, w_ref, b_ref, o_ref):
    c = c_ref[...]
    o_ref[0] = _dot(_silu(c), w_ref[0]) + b_ref[0]


def _ada_call(c_pad, w_ada, b_ada):
    n_out = N_MOD * D_MODEL
    tn = D_MODEL
    return pl.pallas_call(
        _ada_kernel,
        grid=(DEPTH, n_out // tn),
        in_specs=[
            pl.BlockSpec((SUBLANES, D_MODEL), lambda l, j: (0, 0)),
            pl.BlockSpec((1, D_MODEL, tn), lambda l, j: (l, 0, j)),
            pl.BlockSpec((1, 1, tn), lambda l, j: (l, 0, j)),
        ],
        out_specs=pl.BlockSpec((1, SUBLANES, tn), lambda l, j: (l, 0, j)),
        out_shape=jax.ShapeDtypeStruct((DEPTH, SUBLANES, n_out), jnp.float32),
        compiler_params=_cparams(2),
        name="ada_mod",
    )(c_pad, w_ada, b_ada.reshape(DEPTH, 1, n_out))


def _modnorm_to(x_ref, ln_ref, scale, shift, u_ref, rows, chunk=256):
    w = ln_ref[...] * (1.0 + scale)

    def body(r, carry):
        r0 = pl.multiple_of(r * chunk, chunk)
        xf = x_ref[pl.ds(r0, chunk), :]
        ms = jnp.mean(xf * xf, axis=-1, keepdims=True)
        y = xf * lax.rsqrt(ms + EPS)
        u_ref[pl.ds(r0, chunk), :] = (y * w + shift).astype(u_ref.dtype)
        return carry

    lax.fori_loop(0, rows // chunk, body, 0)


def _permute_rows(src_ref, dst_ref, onehot_b, rows):
    def body(c, carry):
        r0 = pl.multiple_of(c * SCAN_CHUNK, SCAN_CHUNK)
        hi, mid, lo = _split3(src_ref[pl.ds(r0, SCAN_CHUNK), :])
        dst_ref[pl.ds(r0, SCAN_CHUNK), :] = (_dot(onehot_b, hi) + _dot(onehot_b, mid)
                                             + _dot(onehot_b, lo)).astype(dst_ref.dtype)
        return carry

    lax.fori_loop(0, rows // SCAN_CHUNK, body, 0)


def _pre_kernel(*refs, tp, permute_in):
    if permute_in:
        (xnat_ref, mod_ref, ln_ref, wdt_ref, hp_ref, cum_ref, perm_ref, toperm_ref,
         x_ref, u_ref, parts_ref) = refs
        _permute_rows(xnat_ref, x_ref, toperm_ref[...], tp)
    else:
        x_ref, mod_ref, ln_ref, wdt_ref, hp_ref, cum_ref, perm_ref, u_ref, parts_ref = refs
    shift = mod_ref[0, 0:1, :]
    scale = mod_ref[0, 1:2, :]
    _modnorm_to(x_ref, ln_ref, scale, shift, u_ref, tp)
    dt_raw_t = _dot_t(wdt_ref[...], u_ref[...])[0:HEADS, :]
    dt_bias = hp_ref[0]
    a_neg = -jnp.exp(hp_ref[1])
    cum_t = cum_ref[...]
    perm = perm_ref[...]
    q = SCAN_CHUNK
    for c in range(tp // q):
        r0 = c * q
        dtr = dt_raw_t[:, r0:r0 + q] + dt_bias
        dt = jnp.maximum(dtr, 0.0) + jnp.log1p(jnp.exp(-jnp.abs(dtr)))
        a_parts = _split3(dt * a_neg)
        acum = _dot(a_parts[0], cum_t) + _dot(a_parts[1], cum_t) + _dot(a_parts[2], cum_t)
        beta = acum - jnp.log(jnp.maximum(dt, F32_TINY))
        terms = jnp.concatenate(_split3(acum * LOG2E) + _split3(beta * LOG2E), axis=0)
        parts_ref[:, r0:r0 + q] = _dot(perm, terms).astype(parts_ref.dtype)


def _pre_call(x2, mod_l, ln, w_dt, hp, cum, perm, seq, to_perm=None, tp=1024):
    t_total = x2.shape[0]
    tiles_per_batch = seq // tp
    permute_in = to_perm is not None
    square = pl.BlockSpec((SCAN_CHUNK, SCAN_CHUNK), lambda i: (0, 0))
    rows = pl.BlockSpec((tp, D_MODEL), lambda i: (i, 0))
    return pl.pallas_call(
        functools.partial(_pre_kernel, tp=tp, permute_in=permute_in),
        grid=(t_total // tp,),
        in_specs=[
            rows,
            pl.BlockSpec((1, MOD_ROWS, D_MODEL), lambda i: (i // tiles_per_batch, 0, 0)),
            pl.BlockSpec((1, D_MODEL), lambda i: (0, 0)),
            pl.BlockSpec((LANES, D_MODEL), lambda i: (0, 0)),
            pl.BlockSpec((2, HEADS, LANES), lambda i: (0, 0, 0)),
            square,
            pl.BlockSpec((GROUPS * PART_ROWS, 2 * N_SPLIT * HEADS), lambda i: (0, 0)),
        ] + ([square] if permute_in else []),
        out_specs=([rows] if permute_in else []) + [
            rows,
            pl.BlockSpec((GROUPS * PART_ROWS, tp), lambda i: (0, i)),
        ],
        out_shape=([jax.ShapeDtypeStruct((t_total, D_MODEL), jnp.float32)] if permute_in else []) + [
            jax.ShapeDtypeStruct((t_total, D_MODEL), jnp.bfloat16),
            jax.ShapeDtypeStruct((GROUPS * PART_ROWS, t_total), jnp.bfloat16),
        ],
        compiler_params=_cparams(1),
        name="pre",
    )(*((x2, mod_l, ln, w_dt, hp, cum, perm) + ((to_perm,) if permute_in else ())))


def _shifted_taps(cur, tail, width):
    n_wrap = width - 1
    cur_tail = cur[SLABS - n_wrap:]
    wrapped = jnp.concatenate([tail[:, SUBLANES - 1:, :], cur_tail[:, :SUBLANES - 1, :]], axis=1)
    taps = [cur]
    for s in range(1, width):
        taps.append(jnp.concatenate([wrapped[n_wrap - s:], cur[:SLABS - s]], axis=0))
    return taps, cur_tail


def _project_piece(u_ref, w_ref, pa_ref, pb_ref, piece, tb):
    lo = piece * PROJ_PIECE
    res = _dot(u_ref[...], w_ref[0, :, lo:lo + PROJ_PIECE]
               ).reshape(tb // SUBLANES, SUBLANES, PROJ_PIECE)
    n_a = min(max(GROUP_COLS - lo, 0), PROJ_PIECE)
    if n_a:
        pa_ref[:, :, lo:lo + n_a] = res[:, :, 0:n_a]
    if n_a < PROJ_PIECE:
        b0 = lo + n_a - GROUP_COLS
        pb_ref[:, :, b0:b0 + PROJ_PIECE - n_a] = res[:, :, n_a:]


def _mix_group(p_ref, parts_ref, g, first, cw_ref, cbias_ref, scw_ref, dskip_ref, nw_ref, e_t,
               causal, ys_ref, yc_ref, gate_ref, state_ref, tail_ref, stail_ref, tb, after_chunk,
               hist_src, keep_start):
    q = SCAN_CHUNK
    nt_dims = (((1,), (1,)), ((), ()))
    tn_dims = (((0,), (0,)), ((), ()))
    cw = cw_ref[g]
    cbias = cbias_ref[g]
    scw = scw_ref[g]
    dskip = dskip_ref[g]
    nw = nw_ref[g]

    gate_ref[...] = p_ref[:, :, G_GATE:GROUP_COLS].reshape(tb, GATE_PER_GROUP).astype(gate_ref.dtype)

    state_in = jnp.where(first, 0.0, state_ref[hist_src])
    tail = jnp.where(first, 0.0, tail_ref[hist_src])
    stail = jnp.where(first, 0.0, stail_ref[hist_src])
    if keep_start:
        state_ref[GROUPS] = state_in
        tail_ref[GROUPS] = tail
        stail_ref[GROUPS] = stail
    state_ref[g] = state_in

    after_chunk(-1)
    for c in range(tb // q):
        r0 = c * q
        b0 = c * SLABS
        sv = p_ref[b0:b0 + SLABS, :, G_CC:G_CX] * p_ref[b0:b0 + SLABS, :, G_CX:G_GATE]
        staps, stail = _shifted_taps(sv, stail, CONV_WIDTH)
        sconv = scw[CONV_WIDTH - 1:CONV_WIDTH, :] * staps[0]
        for s in range(1, CONV_WIDTH):
            sconv += scw[CONV_WIDTH - 1 - s:CONV_WIDTH - s, :] * staps[s]
        yc = p_ref[b0:b0 + SLABS, :, G_CB:G_CC] * sconv
        yc_ref[r0:r0 + q, :] = yc.reshape(q, SHORT_PER_GROUP).astype(yc_ref.dtype)

        taps, tail = _shifted_taps(p_ref[b0:b0 + SLABS, :, G_XBC:G_Z], tail, SSM_CONV_WIDTH)
        acc = cbias + cw[SSM_CONV_WIDTH - 1:SSM_CONV_WIDTH, :] * taps[0]
        for s in range(1, SSM_CONV_WIDTH):
            acc += cw[SSM_CONV_WIDTH - 1 - s:SSM_CONV_WIDTH - s, :] * taps[s]
        xbc = _silu(acc).reshape(q, GROUP_CONV)
        xs = xbc[:, 0:GROUP_X]
        xt_b = xs.astype(jnp.bfloat16).T
        b_b = xbc[:, GROUP_X:GROUP_X + STATE].astype(jnp.bfloat16)
        c_b = xbc[:, GROUP_X + STATE:GROUP_CONV].astype(jnp.bfloat16)

        mc = parts_ref[:, r0:r0 + q]
        acum_l = lax.dot_general(mc, e_t, tn_dims, preferred_element_type=jnp.float32)
        acum_last = acum_l[q - 1:q, :]
        mf = mc.astype(jnp.float32)

        scores_b = lax.dot_general(c_b, b_b, nt_dims, preferred_element_type=jnp.float32
                                   ).astype(jnp.bfloat16)
        ys = []
        for r in range(HEADS_PER_GROUP):
            al = acum_l[:, r * LANES:(r + 1) * LANES]
            last = acum_last[:, r * LANES:(r + 1) * LANES]
            beta_row = (mf[BETA_ROW0 + r:BETA_ROW0 + r + 1, :]
                        + mf[BETA_ROW0 + HEADS_PER_GROUP + r:BETA_ROW0 + HEADS_PER_GROUP + r + 1, :]
                        + mf[BETA_ROW0 + 2 * HEADS_PER_GROUP + r:
                             BETA_ROW0 + 2 * HEADS_PER_GROUP + r + 1, :])
            seg = (al - beta_row).astype(jnp.bfloat16)
            w_r = scores_b * jnp.exp2(jnp.where(causal, seg, -jnp.inf))
            c_r = c_b * jnp.exp2(al).astype(jnp.bfloat16)
            xt_r = xt_b[r * HEAD_DIM:(r + 1) * HEAD_DIM, :]
            xw_r = xt_r * jnp.exp2(last - beta_row).astype(jnp.bfloat16)
            s_r = state_ref[g, r]
            ys.append(lax.dot_general(
                jnp.concatenate([w_r, c_r], axis=1),
                jnp.concatenate([xt_r, s_r.astype(jnp.bfloat16)], axis=1),
                nt_dims, preferred_element_type=jnp.float32))
            state_ref[g, r] = s_r * jnp.exp2(last) + _dot(xw_r, b_b)
        y = jnp.concatenate(ys, axis=-1)

        y = y + dskip * xs
        y = y * _silu(p_ref[b0:b0 + SLABS, :, G_Z:G_CB].reshape(q, GROUP_X))
        y = y * lax.rsqrt(jnp.mean(y * y, axis=-1, keepdims=True) + EPS)
        ys_ref[r0:r0 + q, :] = (y * nw).astype(ys_ref.dtype)
        after_chunk(c)

    tail_ref[g] = tail
    stail_ref[g] = stail


def _mixer_kernel(u_ref, w_ref, parts_a_ref, parts_b_ref,
                  cw_ref, cbias_ref, scw_ref, dskip_ref, nw_ref, et_ref,
                  ys_a_ref, yc_a_ref, gate_a_ref, ys_b_ref, yc_b_ref, gate_b_ref,
                  pa_ref, pb_ref, state_ref, tail_ref, stail_ref, *, tb, n_items, nt):
    s = pl.program_id(0)
    q = SCAN_CHUNK

    @pl.when(s == 0)
    def _():
        pb_ref[...] = jnp.zeros_like(pb_ref)
        state_ref[...] = jnp.zeros_like(state_ref)
        tail_ref[...] = jnp.zeros_like(tail_ref)
        stail_ref[...] = jnp.zeros_like(stail_ref)

    item_a = jnp.minimum(s, n_items - 1)
    item_b = jnp.maximum(s - 1, 0)
    g_a = 2 * (item_a % PAIRS)
    g_b = 2 * (item_b % PAIRS) + 1
    first_a = (item_a // PAIRS) % nt == 0
    first_b = (item_b // PAIRS) % nt == 0

    e_t = et_ref[...]
    t_row = _chunk_time(lax.broadcasted_iota(jnp.int32, (q, q), 0))
    t_col = _chunk_time(lax.broadcasted_iota(jnp.int32, (q, q), 1))
    causal = t_row >= t_col
    common = (cw_ref, cbias_ref, scw_ref, dskip_ref, nw_ref, e_t, causal)
    hist = (state_ref, tail_ref, stail_ref, tb)

    def project_after(schedule):
        def after_chunk(c):
            for piece in schedule.get(c, ()):
                _project_piece(u_ref, w_ref, pa_ref, pb_ref, piece, tb)
        return after_chunk

    _mix_group(pb_ref, parts_b_ref, g_b, first_b, *common, ys_b_ref, yc_b_ref, gate_b_ref, *hist,
               project_after(PIECES_WHILE_ODD), g_b, False)
    drain = s == n_items
    _mix_group(pa_ref, parts_a_ref, g_a, first_a, *common, ys_a_ref, yc_a_ref, gate_a_ref, *hist,
               project_after(PIECES_WHILE_EVEN), jnp.where(drain, GROUPS, g_a), True)


def _mixer_call(u, w_pairs, layer, parts, cw, cbias, scw, dskip_e, nw, e_t, bsz, seq, tb=1024):
    t_total = u.shape[0]
    nt = seq // tb
    n_blocks = t_total // tb
    n_items = n_blocks * PAIRS
    item_a = lambda s: jnp.minimum(s, n_items - 1)
    item_b = lambda s: jnp.maximum(s - 1, 0)
    out_a = lambda s: (item_a(s) // PAIRS, item_a(s) % PAIRS)
    out_b = lambda s: (item_b(s) // PAIRS, item_b(s) % PAIRS)

    def full(arr):
        nd = arr.ndim
        return pl.BlockSpec(arr.shape, lambda s: (0,) * nd)

    half = GROUPS // 2
    out_shapes = [
        jax.ShapeDtypeStruct((t_total, half *GROUP_X), jnp.bfloat16),
        jax.ShapeDtypeStruct((t_total, half *SHORT_PER_GROUP), jnp.bfloat16),
        jax.ShapeDtypeStruct((t_total, half *GATE_PER_GROUP), jnp.bfloat16),
    ]
    widths = (GROUP_X, SHORT_PER_GROUP, GATE_PER_GROUP)
    return pl.pallas_call(
        functools.partial(_mixer_kernel, tb=tb, n_items=n_items, nt=nt),
        grid=(n_items + 1,),
        in_specs=[
            pl.BlockSpec((tb, D_MODEL), lambda s: (item_a(s) // PAIRS, 0)),
            pl.BlockSpec((1, D_MODEL, 2 * GROUP_COLS), lambda s: (layer, 0, item_a(s) % PAIRS)),
            pl.BlockSpec((PART_ROWS, tb), lambda s: (2 * (item_a(s) % PAIRS), item_a(s) // PAIRS)),
            pl.BlockSpec((PART_ROWS, tb),
                         lambda s: (2 * (item_b(s) % PAIRS) + 1, item_b(s) // PAIRS)),
            full(cw), full(cbias), full(scw), full(dskip_e), full(nw), full(e_t),
        ],
        out_specs=([pl.BlockSpec((tb, w), out_a) for w in widths]
                   + [pl.BlockSpec((tb, w), out_b) for w in widths]),
        out_shape=out_shapes + out_shapes,
        scratch_shapes=[
            pltpu.VMEM((tb // SUBLANES, SUBLANES, GROUP_COLS), jnp.float32),
            pltpu.VMEM((tb // SUBLANES, SUBLANES, GROUP_COLS), jnp.float32),
            pltpu.VMEM((GROUPS + 1, HEADS_PER_GROUP, HEAD_DIM, STATE), jnp.float32),
            pltpu.VMEM((GROUPS + 1, SSM_CONV_WIDTH - 1, SUBLANES, GROUP_CONV), jnp.float32),
            pltpu.VMEM((GROUPS + 1, CONV_WIDTH - 1, SUBLANES, SHORT_PER_GROUP), jnp.float32),
        ],
        compiler_params=_cparams(1, VMEM_LIMIT_LARGE),
        name="mixer",
    )(u, w_pairs, parts, parts, cw, cbias, scw, dskip_e, nw, e_t)


def _out_kernel(yca_ref, ycb_ref, ysa_ref, ysb_ref, ga_ref, gb_ref, x_ref, mod_ref,
                wc_ref, ws_ref, wo_ref, o_ref):
    p_conv = _dot(jnp.concatenate([yca_ref[...], ycb_ref[...]], axis=1), wc_ref[0])
    y_ssm = jnp.concatenate(
        [ref[:, i * GROUP_X:(i + 1) * GROUP_X] for i in range(PAIRS) for ref in (ysa_ref, ysb_ref)],
        axis=1)
    p_ssm = _dot(y_ssm, ws_ref[0])
    g_conv = 1.0 / (1.0 + jnp.exp(-ga_ref[...].astype(jnp.float32)))
    g_ssm = 1.0 / (1.0 + jnp.exp(-gb_ref[...].astype(jnp.float32)))
    merged = g_conv * p_conv + g_ssm * p_ssm
    mix = _dot(merged.astype(jnp.bfloat16), wo_ref[0])
    gate1 = mod_ref[0, 2:3, :]
    o_ref[...] = x_ref[...] + gate1 * mix


def _out_call(mixed, x2, mod_l, wc, ws, wo, layer, seq, tm=512):
    ys_a, yc_a, g_a, ys_b, yc_b, g_b = mixed
    t_total = x2.shape[0]
    tiles_per_batch = seq // tm
    row = lambda width: pl.BlockSpec((tm, width), lambda i: (i, 0))
    const = lambda shape: pl.BlockSpec((1,) + shape, lambda i: (layer, 0, 0))
    return pl.pallas_call(
        _out_kernel,
        grid=(t_total // tm,),
        in_specs=[
            row(CONV_DIM // 2), row(CONV_DIM // 2), row(D_SSM // 2), row(D_SSM // 2),
            row(D_MODEL), row(D_MODEL), row(D_MODEL),
            pl.BlockSpec((1, MOD_ROWS, D_MODEL), lambda i: (i // tiles_per_batch, 0, 0)),
            const((CONV_DIM, D_MODEL)), const((D_SSM, D_MODEL)), const((D_MODEL, D_MODEL)),
        ],
        out_specs=pl.BlockSpec((tm, D_MODEL), lambda i: (i, 0)),
        out_shape=jax.ShapeDtypeStruct((t_total, D_MODEL), jnp.float32),
        compiler_params=_cparams(1),
        name="mix_out",
    )(yc_a, yc_b, ys_a, ys_b, g_a, g_b, x2, mod_l, wc, ws, wo)


MLP_NORM_ROWS = 128


def _mlp_kernel(x_ref, xnext_ref, mod_ref, modnext_ref, ln_ref, wup_ref, wdn_ref, fn_ref,
                fromperm_ref, o_ref, ua_ref, ub_ref, *, tm, tf, final):
    ln = ln_ref[...]

    def norm_rows(src_ref, src0, mod, u_ref, r0):
        w = ln * (1.0 + mod[0, 4:5, :])
        xf = src_ref[src0 + r0:src0 + r0 + MLP_NORM_ROWS, :]
        y = xf * lax.rsqrt(jnp.mean(xf * xf, axis=-1, keepdims=True) + EPS)
        u_ref[r0:r0 + MLP_NORM_ROWS, :] = (y * w + mod[0, 3:4, :]).astype(u_ref.dtype)

    @pl.when(pl.program_id(0) == 0)
    def _():
        for r0 in range(0, tm, MLP_NORM_ROWS):
            norm_rows(x_ref, 0, mod_ref, ua_ref, r0)

    def mlp_tile(u_ref, row0, side_work):
        acc = None
        for j in range(D_FF // tf):
            h = jnp.maximum(_dot(u_ref[...], wup_ref[0, :, j * tf:(j + 1) * tf]), 0.0)
            side_work()
            d = _dot((h * h).astype(jnp.bfloat16), wdn_ref[0, j * tf:(j + 1) * tf, :])
            side_work()
            acc = d if acc is None else acc + d
        xn = x_ref[row0:row0 + tm, :] + mod_ref[0, 5:6, :] * acc
        if final:
            ms = jnp.mean(xn * xn, axis=-1, keepdims=True)
            xf = xn * lax.rsqrt(ms + EPS) * fn_ref[...]
            onehot = fromperm_ref[...]
            for c0 in range(0, tm, SCAN_CHUNK):
                hi, mid, lo = _split3(xf[c0:c0 + SCAN_CHUNK, :])
                o_ref[row0 + c0:row0 + c0 + SCAN_CHUNK, :] = (
                    _dot(onehot, hi) + _dot(onehot, mid) + _dot(onehot, lo))
        else:
            o_ref[row0:row0 + tm, :] = xn

    def norm_in_pieces(src_ref, src0, mod, u_ref):
        todo = list(range(0, tm, MLP_NORM_ROWS))
        per_call = -(-len(todo) // (2 * (D_FF // tf)))

        def side_work():
            for _ in range(per_call):
                if todo:
                    norm_rows(src_ref, src0, mod, u_ref, todo.pop(0))
        return side_work

    mlp_tile(ua_ref, 0, norm_in_pieces(x_ref, tm, mod_ref, ub_ref))
    mlp_tile(ub_ref, tm, norm_in_pieces(xnext_ref, 0, modnext_ref, ua_ref))


def _mlp_call(x2, mod_l, ln, w_up, w_dn, layer, fnorm, from_perm, seq, final, tm=512, tf=2048):
    t_total = x2.shape[0]
    n_blocks = t_total // (2 * tm)
    blocks_per_batch = seq // (2 * tm)
    nxt = lambda i: jnp.minimum(i + 1, n_blocks - 1)
    resident = dict(pipeline_mode=pl.Buffered(1))
    return pl.pallas_call(
        functools.partial(_mlp_kernel, tm=tm, tf=tf, final=final),
        grid=(n_blocks,),
        in_specs=[
            pl.BlockSpec((2 * tm, D_MODEL), lambda i: (i, 0)),
            pl.BlockSpec((tm, D_MODEL), lambda i: (2 * nxt(i), 0)),
            pl.BlockSpec((1, MOD_ROWS, D_MODEL), lambda i: (i // blocks_per_batch, 0, 0)),
            pl.BlockSpec((1, MOD_ROWS, D_MODEL), lambda i: (nxt(i) // blocks_per_batch, 0, 0)),
            pl.BlockSpec((1, D_MODEL), lambda i: (0, 0)),
            pl.BlockSpec((1, D_MODEL, D_FF), lambda i: (layer, 0, 0), **resident),
            pl.BlockSpec((1, D_FF, D_MODEL), lambda i: (layer, 0, 0), **resident),
            pl.BlockSpec((1, D_MODEL), lambda i: (0, 0)),
            pl.BlockSpec((SCAN_CHUNK, SCAN_CHUNK), lambda i: (0, 0)),
        ],
        out_specs=pl.BlockSpec((2 * tm, D_MODEL), lambda i: (i, 0)),
        out_shape=jax.ShapeDtypeStruct((t_total, D_MODEL), jnp.float32),
        scratch_shapes=[pltpu.VMEM((tm, D_MODEL), jnp.bfloat16),
                        pltpu.VMEM((tm, D_MODEL), jnp.bfloat16)],
        compiler_params=_cparams(1, VMEM_LIMIT_LARGE),
        name="mlp",
    )(x2, x2, mod_l, mod_l, ln, w_up, w_dn, fnorm, from_perm)


def _head_constants():
    e_t = np.zeros((PART_ROWS, HEADS_PER_GROUP, LANES), np.float32)
    perm = np.zeros((GROUPS, PART_ROWS, 2 * N_SPLIT, HEADS), np.float32)
    for k in range(2 * N_SPLIT):
        for r in range(HEADS_PER_GROUP):
            if k < N_SPLIT:
                e_t[k * HEADS_PER_GROUP + r, r, :] = 1.0
            for g in range(GROUPS):
                perm[g, k * HEADS_PER_GROUP + r, k, g * HEADS_PER_GROUP + r] = 1.0
    e_t = e_t.reshape(PART_ROWS, HEADS_PER_GROUP * LANES)
    perm = perm.reshape(GROUPS * PART_ROWS, 2 * N_SPLIT * HEADS)
    tm = _chunk_time(np.arange(SCAN_CHUNK))
    cum = (tm[:, None] <= tm[None, :]).astype(np.float32)
    to_perm = (np.arange(SCAN_CHUNK)[None, :] == tm[:, None]).astype(np.float32)
    bf = jnp.bfloat16
    return (jnp.asarray(e_t, bf), jnp.asarray(perm, bf), jnp.asarray(cum, bf),
            jnp.asarray(to_perm, bf), jnp.asarray(to_perm.T, bf))


def _regroup_kernel(xs_ref, bs_ref, cs_ref, z_ref, cb_ref, cc_ref, cx_ref, gate_ref, dt_ref,
                    o_ref, odt_ref):
    for ref, lo, hi in ((xs_ref, G_XBC, G_XBC + GROUP_X), (bs_ref, G_XBC + GROUP_X, G_Z - STATE),
                        (cs_ref, G_Z - STATE, G_Z), (z_ref, G_Z, G_CB), (cb_ref, G_CB, G_CC),
                        (cc_ref, G_CC, G_CX), (cx_ref, G_CX, G_GATE), (gate_ref, G_GATE, GROUP_COLS)):
        o_ref[0, :, lo:hi] = ref[0].T.astype(o_ref.dtype)

    @pl.when(pl.program_id(1) == 0)
    def _():
        row = lax.broadcasted_iota(jnp.int32, (LANES, D_MODEL), 0)
        odt_ref[0] = jnp.where(row < HEADS, dt_ref[0], 0.0).astype(odt_ref.dtype)


def _regroup_w_in(w_t):
    def rows(height, off, paired):
        base = off // height
        if paired:
            return pl.BlockSpec((1, height, D_MODEL),
                                lambda l, g: (l, base + (g % 2) * PAIRS + g // 2, 0))
        return pl.BlockSpec((1, height, D_MODEL), lambda l, g: (l, base + g, 0))

    return pl.pallas_call(
        _regroup_kernel,
        grid=(DEPTH, GROUPS),
        in_specs=[
            rows(GROUP_X, OFF_XS, False), rows(STATE, OFF_BS, False), rows(STATE, OFF_CS, False),
            rows(GROUP_X, OFF_Z, False),
            rows(SHORT_PER_GROUP, OFF_CB, True), rows(SHORT_PER_GROUP, OFF_CC, True),
            rows(SHORT_PER_GROUP, OFF_CX, True), rows(GATE_PER_GROUP, 0, True),
            pl.BlockSpec((1, LANES, D_MODEL), lambda l, g: (l, OFF_DT // LANES, 0)),
        ],
        out_specs=[
            pl.BlockSpec((1, D_MODEL, GROUP_COLS), lambda l, g: (l, 0, g)),
            pl.BlockSpec((1, LANES, D_MODEL), lambda l, g: (l, 0, 0)),
        ],
        out_shape=[
            jax.ShapeDtypeStruct((DEPTH, D_MODEL, GROUPS * GROUP_COLS), jnp.bfloat16),
            jax.ShapeDtypeStruct((DEPTH, LANES, D_MODEL), jnp.bfloat16),
        ],
        compiler_params=_cparams(2),
        name="w_in_regroup",
    )(*([w_t] * 9))


def _group_conv_params(p):
    k = p.shape[1]
    xs = p[..., :D_SSM].reshape(DEPTH, k, GROUPS, GROUP_X)
    bs = p[..., D_SSM:D_SSM + GROUPS * STATE].reshape(DEPTH, k, GROUPS, STATE)
    cs = p[..., D_SSM + GROUPS * STATE:].reshape(DEPTH, k, GROUPS, STATE)
    return jnp.swapaxes(jnp.concatenate([xs, bs, cs], axis=-1), 1, 2)


def kernel(x, c, w_ada, b_ada, ln1, ln2, w_in, conv_w, ssm_conv_w, ssm_conv_b, dt_bias, a_log,
           d_skip, ssm_norm_w, w_conv_out, w_ssm_out, w_o, w_up, w_down, final_norm):
    bsz, seq, _ = x.shape
    bf = jnp.bfloat16
    f32 = jnp.float32

    c_pad = jnp.zeros((SUBLANES, D_MODEL), f32).at[:bsz].set(c)
    mod_all = _ada_call(c_pad, w_ada, b_ada)
    mod_all = mod_all[:, :bsz].reshape(DEPTH, bsz, N_MOD, D_MODEL)
    mod_all = jnp.pad(mod_all, ((0, 0), (0, 0), (0, MOD_ROWS - N_MOD), (0, 0)))

    w_pairs, w_dt = _regroup_w_in(jnp.swapaxes(w_in, 1, 2))
    hp = jnp.broadcast_to(jnp.stack([dt_bias, a_log], axis=1)[..., None],
                          (DEPTH, 2, HEADS, LANES))
    cw = _group_conv_params(ssm_conv_w)
    cbias = _group_conv_params(ssm_conv_b[:, None, :])
    scw = conv_w.reshape(DEPTH, CONV_WIDTH, 2, PAIRS, SHORT_PER_GROUP)
    scw = jnp.transpose(scw, (0, 3, 2, 1, 4)).reshape(DEPTH, GROUPS, CONV_WIDTH, SHORT_PER_GROUP)
    dskip_e = jnp.repeat(d_skip, HEAD_DIM, axis=-1).reshape(DEPTH, GROUPS, 1, GROUP_X)
    nw = ssm_norm_w.reshape(DEPTH, GROUPS, 1, GROUP_X)
    ws = w_ssm_out.astype(bf)
    wc = w_conv_out.astype(bf)
    wo = w_o.astype(bf)
    wup = w_up.astype(bf)
    wdn = w_down.astype(bf)

    e_t, perm, cum, to_perm, from_perm = _head_constants()
    fnorm = final_norm.reshape(1, D_MODEL)
    x2 = x.reshape(bsz * seq, D_MODEL)

    for l in range(DEPTH):
        mod_l = mod_all[l]
        pre_args = (mod_l, ln1[l].reshape(1, D_MODEL), w_dt[l], hp[l], cum, perm, seq)
        if l == 0:
            x2, u, parts = _pre_call(x2, *pre_args, to_perm=to_perm)
        else:
            u, parts = _pre_call(x2, *pre_args)
        mixed = _mixer_call(u, w_pairs, l, parts, cw[l], cbias[l], scw[l], dskip_e[l], nw[l], e_t,
                            bsz, seq)
        x2 = _out_call(mixed, x2, mod_l, wc, ws, wo, l, seq)
        x2 = _mlp_call(x2, mod_l, ln2[l].reshape(1, D_MODEL), wup, wdn, l, fnorm, from_perm, seq,
                       final=(l == DEPTH - 1))
    return x2.reshape(bsz, seq, D_MODEL)
```

```python
import functools

import numpy as np
import jax
import jax.numpy as jnp
from jax import lax
from jax.experimental import pallas as pl
from jax.experimental.pallas import tpu as pltpu

D_MODEL = 1024
DEPTH = 4
EPS = 1e-6
N_MOD = 6
MOD_ROWS = 8
CONV_DIM = D_MODEL
CONV_WIDTH = 3
D_SSM = 2 * D_MODEL
HEAD_DIM = 64
HEADS = D_SSM // HEAD_DIM
GROUPS = 8
PAIRS = GROUPS // 2
HEADS_PER_GROUP = HEADS // GROUPS
GROUP_X = HEADS_PER_GROUP * HEAD_DIM
STATE = 128
SSM_CONV_WIDTH = 4
D_FF = 4 * D_MODEL
LANES = 128
SUBLANES = 8
F32_TINY = float(np.finfo(np.float32).tiny)
LOG2E = float(np.log2(np.e))

OFF_CB = 2 * D_MODEL
OFF_CC = OFF_CB + CONV_DIM
OFF_CX = OFF_CC + CONV_DIM
OFF_Z = OFF_CX + CONV_DIM
OFF_XS = OFF_Z + D_SSM
OFF_BS = OFF_XS + D_SSM
OFF_CS = OFF_BS + GROUPS * STATE
OFF_DT = OFF_CS + GROUPS * STATE

SCAN_CHUNK = 128
SLABS = SCAN_CHUNK // SUBLANES
GROUP_CONV = GROUP_X + 2 * STATE
SHORT_PER_GROUP = CONV_DIM // GROUPS
GATE_PER_GROUP = 2 * D_MODEL // GROUPS
G_XBC = 0
G_Z = G_XBC + GROUP_CONV
G_CB = G_Z + GROUP_X
G_CC = G_CB + SHORT_PER_GROUP
G_CX = G_CC + SHORT_PER_GROUP
G_GATE = G_CX + SHORT_PER_GROUP
GROUP_COLS = G_GATE + GATE_PER_GROUP
PROJ_PIECE = 256
PIECES_WHILE_ODD = {-1: (0,), 0: (1,), 1: (2,), 3: (3,), 4: (4,), 7: (5,)}
PIECES_WHILE_EVEN = {0: (6,), 2: (7,), 3: (8,), 4: (9,), 6: (10,)}

N_SPLIT = 3
PART_ROWS = 32
BETA_ROW0 = N_SPLIT * HEADS_PER_GROUP

VMEM_LIMIT = 48 * 1024 * 1024
VMEM_LIMIT_LARGE = 56 * 1024 * 1024


def _cparams(n_axes, limit=VMEM_LIMIT):
    return pltpu.CompilerParams(
        dimension_semantics=("arbitrary",) * n_axes,
        vmem_limit_bytes=limit)


def _dot(a, b):
    return jnp.dot(a, b, preferred_element_type=jnp.float32)


def _dot_t(a, b_t):
    return lax.dot_general(a, b_t, (((1,), (1,)), ((), ())), preferred_element_type=jnp.float32)


def _split3(v):
    hi = v.astype(jnp.bfloat16)
    r1 = v - hi.astype(jnp.float32)
    mid = r1.astype(jnp.bfloat16)
    lo = (r1 - mid.astype(jnp.float32)).astype(jnp.bfloat16)
    return hi, mid, lo


def _silu(v):
    return v / (1.0 + jnp.exp2(v * (-LOG2E)))


def _chunk_time(p):
    return (p % SUBLANES) * SLABS + p // SUBLANES


def _ada_kernel(c_ref, w_ref, b_ref, o_ref):
    c = c_ref[...]
    o_ref[0] = _dot(_silu(c), w_ref[0]) + b_ref[0]


def _ada_call(c_pad, w_ada, b_ada):
    n_out = N_MOD * D_MODEL
    tn = D_MODEL
    return pl.pallas_call(
        _ada_kernel,
        grid=(DEPTH, n_out // tn),
        in_specs=[
            pl.BlockSpec((SUBLANES, D_MODEL), lambda l, j: (0, 0)),
            pl.BlockSpec((1, D_MODEL, tn), lambda l, j: (l, 0, j)),
            pl.BlockSpec((1, 1, tn), lambda l, j: (l, 0, j)),
        ],
        out_specs=pl.BlockSpec((1, SUBLANES, tn), lambda l, j: (l, 0, j)),
        out_shape=jax.ShapeDtypeStruct((DEPTH, SUBLANES, n_out), jnp.float32),
        compiler_params=_cparams(2),
        name="ada_mod",
    )(c_pad, w_ada, b_ada.reshape(DEPTH, 1, n_out))


def _modnorm_to(x_ref, ln_ref, scale, shift, u_ref, rows, chunk=256):
    w = ln_ref[...] * (1.0 + scale)
    for r0 in range(0, rows, chunk):
        xf = x_ref[r0:r0 + chunk, :]
        ms = jnp.mean(xf * xf, axis=-1, keepdims=True)
        y = xf * lax.rsqrt(ms + EPS)
        u_ref[r0:r0 + chunk, :] = (y * w + shift).astype(u_ref.dtype)


def _permute_rows(src_ref, dst_ref, onehot_b, rows):
    for r0 in range(0, rows, SCAN_CHUNK):
        hi, mid, lo = _split3(src_ref[r0:r0 + SCAN_CHUNK, :])
        dst_ref[r0:r0 + SCAN_CHUNK, :] = (_dot(onehot_b, hi) + _dot(onehot_b, mid)
                                          + _dot(onehot_b, lo)).astype(dst_ref.dtype)


def _pre_kernel(*refs, tp, permute_in):
    if permute_in:
        (xnat_ref, mod_ref, ln_ref, wdt_ref, hp_ref, cum_ref, perm_ref, toperm_ref,
         x_ref, u_ref, parts_ref) = refs
        _permute_rows(xnat_ref, x_ref, toperm_ref[...], tp)
    else:
        x_ref, mod_ref, ln_ref, wdt_ref, hp_ref, cum_ref, perm_ref, u_ref, parts_ref = refs
    shift = mod_ref[0, 0:1, :]
    scale = mod_ref[0, 1:2, :]
    _modnorm_to(x_ref, ln_ref, scale, shift, u_ref, tp)
    dt_raw_t = _dot_t(wdt_ref[...], u_ref[...])[0:HEADS, :]
    dt_bias = hp_ref[0]
    a_neg = -jnp.exp(hp_ref[1])
    cum_t = cum_ref[...]
    perm = perm_ref[...]
    q = SCAN_CHUNK
    for c in range(tp // q):
        r0 = c * q
        dtr = dt_raw_t[:, r0:r0 + q] + dt_bias
        dt = jnp.maximum(dtr, 0.0) + jnp.log1p(jnp.exp(-jnp.abs(dtr)))
        a_parts = _split3(dt * a_neg)
        acum = _dot(a_parts[0], cum_t) + _dot(a_parts[1], cum_t) + _dot(a_parts[2], cum_t)
        beta = acum - jnp.log(jnp.maximum(dt, F32_TINY))
        terms = jnp.concatenate(_split3(acum * LOG2E) + _split3(beta * LOG2E), axis=0)
        parts_ref[:, r0:r0 + q] = _dot(perm, terms).astype(parts_ref.dtype)


def _pre_call(x2, mod_l, ln, w_dt, hp, cum, perm, seq, to_perm=None, tp=1024):
    t_total = x2.shape[0]
    tiles_per_batch = seq // tp
    permute_in = to_perm is not None
    square = pl.BlockSpec((SCAN_CHUNK, SCAN_CHUNK), lambda i: (0, 0))
    rows = pl.BlockSpec((tp, D_MODEL), lambda i: (i, 0))
    return pl.pallas_call(
        functools.partial(_pre_kernel, tp=tp, permute_in=permute_in),
        grid=(t_total // tp,),
        in_specs=[
            rows,
            pl.BlockSpec((1, MOD_ROWS, D_MODEL), lambda i: (i // tiles_per_batch, 0, 0)),
            pl.BlockSpec((1, D_MODEL), lambda i: (0, 0)),
            pl.BlockSpec((LANES, D_MODEL), lambda i: (0, 0)),
            pl.BlockSpec((2, HEADS, LANES), lambda i: (0, 0, 0)),
            square,
            pl.BlockSpec((GROUPS * PART_ROWS, 2 * N_SPLIT * HEADS), lambda i: (0, 0)),
        ] + ([square] if permute_in else []),
        out_specs=([rows] if permute_in else []) + [
            rows,
            pl.BlockSpec((GROUPS * PART_ROWS, tp), lambda i: (0, i)),
        ],
        out_shape=([jax.ShapeDtypeStruct((t_total, D_MODEL), jnp.float32)] if permute_in else []) + [
            jax.ShapeDtypeStruct((t_total, D_MODEL), jnp.bfloat16),
            jax.ShapeDtypeStruct((GROUPS * PART_ROWS, t_total), jnp.bfloat16),
        ],
        compiler_params=_cparams(1),
        name="pre",
    )(*((x2, mod_l, ln, w_dt, hp, cum, perm) + ((to_perm,) if permute_in else ())))


def _shifted_taps(cur, tail, width):
    n_wrap = width - 1
    cur_tail = cur[SLABS - n_wrap:]
    wrapped = jnp.concatenate([tail[:, SUBLANES - 1:, :], cur_tail[:, :SUBLANES - 1, :]], axis=1)
    taps = [cur]
    for s in range(1, width):
        taps.append(jnp.concatenate([wrapped[n_wrap - s:], cur[:SLABS - s]], axis=0))
    return taps, cur_tail


def _project_piece(u_ref, w_ref, pa_ref, pb_ref, piece, tb):
    lo = piece * PROJ_PIECE
    res = _dot(u_ref[...], w_ref[0, :, lo:lo + PROJ_PIECE]
               ).reshape(tb // SUBLANES, SUBLANES, PROJ_PIECE)
    n_a = min(max(GROUP_COLS - lo, 0), PROJ_PIECE)
    if n_a:
        pa_ref[:, :, lo:lo + n_a] = res[:, :, 0:n_a]
    if n_a < PROJ_PIECE:
        b0 = lo + n_a - GROUP_COLS
        pb_ref[:, :, b0:b0 + PROJ_PIECE - n_a] = res[:, :, n_a:]


def _mix_group(p_ref, parts_ref, g, first, cw_ref, cbias_ref, scw_ref, dskip_ref, nw_ref, e_t,
               causal, ys_ref, yc_ref, gate_ref, state_ref, tail_ref, stail_ref, tb, after_chunk,
               hist_src, keep_start):
    q = SCAN_CHUNK
    nt_dims = (((1,), (1,)), ((), ()))
    tn_dims = (((0,), (0,)), ((), ()))
    cw = cw_ref[g]
    cbias = cbias_ref[g]
    scw = scw_ref[g]
    dskip = dskip_ref[g]
    nw = nw_ref[g]

    gate_ref[...] = p_ref[:, :, G_GATE:GROUP_COLS].reshape(tb, GATE_PER_GROUP).astype(gate_ref.dtype)

    state_in = jnp.where(first, 0.0, state_ref[hist_src])
    tail = jnp.where(first, 0.0, tail_ref[hist_src])
    stail = jnp.where(first, 0.0, stail_ref[hist_src])
    if keep_start:
        state_ref[GROUPS] = state_in
        tail_ref[GROUPS] = tail
        stail_ref[GROUPS] = stail
    state_ref[g] = state_in

    after_chunk(-1)
    for c in range(tb // q):
        r0 = c * q
        b0 = c * SLABS
        sv = p_ref[b0:b0 + SLABS, :, G_CC:G_CX] * p_ref[b0:b0 + SLABS, :, G_CX:G_GATE]
        staps, stail = _shifted_taps(sv, stail, CONV_WIDTH)
        sconv = scw[CONV_WIDTH - 1:CONV_WIDTH, :] * staps[0]
        for s in range(1, CONV_WIDTH):
            sconv += scw[CONV_WIDTH - 1 - s:CONV_WIDTH - s, :] * staps[s]
        yc = p_ref[b0:b0 + SLABS, :, G_CB:G_CC] * sconv
        yc_ref[r0:r0 + q, :] = yc.reshape(q, SHORT_PER_GROUP).astype(yc_ref.dtype)

        taps, tail = _shifted_taps(p_ref[b0:b0 + SLABS, :, G_XBC:G_Z], tail, SSM_CONV_WIDTH)
        acc = cbias + cw[SSM_CONV_WIDTH - 1:SSM_CONV_WIDTH, :] * taps[0]
        for s in range(1, SSM_CONV_WIDTH):
            acc += cw[SSM_CONV_WIDTH - 1 - s:SSM_CONV_WIDTH - s, :] * taps[s]
        xbc = _silu(acc).reshape(q, GROUP_CONV)
        xs = xbc[:, 0:GROUP_X]
        xt_b = xs.astype(jnp.bfloat16).T
        b_b = xbc[:, GROUP_X:GROUP_X + STATE].astype(jnp.bfloat16)
        c_b = xbc[:, GROUP_X + STATE:GROUP_CONV].astype(jnp.bfloat16)

        mc = parts_ref[:, r0:r0 + q]
        acum_l = lax.dot_general(mc, e_t, tn_dims, preferred_element_type=jnp.float32)
        acum_last = acum_l[q - 1:q, :]
        mf = mc.astype(jnp.float32)

        scores_b = lax.dot_general(c_b, b_b, nt_dims, preferred_element_type=jnp.float32
                                   ).astype(jnp.bfloat16)
        ys = []
        for r in range(HEADS_PER_GROUP):
            al = acum_l[:, r * LANES:(r + 1) * LANES]
            last = acum_last[:, r * LANES:(r + 1) * LANES]
            beta_row = (mf[BETA_ROW0 + r:BETA_ROW0 + r + 1, :]
                        + mf[BETA_ROW0 + HEADS_PER_GROUP + r:BETA_ROW0 + HEADS_PER_GROUP + r + 1, :]
                        + mf[BETA_ROW0 + 2 * HEADS_PER_GROUP + r:
                             BETA_ROW0 + 2 * HEADS_PER_GROUP + r + 1, :])
            seg = (al - beta_row).astype(jnp.bfloat16)
            w_r = scores_b * jnp.exp2(jnp.where(causal, seg, -jnp.inf))
            c_r = c_b * jnp.exp2(al).astype(jnp.bfloat16)
            xt_r = xt_b[r * HEAD_DIM:(r + 1) * HEAD_DIM, :]
            xw_r = xt_r * jnp.exp2(last - beta_row).astype(jnp.bfloat16)
            s_r = state_ref[g, r]
            ys.append(lax.dot_general(
                jnp.concatenate([w_r, c_r], axis=1),
                jnp.concatenate([xt_r, s_r.astype(jnp.bfloat16)], axis=1),
                nt_dims, preferred_element_type=jnp.float32))
            state_ref[g, r] = s_r * jnp.exp2(last) + _dot(xw_r, b_b)
        y = jnp.concatenate(ys, axis=-1)

        y = y + dskip * xs
        y = y * _silu(p_ref[b0:b0 + SLABS, :, G_Z:G_CB].reshape(q, GROUP_X))
        y = y * lax.rsqrt(jnp.mean(y * y, axis=-1, keepdims=True) + EPS)
        ys_ref[r0:r0 + q, :] = (y * nw).astype(ys_ref.dtype)
        after_chunk(c)

    tail_ref[g] = tail
    stail_ref[g] = stail


def _mixer_kernel(u_ref, w_ref, parts_a_ref, parts_b_ref,
                  cw_ref, cbias_ref, scw_ref, dskip_ref, nw_ref, et_ref,
                  ys_a_ref, yc_a_ref, gate_a_ref, ys_b_ref, yc_b_ref, gate_b_ref,
                  pa_ref, pb_ref, state_ref, tail_ref, stail_ref, *, tb, n_items, nt):
    s = pl.program_id(0)
    q = SCAN_CHUNK

    @pl.when(s == 0)
    def _():
        pb_ref[...] = jnp.zeros_like(pb_ref)
        state_ref[...] = jnp.zeros_like(state_ref)
        tail_ref[...] = jnp.zeros_like(tail_ref)
        stail_ref[...] = jnp.zeros_like(stail_ref)

    item_a = jnp.minimum(s, n_items - 1)
    item_b = jnp.maximum(s - 1, 0)
    g_a = 2 * (item_a % PAIRS)
    g_b = 2 * (item_b % PAIRS) + 1
    first_a = (item_a // PAIRS) % nt == 0
    first_b = (item_b // PAIRS) % nt == 0

    e_t = et_ref[...]
    t_row = _chunk_time(lax.broadcasted_iota(jnp.int32, (q, q), 0))
    t_col = _chunk_time(lax.broadcasted_iota(jnp.int32, (q, q), 1))
    causal = t_row >= t_col
    common = (cw_ref, cbias_ref, scw_ref, dskip_ref, nw_ref, e_t, causal)
    hist = (state_ref, tail_ref, stail_ref, tb)

    def project_after(schedule):
        def after_chunk(c):
            for piece in schedule.get(c, ()):
                _project_piece(u_ref, w_ref, pa_ref, pb_ref, piece, tb)
        return after_chunk

    _mix_group(pb_ref, parts_b_ref, g_b, first_b, *common, ys_b_ref, yc_b_ref, gate_b_ref, *hist,
               project_after(PIECES_WHILE_ODD), g_b, False)
    drain = s == n_items
    _mix_group(pa_ref, parts_a_ref, g_a, first_a, *common, ys_a_ref, yc_a_ref, gate_a_ref, *hist,
               project_after(PIECES_WHILE_EVEN), jnp.where(drain, GROUPS, g_a), True)


def _mixer_call(u, w_pairs, layer, parts, cw, cbias, scw, dskip_e, nw, e_t, bsz, seq, tb=1024):
    t_total = u.shape[0]
    nt = seq // tb
    n_blocks = t_total // tb
    n_items = n_blocks * PAIRS
    item_a = lambda s: jnp.minimum(s, n_items - 1)
    item_b = lambda s: jnp.maximum(s - 1, 0)
    out_a = lambda s: (item_a(s) // PAIRS, item_a(s) % PAIRS)
    out_b = lambda s: (item_b(s) // PAIRS, item_b(s) % PAIRS)

    def full(arr):
        nd = arr.ndim
        return pl.BlockSpec(arr.shape, lambda s: (0,) * nd)

    half = GROUPS // 2
    out_shapes = [
        jax.ShapeDtypeStruct((t_total, half *GROUP_X), jnp.bfloat16),
        jax.ShapeDtypeStruct((t_total, half *SHORT_PER_GROUP), jnp.bfloat16),
        jax.ShapeDtypeStruct((t_total, half *GATE_PER_GROUP), jnp.bfloat16),
    ]
    widths = (GROUP_X, SHORT_PER_GROUP, GATE_PER_GROUP)
    return pl.pallas_call(
        functools.partial(_mixer_kernel, tb=tb, n_items=n_items, nt=nt),
        grid=(n_items + 1,),
        in_specs=[
            pl.BlockSpec((tb, D_MODEL), lambda s: (item_a(s) // PAIRS, 0)),
            pl.BlockSpec((1, D_MODEL, 2 * GROUP_COLS), lambda s: (layer, 0, item_a(s) % PAIRS)),
            pl.BlockSpec((PART_ROWS, tb), lambda s: (2 * (item_a(s) % PAIRS), item_a(s) // PAIRS)),
            pl.BlockSpec((PART_ROWS, tb),
                         lambda s: (2 * (item_b(s) % PAIRS) + 1, item_b(s) // PAIRS)),
            full(cw), full(cbias), full(scw), full(dskip_e), full(nw), full(e_t),
        ],
        out_specs=([pl.BlockSpec((tb, w), out_a) for w in widths]
                   + [pl.BlockSpec((tb, w), out_b) for w in widths]),
        out_shape=out_shapes + out_shapes,
        scratch_shapes=[
            pltpu.VMEM((tb // SUBLANES, SUBLANES, GROUP_COLS), jnp.float32),
            pltpu.VMEM((tb // SUBLANES, SUBLANES, GROUP_COLS), jnp.float32),
            pltpu.VMEM((GROUPS + 1, HEADS_PER_GROUP, HEAD_DIM, STATE), jnp.float32),
            pltpu.VMEM((GROUPS + 1, SSM_CONV_WIDTH - 1, SUBLANES, GROUP_CONV), jnp.float32),
            pltpu.VMEM((GROUPS + 1, CONV_WIDTH - 1, SUBLANES, SHORT_PER_GROUP), jnp.float32),
        ],
        compiler_params=_cparams(1, VMEM_LIMIT_LARGE),
        name="mixer",
    )(u, w_pairs, parts, parts, cw, cbias, scw, dskip_e, nw, e_t)


def _out_kernel(yca_ref, ycb_ref, ysa_ref, ysb_ref, ga_ref, gb_ref, x_ref, mod_ref,
                wc_ref, ws_ref, wo_ref, o_ref):
    p_conv = _dot(jnp.concatenate([yca_ref[...], ycb_ref[...]], axis=1), wc_ref[0])
    y_ssm = jnp.concatenate(
        [ref[:, i * GROUP_X:(i + 1) * GROUP_X] for i in range(PAIRS) for ref in (ysa_ref, ysb_ref)],
        axis=1)
    p_ssm = _dot(y_ssm, ws_ref[0])
    g_conv = 1.0 / (1.0 + jnp.exp(-ga_ref[...].astype(jnp.float32)))
    g_ssm = 1.0 / (1.0 + jnp.exp(-gb_ref[...].astype(jnp.float32)))
    merged = g_conv * p_conv + g_ssm * p_ssm
    mix = _dot(merged.astype(jnp.bfloat16), wo_ref[0])
    gate1 = mod_ref[0, 2:3, :]
    o_ref[...] = x_ref[...] + gate1 * mix


def _out_call(mixed, x2, mod_l, wc, ws, wo, layer, seq, tm=512):
    ys_a, yc_a, g_a, ys_b, yc_b, g_b = mixed
    t_total = x2.shape[0]
    tiles_per_batch = seq // tm
    row = lambda width: pl.BlockSpec((tm, width), lambda i: (i, 0))
    const = lambda shape: pl.BlockSpec((1,) + shape, lambda i: (layer, 0, 0))
    return pl.pallas_call(
        _out_kernel,
        grid=(t_total // tm,),
        in_specs=[
            row(CONV_DIM // 2), row(CONV_DIM // 2), row(D_SSM // 2), row(D_SSM // 2),
            row(D_MODEL), row(D_MODEL), row(D_MODEL),
            pl.BlockSpec((1, MOD_ROWS, D_MODEL), lambda i: (i // tiles_per_batch, 0, 0)),
            const((CONV_DIM, D_MODEL)), const((D_SSM, D_MODEL)), const((D_MODEL, D_MODEL)),
        ],
        out_specs=pl.BlockSpec((tm, D_MODEL), lambda i: (i, 0)),
        out_shape=jax.ShapeDtypeStruct((t_total, D_MODEL), jnp.float32),
        compiler_params=_cparams(1),
        name="mix_out",
    )(yc_a, yc_b, ys_a, ys_b, g_a, g_b, x2, mod_l, wc, ws, wo)


MLP_NORM_ROWS = 128


def _mlp_kernel(x_ref, xnext_ref, mod_ref, modnext_ref, ln_ref, wup_ref, wdn_ref, fn_ref,
                fromperm_ref, o_ref, ua_ref, ub_ref, *, tm, tf, final):
    ln = ln_ref[...]

    def norm_rows(src_ref, src0, mod, u_ref, r0):
        w = ln * (1.0 + mod[0, 4:5, :])
        xf = src_ref[src0 + r0:src0 + r0 + MLP_NORM_ROWS, :]
        y = xf * lax.rsqrt(jnp.mean(xf * xf, axis=-1, keepdims=True) + EPS)
        u_ref[r0:r0 + MLP_NORM_ROWS, :] = (y * w + mod[0, 3:4, :]).astype(u_ref.dtype)

    @pl.when(pl.program_id(0) == 0)
    def _():
        for r0 in range(0, tm, MLP_NORM_ROWS):
            norm_rows(x_ref, 0, mod_ref, ua_ref, r0)

    def mlp_tile(u_ref, row0, side_work):
        acc = None
        for j in range(D_FF // tf):
            h = jnp.maximum(_dot(u_ref[...], wup_ref[0, :, j * tf:(j + 1) * tf]), 0.0)
            side_work()
            d = _dot((h * h).astype(jnp.bfloat16), wdn_ref[0, j * tf:(j + 1) * tf, :])
            side_work()
            acc = d if acc is None else acc + d
        xn = x_ref[row0:row0 + tm, :] + mod_ref[0, 5:6, :] * acc
        if final:
            ms = jnp.mean(xn * xn, axis=-1, keepdims=True)
            xf = xn * lax.rsqrt(ms + EPS) * fn_ref[...]
            onehot = fromperm_ref[...]
            for c0 in range(0, tm, SCAN_CHUNK):
                hi, mid, lo = _split3(xf[c0:c0 + SCAN_CHUNK, :])
                o_ref[row0 + c0:row0 + c0 + SCAN_CHUNK, :] = (
                    _dot(onehot, hi) + _dot(onehot, mid) + _dot(onehot, lo))
        else:
            o_ref[row0:row0 + tm, :] = xn

    def norm_in_pieces(src_ref, src0, mod, u_ref):
        todo = list(range(0, tm, MLP_NORM_ROWS))
        per_call = -(-len(todo) // (2 * (D_FF // tf)))

        def side_work():
            for _ in range(per_call):
                if todo:
                    norm_rows(src_ref, src0, mod, u_ref, todo.pop(0))
        return side_work

    mlp_tile(ua_ref, 0, norm_in_pieces(x_ref, tm, mod_ref, ub_ref))
    mlp_tile(ub_ref, tm, norm_in_pieces(xnext_ref, 0, modnext_ref, ua_ref))


def _mlp_call(x2, mod_l, ln, w_up, w_dn, layer, fnorm, from_perm, seq, final, tm=512, tf=2048):
    t_total = x2.shape[0]
    n_blocks = t_total // (2 * tm)
    blocks_per_batch = seq // (2 * tm)
    nxt = lambda i: jnp.minimum(i + 1, n_blocks - 1)
    resident = dict(pipeline_mode=pl.Buffered(1))
    return pl.pallas_call(
        functools.partial(_mlp_kernel, tm=tm, tf=tf, final=final),
        grid=(n_blocks,),
        in_specs=[
            pl.BlockSpec((2 * tm, D_MODEL), lambda i: (i, 0)),
            pl.BlockSpec((tm, D_MODEL), lambda i: (2 * nxt(i), 0)),
            pl.BlockSpec((1, MOD_ROWS, D_MODEL), lambda i: (i // blocks_per_batch, 0, 0)),
            pl.BlockSpec((1, MOD_ROWS, D_MODEL), lambda i: (nxt(i) // blocks_per_batch, 0, 0)),
            pl.BlockSpec((1, D_MODEL), lambda i: (0, 0)),
            pl.BlockSpec((1, D_MODEL, D_FF), lambda i: (layer, 0, 0), **resident),
            pl.BlockSpec((1, D_FF, D_MODEL), lambda i: (layer, 0, 0), **resident),
            pl.BlockSpec((1, D_MODEL), lambda i: (0, 0)),
            pl.BlockSpec((SCAN_CHUNK, SCAN_CHUNK), lambda i: (0, 0)),
        ],
        out_specs=pl.BlockSpec((2 * tm, D_MODEL), lambda i: (i, 0)),
        out_shape=jax.ShapeDtypeStruct((t_total, D_MODEL), jnp.float32),
        scratch_shapes=[pltpu.VMEM((tm, D_MODEL), jnp.bfloat16),
                        pltpu.VMEM((tm, D_MODEL), jnp.bfloat16)],
        compiler_params=_cparams(1, VMEM_LIMIT_LARGE),
        name="mlp",
    )(x2, x2, mod_l, mod_l, ln, w_up, w_dn, fnorm, from_perm)


def _head_constants():
    e_t = np.zeros((PART_ROWS, HEADS_PER_GROUP, LANES), np.float32)
    perm = np.zeros((GROUPS, PART_ROWS, 2 * N_SPLIT, HEADS), np.float32)
    for k in range(2 * N_SPLIT):
        for r in range(HEADS_PER_GROUP):
            if k < N_SPLIT:
                e_t[k * HEADS_PER_GROUP + r, r, :] = 1.0
            for g in range(GROUPS):
                perm[g, k * HEADS_PER_GROUP + r, k, g * HEADS_PER_GROUP + r] = 1.0
    e_t = e_t.reshape(PART_ROWS, HEADS_PER_GROUP * LANES)
    perm = perm.reshape(GROUPS * PART_ROWS, 2 * N_SPLIT * HEADS)
    tm = _chunk_time(np.arange(SCAN_CHUNK))
    cum = (tm[:, None] <= tm[None, :]).astype(np.float32)
    to_perm = (np.arange(SCAN_CHUNK)[None, :] == tm[:, None]).astype(np.float32)
    bf = jnp.bfloat16
    return (jnp.asarray(e_t, bf), jnp.asarray(perm, bf), jnp.asarray(cum, bf),
            jnp.asarray(to_perm, bf), jnp.asarray(to_perm.T, bf))


def _regroup_kernel(xs_ref, bs_ref, cs_ref, z_ref, cb_ref, cc_ref, cx_ref, gate_ref, dt_ref,
                    o_ref, odt_ref):
    for ref, lo, hi in ((xs_ref, G_XBC, G_XBC + GROUP_X), (bs_ref, G_XBC + GROUP_X, G_Z - STATE),
                        (cs_ref, G_Z - STATE, G_Z), (z_ref, G_Z, G_CB), (cb_ref, G_CB, G_CC),
                        (cc_ref, G_CC, G_CX), (cx_ref, G_CX, G_GATE), (gate_ref, G_GATE, GROUP_COLS)):
        o_ref[0, :, lo:hi] = ref[0].T.astype(o_ref.dtype)

    @pl.when(pl.program_id(1) == 0)
    def _():
        row = lax.broadcasted_iota(jnp.int32, (LANES, D_MODEL), 0)
        odt_ref[0] = jnp.where(row < HEADS, dt_ref[0], 0.0).astype(odt_ref.dtype)


def _regroup_w_in(w_t):
    def rows(height, off, paired):
        base = off // height
        if paired:
            return pl.BlockSpec((1, height, D_MODEL),
                                lambda l, g: (l, base + (g % 2) * PAIRS + g // 2, 0))
        return pl.BlockSpec((1, height, D_MODEL), lambda l, g: (l, base + g, 0))

    return pl.pallas_call(
        _regroup_kernel,
        grid=(DEPTH, GROUPS),
        in_specs=[
            rows(GROUP_X, OFF_XS, False), rows(STATE, OFF_BS, False), rows(STATE, OFF_CS, False),
            rows(GROUP_X, OFF_Z, False),
            rows(SHORT_PER_GROUP, OFF_CB, True), rows(SHORT_PER_GROUP, OFF_CC, True),
            rows(SHORT_PER_GROUP, OFF_CX, True), rows(GATE_PER_GROUP, 0, True),
            pl.BlockSpec((1, LANES, D_MODEL), lambda l, g: (l, OFF_DT // LANES, 0)),
        ],
        out_specs=[
            pl.BlockSpec((1, D_MODEL, GROUP_COLS), lambda l, g: (l, 0, g)),
            pl.BlockSpec((1, LANES, D_MODEL), lambda l, g: (l, 0, 0)),
        ],
        out_shape=[
            jax.ShapeDtypeStruct((DEPTH, D_MODEL, GROUPS * GROUP_COLS), jnp.bfloat16),
            jax.ShapeDtypeStruct((DEPTH, LANES, D_MODEL), jnp.bfloat16),
        ],
        compiler_params=_cparams(2),
        name="w_in_regroup",
    )(*([w_t] * 9))


def _group_conv_params(p):
    k = p.shape[1]
    xs = p[..., :D_SSM].reshape(DEPTH, k, GROUPS, GROUP_X)
    bs = p[..., D_SSM:D_SSM + GROUPS * STATE].reshape(DEPTH, k, GROUPS, STATE)
    cs = p[..., D_SSM + GROUPS * STATE:].reshape(DEPTH, k, GROUPS, STATE)
    return jnp.swapaxes(jnp.concatenate([xs, bs, cs], axis=-1), 1, 2)


def kernel(x, c, w_ada, b_ada, ln1, ln2, w_in, conv_w, ssm_conv_w, ssm_conv_b, dt_bias, a_log,
           d_skip, ssm_norm_w, w_conv_out, w_ssm_out, w_o, w_up, w_down, final_norm):
    bsz, seq, _ = x.shape
    bf = jnp.bfloat16
    f32 = jnp.float32

    c_pad = jnp.zeros((SUBLANES, D_MODEL), f32).at[:bsz].set(c)
    mod_all = _ada_call(c_pad, w_ada, b_ada)
    mod_all = mod_all[:, :bsz].reshape(DEPTH, bsz, N_MOD, D_MODEL)
    mod_all = jnp.pad(mod_all, ((0, 0), (0, 0), (0, MOD_ROWS - N_MOD), (0, 0)))

    w_pairs, w_dt = _regroup_w_in(jnp.swapaxes(w_in, 1, 2))
    hp = jnp.broadcast_to(jnp.stack([dt_bias, a_log], axis=1)[..., None],
                          (DEPTH, 2, HEADS, LANES))
    cw = _group_conv_params(ssm_conv_w)
    cbias = _group_conv_params(ssm_conv_b[:, None, :])
    scw = conv_w.reshape(DEPTH, CONV_WIDTH, 2, PAIRS, SHORT_PER_GROUP)
    scw = jnp.transpose(scw, (0, 3, 2, 1, 4)).reshape(DEPTH, GROUPS, CONV_WIDTH, SHORT_PER_GROUP)
    dskip_e = jnp.repeat(d_skip, HEAD_DIM, axis=-1).reshape(DEPTH, GROUPS, 1, GROUP_X)
    nw = ssm_norm_w.reshape(DEPTH, GROUPS, 1, GROUP_X)
    ws = w_ssm_out.astype(bf)
    wc = w_conv_out.astype(bf)
    wo = w_o.astype(bf)
    wup = w_up.astype(bf)
    wdn = w_down.astype(bf)

    e_t, perm, cum, to_perm, from_perm = _head_constants()
    fnorm = final_norm.reshape(1, D_MODEL)
    x2 = x.reshape(bsz * seq, D_MODEL)

    for l in range(DEPTH):
        mod_l = mod_all[l]
        pre_args = (mod_l, ln1[l].reshape(1, D_MODEL), w_dt[l], hp[l], cum, perm, seq)
        if l == 0:
            x2, u, parts = _pre_call(x2, *pre_args, to_perm=to_perm)
        else:
            u, parts = _pre_call(x2, *pre_args)
        mixed = _mixer_call(u, w_pairs, l, parts, cw[l], cbias[l], scw[l], dskip_e[l], nw[l], e_t,
                            bsz, seq)
        x2 = _out_call(mixed, x2, mod_l, wc, ws, wo, l, seq)
        x2 = _mlp_call(x2, mod_l, ln2[l].reshape(1, D_MODEL), wup, wdn, l, fnorm, from_perm, seq,
                       final=(l == DEPTH - 1))
    return x2.reshape(bsz, seq, D_MODEL)
```

```python
import functools

import numpy as np
import jax
import jax.numpy as jnp
from jax import lax
from jax.experimental import pallas as pl
from jax.experimental.pallas import tpu as pltpu

D_MODEL = 1024
DEPTH = 4
EPS = 1e-6
N_MOD = 6
MOD_ROWS = 8
CONV_DIM = D_MODEL
CONV_WIDTH = 3
D_SSM = 2 * D_MODEL
HEAD_DIM = 64
HEADS = D_SSM // HEAD_DIM
GROUPS = 8
PAIRS = GROUPS // 2
HEADS_PER_GROUP = HEADS // GROUPS
GROUP_X = HEADS_PER_GROUP * HEAD_DIM
STATE = 128
SSM_CONV_WIDTH = 4
D_FF = 4 * D_MODEL
LANES = 128
SUBLANES = 8
F32_TINY = float(np.finfo(np.float32).tiny)
LOG2E = float(np.log2(np.e))

OFF_CB = 2 * D_MODEL
OFF_CC = OFF_CB + CONV_DIM
OFF_CX = OFF_CC + CONV_DIM
OFF_Z = OFF_CX + CONV_DIM
OFF_XS = OFF_Z + D_SSM
OFF_BS = OFF_XS + D_SSM
OFF_CS = OFF_BS + GROUPS * STATE
OFF_DT = OFF_CS + GROUPS * STATE

SCAN_CHUNK = 128
SLABS = SCAN_CHUNK // SUBLANES
GROUP_CONV = GROUP_X + 2 * STATE
SHORT_PER_GROUP = CONV_DIM // GROUPS
GATE_PER_GROUP = 2 * D_MODEL // GROUPS
G_XBC = 0
G_Z = G_XBC + GROUP_CONV
G_CB = G_Z + GROUP_X
G_CC = G_CB + SHORT_PER_GROUP
G_CX = G_CC + SHORT_PER_GROUP
G_GATE = G_CX + SHORT_PER_GROUP
GROUP_COLS = G_GATE + GATE_PER_GROUP
PROJ_PIECE = 256
PIECES_WHILE_ODD = {-1: (0,), 0: (1,), 1: (2,), 3: (3,), 4: (4,), 7: (5,)}
PIECES_WHILE_EVEN = {0: (6,), 2: (7,), 3: (8,), 4: (9,), 6: (10,)}

N_SPLIT = 3
PART_ROWS = 32
BETA_ROW0 = N_SPLIT * HEADS_PER_GROUP

VMEM_LIMIT = 48 * 1024 * 1024
VMEM_LIMIT_LARGE = 56 * 1024 * 1024


def _cparams(n_axes, limit=VMEM_LIMIT):
    return pltpu.CompilerParams(
        dimension_semantics=("arbitrary",) * n_axes,
        vmem_limit_bytes=limit)


def _dot(a, b):
    return jnp.dot(a, b, preferred_element_type=jnp.float32)


def _dot_t(a, b_t):
    return lax.dot_general(a, b_t, (((1,), (1,)), ((), ())), preferred_element_type=jnp.float32)


def _split3(v):
    hi = v.astype(jnp.bfloat16)
    r1 = v - hi.astype(jnp.float32)
    mid = r1.astype(jnp.bfloat16)
    lo = (r1 - mid.astype(jnp.float32)).astype(jnp.bfloat16)
    return hi, mid, lo


def _silu(v):
    return v / (1.0 + jnp.exp2(v * (-LOG2E)))


def _chunk_time(p):
    return (p % SUBLANES) * SLABS + p // SUBLANES


def _ada_kernel(c_ref, w_ref, b_ref, o_ref):
    c = c_ref[...]
    o_ref[0] = _dot(_silu(c), w_ref[0]) + b_ref[0]


def _ada_call(c_pad, w_ada, b_ada):
    n_out = N_MOD * D_MODEL
    tn = D_MODEL
    return pl.pallas_call(
        _ada_kernel,
        grid=(DEPTH, n_out // tn),
        in_specs=[
            pl.BlockSpec((SUBLANES, D_MODEL), lambda l, j: (0, 0)),
            pl.BlockSpec((1, D_MODEL, tn), lambda l, j: (l, 0, j)),
            pl.BlockSpec((1, 1, tn), lambda l, j: (l, 0, j)),
        ],
        out_specs=pl.BlockSpec((1, SUBLANES, tn), lambda l, j: (l, 0, j)),
        out_shape=jax.ShapeDtypeStruct((DEPTH, SUBLANES, n_out), jnp.float32),
        compiler_params=_cparams(2),
        name="ada_mod",
    )(c_pad, w_ada, b_ada.reshape(DEPTH, 1, n_out))


def _modnorm_to(x_ref, ln_ref, scale, shift, u_ref, rows, chunk=256):
    w = ln_ref[...] * (1.0 + scale)
    for r0 in range(0, rows, chunk):
        xf = x_ref[r0:r0 + chunk, :]
        ms = jnp.mean(xf * xf, axis=-1, keepdims=True)
        y = xf * lax.rsqrt(ms + EPS)
        u_ref[r0:r0 + chunk, :] = (y * w + shift).astype(u_ref.dtype)


def _permute_rows(src_ref, dst_ref, onehot_b, rows):
    for r0 in range(0, rows, SCAN_CHUNK):
        hi, mid, lo = _split3(src_ref[r0:r0 + SCAN_CHUNK, :])
        dst_ref[r0:r0 + SCAN_CHUNK, :] = (_dot(onehot_b, hi) + _dot(onehot_b, mid)
                                          + _dot(onehot_b, lo)).astype(dst_ref.dtype)


def _pre_kernel(*refs, tp, permute_in):
    if permute_in:
        (xnat_ref, mod_ref, ln_ref, wdt_ref, hp_ref, cum_ref, perm_ref, toperm_ref,
         x_ref, u_ref, parts_ref) = refs
        _permute_rows(xnat_ref, x_ref, toperm_ref[...], tp)
    else:
        x_ref, mod_ref, ln_ref, wdt_ref, hp_ref, cum_ref, perm_ref, u_ref, parts_ref = refs
    shift = mod_ref[0, 0:1, :]
    scale = mod_ref[0, 1:2, :]
    _modnorm_to(x_ref, ln_ref, scale, shift, u_ref, tp)
    _dt_parts(u_ref[...], wdt_ref, hp_ref, cum_ref, perm_ref, parts_ref, 0)


def _dt_parts(u_rows, wdt_ref, hp_ref, cum_ref, perm_ref, parts_ref, col0):
    dt_raw_t = _dot_t(wdt_ref[...], u_rows)[0:HEADS, :]
    dt_bias = hp_ref[0]
    a_neg = -jnp.exp(hp_ref[1])
    cum_t = cum_ref[...]
    perm = perm_ref[...]
    q = SCAN_CHUNK
    for c in range(u_rows.shape[0] // q):
        r0 = c * q
        dtr = dt_raw_t[:, r0:r0 + q] + dt_bias
        dt = jnp.maximum(dtr, 0.0) + jnp.log1p(jnp.exp(-jnp.abs(dtr)))
        a_parts = _split3(dt * a_neg)
        acum = _dot(a_parts[0], cum_t) + _dot(a_parts[1], cum_t) + _dot(a_parts[2], cum_t)
        beta = acum - jnp.log(jnp.maximum(dt, F32_TINY))
        terms = jnp.concatenate(_split3(acum * LOG2E) + _split3(beta * LOG2E), axis=0)
        parts_ref[:, col0 + r0:col0 + r0 + q] = _dot(perm, terms).astype(parts_ref.dtype)


def _pre_call(x2, mod_l, ln, w_dt, hp, cum, perm, seq, to_perm=None, tp=1024):
    t_total = x2.shape[0]
    tiles_per_batch = seq // tp
    permute_in = to_perm is not None
    square = pl.BlockSpec((SCAN_CHUNK, SCAN_CHUNK), lambda i: (0, 0))
    rows = pl.BlockSpec((tp, D_MODEL), lambda i: (i, 0))
    return pl.pallas_call(
        functools.partial(_pre_kernel, tp=tp, permute_in=permute_in),
        grid=(t_total // tp,),
        in_specs=[
            rows,
            pl.BlockSpec((1, MOD_ROWS, D_MODEL), lambda i: (i // tiles_per_batch, 0, 0)),
            pl.BlockSpec((1, D_MODEL), lambda i: (0, 0)),
            pl.BlockSpec((LANES, D_MODEL), lambda i: (0, 0)),
            pl.BlockSpec((2, HEADS, LANES), lambda i: (0, 0, 0)),
            square,
            pl.BlockSpec((GROUPS * PART_ROWS, 2 * N_SPLIT * HEADS), lambda i: (0, 0)),
        ] + ([square] if permute_in else []),
        out_specs=([rows] if permute_in else []) + [
            rows,
            pl.BlockSpec((GROUPS * PART_ROWS, tp), lambda i: (0, i)),
        ],
        out_shape=([jax.ShapeDtypeStruct((t_total, D_MODEL), jnp.float32)] if permute_in else []) + [
            jax.ShapeDtypeStruct((t_total, D_MODEL), jnp.bfloat16),
            jax.ShapeDtypeStruct((GROUPS * PART_ROWS, t_total), jnp.bfloat16),
        ],
        compiler_params=_cparams(1),
        name="pre",
    )(*((x2, mod_l, ln, w_dt, hp, cum, perm) + ((to_perm,) if permute_in else ())))


def _shifted_taps(cur, tail, width):
    n_wrap = width - 1
    cur_tail = cur[SLABS - n_wrap:]
    wrapped = jnp.concatenate([tail[:, SUBLANES - 1:, :], cur_tail[:, :SUBLANES - 1, :]], axis=1)
    taps = [cur]
    for s in range(1, width):
        taps.append(jnp.concatenate([wrapped[n_wrap - s:], cur[:SLABS - s]], axis=0))
    return taps, cur_tail


def _project_piece(u_ref, w_ref, pa_ref, pb_ref, piece, tb):
    lo = piece * PROJ_PIECE
    res = _dot(u_ref[...], w_ref[0, :, lo:lo + PROJ_PIECE]
               ).reshape(tb // SUBLANES, SUBLANES, PROJ_PIECE)
    n_a = min(max(GROUP_COLS - lo, 0), PROJ_PIECE)
    if n_a:
        pa_ref[:, :, lo:lo + n_a] = res[:, :, 0:n_a]
    if n_a < PROJ_PIECE:
        b0 = lo + n_a - GROUP_COLS
        pb_ref[:, :, b0:b0 + PROJ_PIECE - n_a] = res[:, :, n_a:]


def _mix_group(p_ref, parts_ref, g, first, cw_ref, cbias_ref, scw_ref, dskip_ref, nw_ref, e_t,
               causal, ys_ref, yc_ref, gate_ref, state_ref, tail_ref, stail_ref, tb, after_chunk,
               hist_src, keep_start):
    q = SCAN_CHUNK
    nt_dims = (((1,), (1,)), ((), ()))
    tn_dims = (((0,), (0,)), ((), ()))
    cw = cw_ref[g]
    cbias = cbias_ref[g]
    scw = scw_ref[g]
    dskip = dskip_ref[g]
    nw = nw_ref[g]

    gate_ref[...] = p_ref[:, :, G_GATE:GROUP_COLS].reshape(tb, GATE_PER_GROUP).astype(gate_ref.dtype)

    state_in = jnp.where(first, 0.0, state_ref[hist_src])
    tail = jnp.where(first, 0.0, tail_ref[hist_src])
    stail = jnp.where(first, 0.0, stail_ref[hist_src])
    if keep_start:
        state_ref[GROUPS] = state_in
        tail_ref[GROUPS] = tail
        stail_ref[GROUPS] = stail
    state_ref[g] = state_in

    after_chunk(-1)
    for c in range(tb // q):
        r0 = c * q
        b0 = c * SLABS
        sv = p_ref[b0:b0 + SLABS, :, G_CC:G_CX] * p_ref[b0:b0 + SLABS, :, G_CX:G_GATE]
        staps, stail = _shifted_taps(sv, stail, CONV_WIDTH)
        sconv = scw[CONV_WIDTH - 1:CONV_WIDTH, :] * staps[0]
        for s in range(1, CONV_WIDTH):
            sconv += scw[CONV_WIDTH - 1 - s:CONV_WIDTH - s, :] * staps[s]
        yc = p_ref[b0:b0 + SLABS, :, G_CB:G_CC] * sconv
        yc_ref[r0:r0 + q, :] = yc.reshape(q, SHORT_PER_GROUP).astype(yc_ref.dtype)

        taps, tail = _shifted_taps(p_ref[b0:b0 + SLABS, :, G_XBC:G_Z], tail, SSM_CONV_WIDTH)
        acc = cbias + cw[SSM_CONV_WIDTH - 1:SSM_CONV_WIDTH, :] * taps[0]
        for s in range(1, SSM_CONV_WIDTH):
            acc += cw[SSM_CONV_WIDTH - 1 - s:SSM_CONV_WIDTH - s, :] * taps[s]
        xbc = _silu(acc).reshape(q, GROUP_CONV)
        xs = xbc[:, 0:GROUP_X]
        xt_b = xs.astype(jnp.bfloat16).T
        b_b = xbc[:, GROUP_X:GROUP_X + STATE].astype(jnp.bfloat16)
        c_b = xbc[:, GROUP_X + STATE:GROUP_CONV].astype(jnp.bfloat16)

        mc = parts_ref[:, r0:r0 + q]
        acum_l = lax.dot_general(mc, e_t, tn_dims, preferred_element_type=jnp.float32)
        acum_last = acum_l[q - 1:q, :]
        mf = mc.astype(jnp.float32)

        scores_b = lax.dot_general(c_b, b_b, nt_dims, preferred_element_type=jnp.float32
                                   ).astype(jnp.bfloat16)
        ys = []
        for r in range(HEADS_PER_GROUP):
            al = acum_l[:, r * LANES:(r + 1) * LANES]
            last = acum_last[:, r * LANES:(r + 1) * LANES]
            beta_row = (mf[BETA_ROW0 + r:BETA_ROW0 + r + 1, :]
                        + mf[BETA_ROW0 + HEADS_PER_GROUP + r:BETA_ROW0 + HEADS_PER_GROUP + r + 1, :]
                        + mf[BETA_ROW0 + 2 * HEADS_PER_GROUP + r:
                             BETA_ROW0 + 2 * HEADS_PER_GROUP + r + 1, :])
            seg = (al - beta_row).astype(jnp.bfloat16)
            w_r = scores_b * jnp.exp2(jnp.where(causal, seg, -jnp.inf))
            c_r = c_b * jnp.exp2(al).astype(jnp.bfloat16)
            xt_r = xt_b[r * HEAD_DIM:(r + 1) * HEAD_DIM, :]
            xw_r = xt_r * jnp.exp2(last - beta_row).astype(jnp.bfloat16)
            s_r = state_ref[g, r]
            ys.append(lax.dot_general(
                jnp.concatenate([w_r, c_r], axis=1),
                jnp.concatenate([xt_r, s_r.astype(jnp.bfloat16)], axis=1),
                nt_dims, preferred_element_type=jnp.float32))
            state_ref[g, r] = s_r * jnp.exp2(last) + _dot(xw_r, b_b)
        y = jnp.concatenate(ys, axis=-1)

        y = y + dskip * xs
        y = y * _silu(p_ref[b0:b0 + SLABS, :, G_Z:G_CB].reshape(q, GROUP_X))
        y = y * lax.rsqrt(jnp.mean(y * y, axis=-1, keepdims=True) + EPS)
        ys_ref[r0:r0 + q, :] = (y * nw).astype(ys_ref.dtype)
        after_chunk(c)

    tail_ref[g] = tail
    stail_ref[g] = stail


def _mixer_kernel(u_ref, w_ref, parts_a_ref, parts_b_ref,
                  cw_ref, cbias_ref, scw_ref, dskip_ref, nw_ref, et_ref,
                  ys_a_ref, yc_a_ref, gate_a_ref, ys_b_ref, yc_b_ref, gate_b_ref,
                  pa_ref, pb_ref, state_ref, tail_ref, stail_ref, *, tb, n_items, nt):
    s = pl.program_id(0)
    q = SCAN_CHUNK

    @pl.when(s == 0)
    def _():
        pb_ref[...] = jnp.zeros_like(pb_ref)
        state_ref[...] = jnp.zeros_like(state_ref)
        tail_ref[...] = jnp.zeros_like(tail_ref)
        stail_ref[...] = jnp.zeros_like(stail_ref)

    item_a = jnp.minimum(s, n_items - 1)
    item_b = jnp.maximum(s - 1, 0)
    g_a = 2 * (item_a % PAIRS)
    g_b = 2 * (item_b % PAIRS) + 1
    first_a = (item_a // PAIRS) % nt == 0
    first_b = (item_b // PAIRS) % nt == 0

    e_t = et_ref[...]
    t_row = _chunk_time(lax.broadcasted_iota(jnp.int32, (q, q), 0))
    t_col = _chunk_time(lax.broadcasted_iota(jnp.int32, (q, q), 1))
    causal = t_row >= t_col
    common = (cw_ref, cbias_ref, scw_ref, dskip_ref, nw_ref, e_t, causal)
    hist = (state_ref, tail_ref, stail_ref, tb)

    def project_after(schedule):
        def after_chunk(c):
            for piece in schedule.get(c, ()):
                _project_piece(u_ref, w_ref, pa_ref, pb_ref, piece, tb)
        return after_chunk

    _mix_group(pb_ref, parts_b_ref, g_b, first_b, *common, ys_b_ref, yc_b_ref, gate_b_ref, *hist,
               project_after(PIECES_WHILE_ODD), g_b, False)
    drain = s == n_items
    _mix_group(pa_ref, parts_a_ref, g_a, first_a, *common, ys_a_ref, yc_a_ref, gate_a_ref, *hist,
               project_after(PIECES_WHILE_EVEN), jnp.where(drain, GROUPS, g_a), True)


def _mixer_call(u, w_pairs, layer, parts, cw, cbias, scw, dskip_e, nw, e_t, bsz, seq, tb=1024):
    t_total = u.shape[0]
    nt = seq // tb
    n_blocks = t_total // tb
    n_items = n_blocks * PAIRS
    item_a = lambda s: jnp.minimum(s, n_items - 1)
    item_b = lambda s: jnp.maximum(s - 1, 0)
    out_a = lambda s: (item_a(s) // PAIRS, item_a(s) % PAIRS)
    out_b = lambda s: (item_b(s) // PAIRS, item_b(s) % PAIRS)

    def full(arr):
        nd = arr.ndim
        return pl.BlockSpec(arr.shape, lambda s: (0,) * nd)

    half = GROUPS // 2
    out_shapes = [
        jax.ShapeDtypeStruct((t_total, half *GROUP_X), jnp.bfloat16),
        jax.ShapeDtypeStruct((t_total, half *SHORT_PER_GROUP), jnp.bfloat16),
        jax.ShapeDtypeStruct((t_total, half *GATE_PER_GROUP), jnp.bfloat16),
    ]
    widths = (GROUP_X, SHORT_PER_GROUP, GATE_PER_GROUP)
    return pl.pallas_call(
        functools.partial(_mixer_kernel, tb=tb, n_items=n_items, nt=nt),
        grid=(n_items + 1,),
        in_specs=[
            pl.BlockSpec((tb, D_MODEL), lambda s: (item_a(s) // PAIRS, 0)),
            pl.BlockSpec((1, D_MODEL, 2 * GROUP_COLS), lambda s: (layer, 0, item_a(s) % PAIRS)),
            pl.BlockSpec((PART_ROWS, tb), lambda s: (2 * (item_a(s) % PAIRS), item_a(s) // PAIRS)),
            pl.BlockSpec((PART_ROWS, tb),
                         lambda s: (2 * (item_b(s) % PAIRS) + 1, item_b(s) // PAIRS)),
            full(cw), full(cbias), full(scw), full(dskip_e), full(nw), full(e_t),
        ],
        out_specs=([pl.BlockSpec((tb, w), out_a) for w in widths]
                   + [pl.BlockSpec((tb, w), out_b) for w in widths]),
        out_shape=out_shapes + out_shapes,
        scratch_shapes=[
            pltpu.VMEM((tb // SUBLANES, SUBLANES, GROUP_COLS), jnp.float32),
            pltpu.VMEM((tb // SUBLANES, SUBLANES, GROUP_COLS), jnp.float32),
            pltpu.VMEM((GROUPS + 1, HEADS_PER_GROUP, HEAD_DIM, STATE), jnp.float32),
            pltpu.VMEM((GROUPS + 1, SSM_CONV_WIDTH - 1, SUBLANES, GROUP_CONV), jnp.float32),
            pltpu.VMEM((GROUPS + 1, CONV_WIDTH - 1, SUBLANES, SHORT_PER_GROUP), jnp.float32),
        ],
        compiler_params=_cparams(1, VMEM_LIMIT_LARGE),
        name="mixer",
    )(u, w_pairs, parts, parts, cw, cbias, scw, dskip_e, nw, e_t)


def _out_kernel(yca_ref, ycb_ref, ysa_ref, ysb_ref, ga_ref, gb_ref, x_ref, mod_ref,
                wc_ref, ws_ref, wo_ref, o_ref):
    p_conv = _dot(jnp.concatenate([yca_ref[...], ycb_ref[...]], axis=1), wc_ref[0])
    y_ssm = jnp.concatenate(
        [ref[:, i * GROUP_X:(i + 1) * GROUP_X] for i in range(PAIRS) for ref in (ysa_ref, ysb_ref)],
        axis=1)
    p_ssm = _dot(y_ssm, ws_ref[0])
    g_conv = 1.0 / (1.0 + jnp.exp(-ga_ref[...].astype(jnp.float32)))
    g_ssm = 1.0 / (1.0 + jnp.exp(-gb_ref[...].astype(jnp.float32)))
    merged = g_conv * p_conv + g_ssm * p_ssm
    mix = _dot(merged.astype(jnp.bfloat16), wo_ref[0])
    gate1 = mod_ref[0, 2:3, :]
    o_ref[...] = x_ref[...] + gate1 * mix


def _out_call(mixed, x2, mod_l, wc, ws, wo, layer, seq, tm=512):
    ys_a, yc_a, g_a, ys_b, yc_b, g_b = mixed
    t_total = x2.shape[0]
    tiles_per_batch = seq // tm
    row = lambda width: pl.BlockSpec((tm, width), lambda i: (i, 0))
    const = lambda shape: pl.BlockSpec((1,) + shape, lambda i: (layer, 0, 0))
    return pl.pallas_call(
        _out_kernel,
        grid=(t_total // tm,),
        in_specs=[
            row(CONV_DIM // 2), row(CONV_DIM // 2), row(D_SSM // 2), row(D_SSM // 2),
            row(D_MODEL), row(D_MODEL), row(D_MODEL),
            pl.BlockSpec((1, MOD_ROWS, D_MODEL), lambda i: (i // tiles_per_batch, 0, 0)),
            const((CONV_DIM, D_MODEL)), const((D_SSM, D_MODEL)), const((D_MODEL, D_MODEL)),
        ],
        out_specs=pl.BlockSpec((tm, D_MODEL), lambda i: (i, 0)),
        out_shape=jax.ShapeDtypeStruct((t_total, D_MODEL), jnp.float32),
        compiler_params=_cparams(1),
        name="mix_out",
    )(yc_a, yc_b, ys_a, ys_b, g_a, g_b, x2, mod_l, wc, ws, wo)


MLP_NORM_ROWS = 128


def _mlp_kernel(x_ref, xnext_ref, mod_ref, modnext_ref, ln_ref, wup_ref, wdn_ref, fn_ref,
                fromperm_ref, lmod_ref, lln_ref, wdt_ref, hp_ref, cum_ref, perm_ref,
                *out_and_scratch, tm, tf, final):
    if final:
        o_ref, ua_ref, ub_ref = out_and_scratch
    else:
        o_ref, unext_ref, parts_ref, ua_ref, ub_ref = out_and_scratch
    ln = ln_ref[...]

    def norm_rows(src_ref, src0, mod, u_ref, r0):
        w = ln * (1.0 + mod[0, 4:5, :])
        xf = src_ref[src0 + r0:src0 + r0 + MLP_NORM_ROWS, :]
        y = xf * lax.rsqrt(jnp.mean(xf * xf, axis=-1, keepdims=True) + EPS)
        u_ref[r0:r0 + MLP_NORM_ROWS, :] = (y * w + mod[0, 3:4, :]).astype(u_ref.dtype)

    @pl.when(pl.program_id(0) == 0)
    def _():
        for r0 in range(0, tm, MLP_NORM_ROWS):
            norm_rows(x_ref, 0, mod_ref, ua_ref, r0)

    def mlp_tile(u_ref, row0, side_work):
        acc = None
        for j in range(D_FF // tf):
            h = jnp.maximum(_dot(u_ref[...], wup_ref[0, :, j * tf:(j + 1) * tf]), 0.0)
            side_work()
            d = _dot((h * h).astype(jnp.bfloat16), wdn_ref[0, j * tf:(j + 1) * tf, :])
            side_work()
            acc = d if acc is None else acc + d
        xn = x_ref[row0:row0 + tm, :] + mod_ref[0, 5:6, :] * acc
        if final:
            ms = jnp.mean(xn * xn, axis=-1, keepdims=True)
            xf = xn * lax.rsqrt(ms + EPS) * fn_ref[...]
            onehot = fromperm_ref[...]
            for c0 in range(0, tm, SCAN_CHUNK):
                hi, mid, lo = _split3(xf[c0:c0 + SCAN_CHUNK, :])
                o_ref[row0 + c0:row0 + c0 + SCAN_CHUNK, :] = (
                    _dot(onehot, hi) + _dot(onehot, mid) + _dot(onehot, lo))
        else:
            o_ref[row0:row0 + tm, :] = xn
            w = lln_ref[...] * (1.0 + lmod_ref[0, 1:2, :])
            y = xn * lax.rsqrt(jnp.mean(xn * xn, axis=-1, keepdims=True) + EPS)
            u_rows = (y * w + lmod_ref[0, 0:1, :]).astype(unext_ref.dtype)
            unext_ref[row0:row0 + tm, :] = u_rows
            _dt_parts(u_rows, wdt_ref, hp_ref, cum_ref, perm_ref, parts_ref, row0)

    def norm_in_pieces(src_ref, src0, mod, u_ref):
        todo = list(range(0, tm, MLP_NORM_ROWS))
        per_call = -(-len(todo) // (2 * (D_FF // tf)))

        def side_work():
            for _ in range(per_call):
                if todo:
                    norm_rows(src_ref, src0, mod, u_ref, todo.pop(0))
        return side_work

    mlp_tile(ua_ref, 0, norm_in_pieces(x_ref, tm, mod_ref, ub_ref))
    mlp_tile(ub_ref, tm, norm_in_pieces(xnext_ref, 0, modnext_ref, ua_ref))


def _mlp_call(x2, mod_l, ln, w_up, w_dn, layer, fnorm, from_perm, lead, seq, final,
              tm=512, tf=2048):
    lmod, lln, lwdt, lhp, cum, perm = lead
    t_total = x2.shape[0]
    n_blocks = t_total // (2 * tm)
    blocks_per_batch = seq // (2 * tm)
    nxt = lambda i: jnp.minimum(i + 1, n_blocks - 1)
    resident = dict(pipeline_mode=pl.Buffered(1))
    rows = pl.BlockSpec((2 * tm, D_MODEL), lambda i: (i, 0))
    out_specs = [rows]
    out_shape = [jax.ShapeDtypeStruct((t_total, D_MODEL), jnp.float32)]
    if not final:
        out_specs += [rows, pl.BlockSpec((GROUPS * PART_ROWS, 2 * tm), lambda i: (0, i))]
        out_shape += [jax.ShapeDtypeStruct((t_total, D_MODEL), jnp.bfloat16),
                      jax.ShapeDtypeStruct((GROUPS * PART_ROWS, t_total), jnp.bfloat16)]
    return pl.pallas_call(
        functools.partial(_mlp_kernel, tm=tm, tf=tf, final=final),
        grid=(n_blocks,),
        in_specs=[
            pl.BlockSpec((2 * tm, D_MODEL), lambda i: (i, 0)),
            pl.BlockSpec((tm, D_MODEL), lambda i: (2 * nxt(i), 0)),
            pl.BlockSpec((1, MOD_ROWS, D_MODEL), lambda i: (i // blocks_per_batch, 0, 0)),
            pl.BlockSpec((1, MOD_ROWS, D_MODEL), lambda i: (nxt(i) // blocks_per_batch, 0, 0)),
            pl.BlockSpec((1, D_MODEL), lambda i: (0, 0)),
            pl.BlockSpec((1, D_MODEL, D_FF), lambda i: (layer, 0, 0), **resident),
            pl.BlockSpec((1, D_FF, D_MODEL), lambda i: (layer, 0, 0), **resident),
            pl.BlockSpec((1, D_MODEL), lambda i: (0, 0)),
            pl.BlockSpec((SCAN_CHUNK, SCAN_CHUNK), lambda i: (0, 0)),
            pl.BlockSpec((1, MOD_ROWS, D_MODEL), lambda i: (i // blocks_per_batch, 0, 0)),
            pl.BlockSpec((1, D_MODEL), lambda i: (0, 0)),
            pl.BlockSpec((LANES, D_MODEL), lambda i: (0, 0)),
            pl.BlockSpec((2, HEADS, LANES), lambda i: (0, 0, 0)),
            pl.BlockSpec((SCAN_CHUNK, SCAN_CHUNK), lambda i: (0, 0)),
            pl.BlockSpec((GROUPS * PART_ROWS, 2 * N_SPLIT * HEADS), lambda i: (0, 0)),
        ],
        out_specs=out_specs,
        out_shape=out_shape,
        scratch_shapes=[pltpu.VMEM((tm, D_MODEL), jnp.bfloat16),
                        pltpu.VMEM((tm, D_MODEL), jnp.bfloat16)],
        compiler_params=_cparams(1, VMEM_LIMIT_LARGE),
        name="mlp",
    )(x2, x2, mod_l, mod_l, ln, w_up, w_dn, fnorm, from_perm, lmod, lln, lwdt, lhp, cum, perm)


def _head_constants():
    e_t = np.zeros((PART_ROWS, HEADS_PER_GROUP, LANES), np.float32)
    perm = np.zeros((GROUPS, PART_ROWS, 2 * N_SPLIT, HEADS), np.float32)
    for k in range(2 * N_SPLIT):
        for r in range(HEADS_PER_GROUP):
            if k < N_SPLIT:
                e_t[k * HEADS_PER_GROUP + r, r, :] = 1.0
            for g in range(GROUPS):
                perm[g, k * HEADS_PER_GROUP + r, k, g * HEADS_PER_GROUP + r] = 1.0
    e_t = e_t.reshape(PART_ROWS, HEADS_PER_GROUP * LANES)
    perm = perm.reshape(GROUPS * PART_ROWS, 2 * N_SPLIT * HEADS)
    tm = _chunk_time(np.arange(SCAN_CHUNK))
    cum = (tm[:, None] <= tm[None, :]).astype(np.float32)
    to_perm = (np.arange(SCAN_CHUNK)[None, :] == tm[:, None]).astype(np.float32)
    bf = jnp.bfloat16
    return (jnp.asarray(e_t, bf), jnp.asarray(perm, bf), jnp.asarray(cum, bf),
            jnp.asarray(to_perm, bf), jnp.asarray(to_perm.T, bf))


def _regroup_kernel(xs_ref, bs_ref, cs_ref, z_ref, cb_ref, cc_ref, cx_ref, gate_ref, dt_ref,
                    o_ref, odt_ref):
    for ref, lo, hi in ((xs_ref, G_XBC, G_XBC + GROUP_X), (bs_ref, G_XBC + GROUP_X, G_Z - STATE),
                        (cs_ref, G_Z - STATE, G_Z), (z_ref, G_Z, G_CB), (cb_ref, G_CB, G_CC),
                        (cc_ref, G_CC, G_CX), (cx_ref, G_CX, G_GATE), (gate_ref, G_GATE, GROUP_COLS)):
        o_ref[0, :, lo:hi] = ref[0].T.astype(o_ref.dtype)

    @pl.when(pl.program_id(1) == 0)
    def _():
        row = lax.broadcasted_iota(jnp.int32, (LANES, D_MODEL), 0)
        odt_ref[0] = jnp.where(row < HEADS, dt_ref[0], 0.0).astype(odt_ref.dtype)


def _regroup_w_in(w_t):
    def rows(height, off, paired):
        base = off // height
        if paired:
            return pl.BlockSpec((1, height, D_MODEL),
                                lambda l, g: (l, base + (g % 2) * PAIRS + g // 2, 0))
        return pl.BlockSpec((1, height, D_MODEL), lambda l, g: (l, base + g, 0))

    return pl.pallas_call(
        _regroup_kernel,
        grid=(DEPTH, GROUPS),
        in_specs=[
            rows(GROUP_X, OFF_XS, False), rows(STATE, OFF_BS, False), rows(STATE, OFF_CS, False),
            rows(GROUP_X, OFF_Z, False),
            rows(SHORT_PER_GROUP, OFF_CB, True), rows(SHORT_PER_GROUP, OFF_CC, True),
            rows(SHORT_PER_GROUP, OFF_CX, True), rows(GATE_PER_GROUP, 0, True),
            pl.BlockSpec((1, LANES, D_MODEL), lambda l, g: (l, OFF_DT // LANES, 0)),
        ],
        out_specs=[
            pl.BlockSpec((1, D_MODEL, GROUP_COLS), lambda l, g: (l, 0, g)),
            pl.BlockSpec((1, LANES, D_MODEL), lambda l, g: (l, 0, 0)),
        ],
        out_shape=[
            jax.ShapeDtypeStruct((DEPTH, D_MODEL, GROUPS * GROUP_COLS), jnp.bfloat16),
            jax.ShapeDtypeStruct((DEPTH, LANES, D_MODEL), jnp.bfloat16),
        ],
        compiler_params=_cparams(2),
        name="w_in_regroup",
    )(*([w_t] * 9))


def _group_conv_params(p):
    k = p.shape[1]
    xs = p[..., :D_SSM].reshape(DEPTH, k, GROUPS, GROUP_X)
    bs = p[..., D_SSM:D_SSM + GROUPS * STATE].reshape(DEPTH, k, GROUPS, STATE)
    cs = p[..., D_SSM + GROUPS * STATE:].reshape(DEPTH, k, GROUPS, STATE)
    return jnp.swapaxes(jnp.concatenate([xs, bs, cs], axis=-1), 1, 2)


def kernel(x, c, w_ada, b_ada, ln1, ln2, w_in, conv_w, ssm_conv_w, ssm_conv_b, dt_bias, a_log,
           d_skip, ssm_norm_w, w_conv_out, w_ssm_out, w_o, w_up, w_down, final_norm):
    bsz, seq, _ = x.shape
    bf = jnp.bfloat16
    f32 = jnp.float32

    c_pad = jnp.zeros((SUBLANES, D_MODEL), f32).at[:bsz].set(c)
    mod_all = _ada_call(c_pad, w_ada, b_ada)
    mod_all = mod_all[:, :bsz].reshape(DEPTH, bsz, N_MOD, D_MODEL)
    mod_all = jnp.pad(mod_all, ((0, 0), (0, 0), (0, MOD_ROWS - N_MOD), (0, 0)))

    w_pairs, w_dt = _regroup_w_in(jnp.swapaxes(w_in, 1, 2))
    hp = jnp.broadcast_to(jnp.stack([dt_bias, a_log], axis=1)[..., None],
                          (DEPTH, 2, HEADS, LANES))
    cw = _group_conv_params(ssm_conv_w)
    cbias = _group_conv_params(ssm_conv_b[:, None, :])
    scw = conv_w.reshape(DEPTH, CONV_WIDTH, 2, PAIRS, SHORT_PER_GROUP)
    scw = jnp.transpose(scw, (0, 3, 2, 1, 4)).reshape(DEPTH, GROUPS, CONV_WIDTH, SHORT_PER_GROUP)
    dskip_e = jnp.repeat(d_skip, HEAD_DIM, axis=-1).reshape(DEPTH, GROUPS, 1, GROUP_X)
    nw = ssm_norm_w.reshape(DEPTH, GROUPS, 1, GROUP_X)
    ws = w_ssm_out.astype(bf)
    wc = w_conv_out.astype(bf)
    wo = w_o.astype(bf)
    wup = w_up.astype(bf)
    wdn = w_down.astype(bf)

    e_t, perm, cum, to_perm, from_perm = _head_constants()
    fnorm = final_norm.reshape(1, D_MODEL)
    x2 = x.reshape(bsz * seq, D_MODEL)

    x2, u, parts = _pre_call(x2, mod_all[0], ln1[0].reshape(1, D_MODEL), w_dt[0], hp[0], cum,
                             perm, seq, to_perm=to_perm)
    for l in range(DEPTH):
        mod_l = mod_all[l]
        final = l == DEPTH - 1
        mixed = _mixer_call(u, w_pairs, l, parts, cw[l], cbias[l], scw[l], dskip_e[l], nw[l], e_t,
                            bsz, seq)
        x2 = _out_call(mixed, x2, mod_l, wc, ws, wo, l, seq)
        n = l if final else l + 1
        lead = (mod_all[n], ln1[n].reshape(1, D_MODEL), w_dt[n], hp[n], cum, perm)
        res = _mlp_call(x2, mod_l, ln2[l].reshape(1, D_MODEL), wup, wdn, l, fnorm, from_perm, lead,
                        seq, final=final)
        if final:
            x2 = res[0]
        else:
            x2, u, parts = res
    return x2.reshape(bsz, seq, D_MODEL)
```

```python
import functools

import numpy as np
import jax
import jax.numpy as jnp
from jax import lax
from jax.experimental import pallas as pl
from jax.experimental.pallas import tpu as pltpu

D_MODEL = 1024
DEPTH = 4
EPS = 1e-6
N_MOD = 6
MOD_ROWS = 8
CONV_DIM = D_MODEL
CONV_WIDTH = 3
D_SSM = 2 * D_MODEL
HEAD_DIM = 64
HEADS = D_SSM // HEAD_DIM
GROUPS = 8
PAIRS = GROUPS // 2
HEADS_PER_GROUP = HEADS // GROUPS
GROUP_X = HEADS_PER_GROUP * HEAD_DIM
STATE = 128
SSM_CONV_WIDTH = 4
D_FF = 4 * D_MODEL
LANES = 128
SUBLANES = 8
F32_TINY = float(np.finfo(np.float32).tiny)
LOG2E = float(np.log2(np.e))

OFF_CB = 2 * D_MODEL
OFF_CC = OFF_CB + CONV_DIM
OFF_CX = OFF_CC + CONV_DIM
OFF_Z = OFF_CX + CONV_DIM
OFF_XS = OFF_Z + D_SSM
OFF_BS = OFF_XS + D_SSM
OFF_CS = OFF_BS + GROUPS * STATE
OFF_DT = OFF_CS + GROUPS * STATE

SCAN_CHUNK = 128
SLABS = SCAN_CHUNK // SUBLANES
GROUP_CONV = GROUP_X + 2 * STATE
SHORT_PER_GROUP = CONV_DIM // GROUPS
GATE_PER_GROUP = 2 * D_MODEL // GROUPS
G_XBC = 0
G_Z = G_XBC + GROUP_CONV
G_CB = G_Z + GROUP_X
G_CC = G_CB + SHORT_PER_GROUP
G_CX = G_CC + SHORT_PER_GROUP
G_GATE = G_CX + SHORT_PER_GROUP
GROUP_COLS = G_GATE + GATE_PER_GROUP
PROJ_PIECE = 256
PIECES_WHILE_ODD = {-1: (0,), 0: (1,), 1: (2,), 3: (3,), 4: (4,), 7: (5,)}
PIECES_WHILE_EVEN = {0: (6,), 2: (7,), 3: (8,), 4: (9,), 6: (10,)}

N_SPLIT = 3
PART_ROWS = 32
BETA_ROW0 = N_SPLIT * HEADS_PER_GROUP

VMEM_LIMIT = 48 * 1024 * 1024
VMEM_LIMIT_LARGE = 56 * 1024 * 1024


def _cparams(n_axes, limit=VMEM_LIMIT):
    return pltpu.CompilerParams(
        dimension_semantics=("arbitrary",) * n_axes,
        vmem_limit_bytes=limit)


def _dot(a, b):
    return jnp.dot(a, b, preferred_element_type=jnp.float32)


def _dot_t(a, b_t):
    return lax.dot_general(a, b_t, (((1,), (1,)), ((), ())), preferred_element_type=jnp.float32)


def _split3(v):
    hi = v.astype(jnp.bfloat16)
    r1 = v - hi.astype(jnp.float32)
    mid = r1.astype(jnp.bfloat16)
    lo = (r1 - mid.astype(jnp.float32)).astype(jnp.bfloat16)
    return hi, mid, lo


def _silu(v):
    return v / (1.0 + jnp.exp2(v * (-LOG2E)))


def _chunk_time(p):
    return (p % SUBLANES) * SLABS + p // SUBLANES


def _ada_kernel(c_ref, w_ref, b_ref, o_ref):
    c = c_ref[...]
    o_ref[0] = _dot(_silu(c).astype(jnp.bfloat16), w_ref[0].astype(jnp.bfloat16)) + b_ref[0]


def _ada_call(c_pad, w_ada, b_ada):
    n_out = N_MOD * D_MODEL
    tn = D_MODEL
    return pl.pallas_call(
        _ada_kernel,
        grid=(DEPTH, n_out // tn),
        in_specs=[
            pl.BlockSpec((SUBLANES, D_MODEL), lambda l, j: (0, 0)),
            pl.BlockSpec((1, D_MODEL, tn), lambda l, j: (l, 0, j)),
            pl.BlockSpec((1, 1, tn), lambda l, j: (l, 0, j)),
        ],
        out_specs=pl.BlockSpec((1, SUBLANES, tn), lambda l, j: (l, 0, j)),
        out_shape=jax.ShapeDtypeStruct((DEPTH, SUBLANES, n_out), jnp.float32),
        compiler_params=_cparams(2),
        name="ada_mod",
    )(c_pad, w_ada, b_ada.reshape(DEPTH, 1, n_out))


def _modnorm_to(x_ref, ln_ref, scale, shift, u_ref, rows, chunk=256):
    w = ln_ref[...] * (1.0 + scale)
    for r0 in range(0, rows, chunk):
        xf = x_ref[r0:r0 + chunk, :]
        ms = jnp.mean(xf * xf, axis=-1, keepdims=True)
        y = xf * lax.rsqrt(ms + EPS)
        u_ref[r0:r0 + chunk, :] = (y * w + shift).astype(u_ref.dtype)


def _permute_rows(src_ref, dst_ref, onehot_b, rows):
    for r0 in range(0, rows, SCAN_CHUNK):
        hi, mid, lo = _split3(src_ref[r0:r0 + SCAN_CHUNK, :])
        dst_ref[r0:r0 + SCAN_CHUNK, :] = (_dot(onehot_b, hi) + _dot(onehot_b, mid)
                                          + _dot(onehot_b, lo)).astype(dst_ref.dtype)


def _pre_kernel(*refs, tp, permute_in):
    if permute_in:
        (xnat_ref, mod_ref, ln_ref, wdt_ref, hp_ref, cum_ref, perm_ref, toperm_ref,
         x_ref, u_ref, parts_ref) = refs
        _permute_rows(xnat_ref, x_ref, toperm_ref[...], tp)
    else:
        x_ref, mod_ref, ln_ref, wdt_ref, hp_ref, cum_ref, perm_ref, u_ref, parts_ref = refs
    shift = mod_ref[0, 0:1, :]
    scale = mod_ref[0, 1:2, :]
    _modnorm_to(x_ref, ln_ref, scale, shift, u_ref, tp)
    _dt_parts(u_ref[...], wdt_ref, hp_ref, cum_ref, perm_ref, parts_ref, 0)


def _dt_parts(u_rows, wdt_ref, hp_ref, cum_ref, perm_ref, parts_ref, col0):
    dt_raw_t = _dot_t(wdt_ref[...], u_rows)[0:HEADS, :]
    dt_bias = hp_ref[0]
    a_neg = -jnp.exp(hp_ref[1])
    cum_t = cum_ref[...]
    perm = perm_ref[...]
    q = SCAN_CHUNK
    for c in range(u_rows.shape[0] // q):
        r0 = c * q
        dtr = dt_raw_t[:, r0:r0 + q] + dt_bias
        dt = jnp.maximum(dtr, 0.0) + jnp.log1p(jnp.exp(-jnp.abs(dtr)))
        a_parts = _split3(dt * a_neg)
        acum = _dot(a_parts[0], cum_t) + _dot(a_parts[1], cum_t) + _dot(a_parts[2], cum_t)
        beta = acum - jnp.log(jnp.maximum(dt, F32_TINY))
        terms = jnp.concatenate(_split3(acum * LOG2E) + _split3(beta * LOG2E), axis=0)
        parts_ref[:, col0 + r0:col0 + r0 + q] = _dot(perm, terms).astype(parts_ref.dtype)


def _pre_call(x2, mod_l, ln, w_dt, hp, cum, perm, seq, to_perm=None, tp=1024):
    t_total = x2.shape[0]
    tiles_per_batch = seq // tp
    permute_in = to_perm is not None
    square = pl.BlockSpec((SCAN_CHUNK, SCAN_CHUNK), lambda i: (0, 0))
    rows = pl.BlockSpec((tp, D_MODEL), lambda i: (i, 0))
    return pl.pallas_call(
        functools.partial(_pre_kernel, tp=tp, permute_in=permute_in),
        grid=(t_total // tp,),
        in_specs=[
            rows,
            pl.BlockSpec((1, MOD_ROWS, D_MODEL), lambda i: (i // tiles_per_batch, 0, 0)),
            pl.BlockSpec((1, D_MODEL), lambda i: (0, 0)),
            pl.BlockSpec((LANES, D_MODEL), lambda i: (0, 0)),
            pl.BlockSpec((2, HEADS, LANES), lambda i: (0, 0, 0)),
            square,
            pl.BlockSpec((GROUPS * PART_ROWS, 2 * N_SPLIT * HEADS), lambda i: (0, 0)),
        ] + ([square] if permute_in else []),
        out_specs=([rows] if permute_in else []) + [
            rows,
            pl.BlockSpec((GROUPS * PART_ROWS, tp), lambda i: (0, i)),
        ],
        out_shape=([jax.ShapeDtypeStruct((t_total, D_MODEL), jnp.float32)] if permute_in else []) + [
            jax.ShapeDtypeStruct((t_total, D_MODEL), jnp.bfloat16),
            jax.ShapeDtypeStruct((GROUPS * PART_ROWS, t_total), jnp.bfloat16),
        ],
        compiler_params=_cparams(1),
        name="pre",
    )(*((x2, mod_l, ln, w_dt, hp, cum, perm) + ((to_perm,) if permute_in else ())))


def _shifted_taps(cur, tail, width):
    n_wrap = width - 1
    cur_tail = cur[SLABS - n_wrap:]
    wrapped = jnp.concatenate([tail[:, SUBLANES - 1:, :], cur_tail[:, :SUBLANES - 1, :]], axis=1)
    taps = [cur]
    for s in range(1, width):
        taps.append(jnp.concatenate([wrapped[n_wrap - s:], cur[:SLABS - s]], axis=0))
    return taps, cur_tail


def _project_piece(u_ref, w_ref, pa_ref, pb_ref, piece, tb):
    lo = piece * PROJ_PIECE
    res = _dot(u_ref[...], w_ref[0, :, lo:lo + PROJ_PIECE]
               ).reshape(tb // SUBLANES, SUBLANES, PROJ_PIECE)
    n_a = min(max(GROUP_COLS - lo, 0), PROJ_PIECE)
    if n_a:
        pa_ref[:, :, lo:lo + n_a] = res[:, :, 0:n_a]
    if n_a < PROJ_PIECE:
        b0 = lo + n_a - GROUP_COLS
        pb_ref[:, :, b0:b0 + PROJ_PIECE - n_a] = res[:, :, n_a:]


def _mix_group(p_ref, parts_ref, g, first, cw_ref, cbias_ref, scw_ref, dskip_ref, nw_ref, e_t,
               causal, ys_ref, yc_ref, gate_ref, state_ref, tail_ref, stail_ref, tb, after_chunk,
               hist_src, keep_start):
    q = SCAN_CHUNK
    nt_dims = (((1,), (1,)), ((), ()))
    tn_dims = (((0,), (0,)), ((), ()))
    cw = cw_ref[g]
    cbias = cbias_ref[g]
    scw = scw_ref[g]
    dskip = dskip_ref[g]
    nw = nw_ref[g]

    gate_ref[...] = p_ref[:, :, G_GATE:GROUP_COLS].reshape(tb, GATE_PER_GROUP).astype(gate_ref.dtype)

    state_in = jnp.where(first, 0.0, state_ref[hist_src])
    tail = jnp.where(first, 0.0, tail_ref[hist_src])
    stail = jnp.where(first, 0.0, stail_ref[hist_src])
    if keep_start:
        state_ref[GROUPS] = state_in
        tail_ref[GROUPS] = tail
        stail_ref[GROUPS] = stail
    state_ref[g] = state_in

    after_chunk(-1)
    for c in range(tb // q):
        r0 = c * q
        b0 = c * SLABS
        sv = p_ref[b0:b0 + SLABS, :, G_CC:G_CX] * p_ref[b0:b0 + SLABS, :, G_CX:G_GATE]
        staps, stail = _shifted_taps(sv, stail, CONV_WIDTH)
        sconv = scw[CONV_WIDTH - 1:CONV_WIDTH, :] * staps[0]
        for s in range(1, CONV_WIDTH):
            sconv += scw[CONV_WIDTH - 1 - s:CONV_WIDTH - s, :] * staps[s]
        yc = p_ref[b0:b0 + SLABS, :, G_CB:G_CC] * sconv
        yc_ref[r0:r0 + q, :] = yc.reshape(q, SHORT_PER_GROUP).astype(yc_ref.dtype)

        taps, tail = _shifted_taps(p_ref[b0:b0 + SLABS, :, G_XBC:G_Z], tail, SSM_CONV_WIDTH)
        acc = cbias + cw[SSM_CONV_WIDTH - 1:SSM_CONV_WIDTH, :] * taps[0]
        for s in range(1, SSM_CONV_WIDTH):
            acc += cw[SSM_CONV_WIDTH - 1 - s:SSM_CONV_WIDTH - s, :] * taps[s]
        xbc = _silu(acc).reshape(q, GROUP_CONV)
        xs = xbc[:, 0:GROUP_X]
        xt_b = xs.astype(jnp.bfloat16).T
        b_b = xbc[:, GROUP_X:GROUP_X + STATE].astype(jnp.bfloat16)
        c_b = xbc[:, GROUP_X + STATE:GROUP_CONV].astype(jnp.bfloat16)

        mc = parts_ref[:, r0:r0 + q]
        acum_l = lax.dot_general(mc, e_t, tn_dims, preferred_element_type=jnp.float32)
        acum_last = acum_l[q - 1:q, :]
        mf = mc.astype(jnp.float32)

        scores_b = lax.dot_general(c_b, b_b, nt_dims, preferred_element_type=jnp.float32
                                   ).astype(jnp.bfloat16)
        ys = []
        for r in range(HEADS_PER_GROUP):
            al = acum_l[:, r * LANES:(r + 1) * LANES]
            last = acum_last[:, r * LANES:(r + 1) * LANES]
            beta_row = (mf[BETA_ROW0 + r:BETA_ROW0 + r + 1, :]
                        + mf[BETA_ROW0 + HEADS_PER_GROUP + r:BETA_ROW0 + HEADS_PER_GROUP + r + 1, :]
                        + mf[BETA_ROW0 + 2 * HEADS_PER_GROUP + r:
                             BETA_ROW0 + 2 * HEADS_PER_GROUP + r + 1, :])
            seg = (al - beta_row).astype(jnp.bfloat16)
            w_r = scores_b * jnp.exp2(jnp.where(causal, seg, -jnp.inf))
            c_r = c_b * jnp.exp2(al).astype(jnp.bfloat16)
            xt_r = xt_b[r * HEAD_DIM:(r + 1) * HEAD_DIM, :]
            xw_r = xt_r * jnp.exp2(last - beta_row).astype(jnp.bfloat16)
            s_r = state_ref[g, r]
            ys.append(lax.dot_general(
                jnp.concatenate([w_r, c_r], axis=1),
                jnp.concatenate([xt_r, s_r.astype(jnp.bfloat16)], axis=1),
                nt_dims, preferred_element_type=jnp.float32))
            state_ref[g, r] = s_r * jnp.exp2(last) + _dot(xw_r, b_b)
        y = jnp.concatenate(ys, axis=-1)

        y = y + dskip * xs
        y = y * _silu(p_ref[b0:b0 + SLABS, :, G_Z:G_CB].reshape(q, GROUP_X))
        y = y * lax.rsqrt(jnp.mean(y * y, axis=-1, keepdims=True) + EPS)
        ys_ref[r0:r0 + q, :] = (y * nw).astype(ys_ref.dtype)
        after_chunk(c)

    tail_ref[g] = tail
    stail_ref[g] = stail


def _mixer_kernel(u_ref, w_ref, parts_a_ref, parts_b_ref,
                  cw_ref, cbias_ref, scw_ref, dskip_ref, nw_ref, et_ref,
                  ys_a_ref, yc_a_ref, gate_a_ref, ys_b_ref, yc_b_ref, gate_b_ref,
                  pa_ref, pb_ref, state_ref, tail_ref, stail_ref, *, tb, n_items, nt):
    s = pl.program_id(0)
    q = SCAN_CHUNK

    @pl.when(s == 0)
    def _():
        pb_ref[...] = jnp.zeros_like(pb_ref)
        state_ref[...] = jnp.zeros_like(state_ref)
        tail_ref[...] = jnp.zeros_like(tail_ref)
        stail_ref[...] = jnp.zeros_like(stail_ref)

    item_a = jnp.minimum(s, n_items - 1)
    item_b = jnp.maximum(s - 1, 0)
    g_a = 2 * (item_a % PAIRS)
    g_b = 2 * (item_b % PAIRS) + 1
    first_a = (item_a // PAIRS) % nt == 0
    first_b = (item_b // PAIRS) % nt == 0

    e_t = et_ref[...]
    t_row = _chunk_time(lax.broadcasted_iota(jnp.int32, (q, q), 0))
    t_col = _chunk_time(lax.broadcasted_iota(jnp.int32, (q, q), 1))
    causal = t_row >= t_col
    common = (cw_ref, cbias_ref, scw_ref, dskip_ref, nw_ref, e_t, causal)
    hist = (state_ref, tail_ref, stail_ref, tb)

    def project_after(schedule):
        def after_chunk(c):
            for piece in schedule.get(c, ()):
                _project_piece(u_ref, w_ref, pa_ref, pb_ref, piece, tb)
        return after_chunk

    _mix_group(pb_ref, parts_b_ref, g_b, first_b, *common, ys_b_ref, yc_b_ref, gate_b_ref, *hist,
               project_after(PIECES_WHILE_ODD), g_b, False)
    drain = s == n_items
    _mix_group(pa_ref, parts_a_ref, g_a, first_a, *common, ys_a_ref, yc_a_ref, gate_a_ref, *hist,
               project_after(PIECES_WHILE_EVEN), jnp.where(drain, GROUPS, g_a), True)


def _mixer_call(u, w_pairs, layer, parts, cw, cbias, scw, dskip_e, nw, e_t, bsz, seq, tb=1024):
    t_total = u.shape[0]
    nt = seq // tb
    n_blocks = t_total // tb
    n_items = n_blocks * PAIRS
    item_a = lambda s: jnp.minimum(s, n_items - 1)
    item_b = lambda s: jnp.maximum(s - 1, 0)
    out_a = lambda s: (item_a(s) // PAIRS, item_a(s) % PAIRS)
    out_b = lambda s: (item_b(s) // PAIRS, item_b(s) % PAIRS)

    def full(arr):
        nd = arr.ndim
        return pl.BlockSpec(arr.shape, lambda s: (0,) * nd)

    half = GROUPS // 2
    out_shapes = [
        jax.ShapeDtypeStruct((t_total, half *GROUP_X), jnp.bfloat16),
        jax.ShapeDtypeStruct((t_total, half *SHORT_PER_GROUP), jnp.bfloat16),
        jax.ShapeDtypeStruct((t_total, half *GATE_PER_GROUP), jnp.bfloat16),
    ]
    widths = (GROUP_X, SHORT_PER_GROUP, GATE_PER_GROUP)
    return pl.pallas_call(
        functools.partial(_mixer_kernel, tb=tb, n_items=n_items, nt=nt),
        grid=(n_items + 1,),
        in_specs=[
            pl.BlockSpec((tb, D_MODEL), lambda s: (item_a(s) // PAIRS, 0)),
            pl.BlockSpec((1, D_MODEL, 2 * GROUP_COLS), lambda s: (layer, 0, item_a(s) % PAIRS)),
            pl.BlockSpec((PART_ROWS, tb), lambda s: (2 * (item_a(s) % PAIRS), item_a(s) // PAIRS)),
            pl.BlockSpec((PART_ROWS, tb),
                         lambda s: (2 * (item_b(s) % PAIRS) + 1, item_b(s) // PAIRS)),
            full(cw), full(cbias), full(scw), full(dskip_e), full(nw), full(e_t),
        ],
        out_specs=([pl.BlockSpec((tb, w), out_a) for w in widths]
                   + [pl.BlockSpec((tb, w), out_b) for w in widths]),
        out_shape=out_shapes + out_shapes,
        scratch_shapes=[
            pltpu.VMEM((tb // SUBLANES, SUBLANES, GROUP_COLS), jnp.float32),
            pltpu.VMEM((tb // SUBLANES, SUBLANES, GROUP_COLS), jnp.float32),
            pltpu.VMEM((GROUPS + 1, HEADS_PER_GROUP, HEAD_DIM, STATE), jnp.float32),
            pltpu.VMEM((GROUPS + 1, SSM_CONV_WIDTH - 1, SUBLANES, GROUP_CONV), jnp.float32),
            pltpu.VMEM((GROUPS + 1, CONV_WIDTH - 1, SUBLANES, SHORT_PER_GROUP), jnp.float32),
        ],
        compiler_params=_cparams(1, VMEM_LIMIT_LARGE),
        name="mixer",
    )(u, w_pairs, parts, parts, cw, cbias, scw, dskip_e, nw, e_t)


def _out_kernel(yca_ref, ycb_ref, ysa_ref, ysb_ref, ga_ref, gb_ref, x_ref, mod_ref,
                wc_ref, ws_ref, wo_ref, o_ref):
    p_conv = _dot(jnp.concatenate([yca_ref[...], ycb_ref[...]], axis=1), wc_ref[0])
    y_ssm = jnp.concatenate(
        [ref[:, i * GROUP_X:(i + 1) * GROUP_X] for i in range(PAIRS) for ref in (ysa_ref, ysb_ref)],
        axis=1)
    p_ssm = _dot(y_ssm, ws_ref[0])
    g_conv = 1.0 / (1.0 + jnp.exp(-ga_ref[...].astype(jnp.float32)))
    g_ssm = 1.0 / (1.0 + jnp.exp(-gb_ref[...].astype(jnp.float32)))
    merged = g_conv * p_conv + g_ssm * p_ssm
    mix = _dot(merged.astype(jnp.bfloat16), wo_ref[0])
    gate1 = mod_ref[0, 2:3, :]
    o_ref[...] = x_ref[...] + gate1 * mix


def _out_call(mixed, x2, mod_l, wc, ws, wo, layer, seq, tm=512):
    ys_a, yc_a, g_a, ys_b, yc_b, g_b = mixed
    t_total = x2.shape[0]
    tiles_per_batch = seq // tm
    row = lambda width: pl.BlockSpec((tm, width), lambda i: (i, 0))
    const = lambda shape: pl.BlockSpec((1,) + shape, lambda i: (layer, 0, 0))
    return pl.pallas_call(
        _out_kernel,
        grid=(t_total // tm,),
        in_specs=[
            row(CONV_DIM // 2), row(CONV_DIM // 2), row(D_SSM // 2), row(D_SSM // 2),
            row(D_MODEL), row(D_MODEL), row(D_MODEL),
            pl.BlockSpec((1, MOD_ROWS, D_MODEL), lambda i: (i // tiles_per_batch, 0, 0)),
            const((CONV_DIM, D_MODEL)), const((D_SSM, D_MODEL)), const((D_MODEL, D_MODEL)),
        ],
        out_specs=pl.BlockSpec((tm, D_MODEL), lambda i: (i, 0)),
        out_shape=jax.ShapeDtypeStruct((t_total, D_MODEL), jnp.float32),
        compiler_params=_cparams(1),
        name="mix_out",
    )(yc_a, yc_b, ys_a, ys_b, g_a, g_b, x2, mod_l, wc, ws, wo)


MLP_NORM_ROWS = 128


def _mlp_kernel(x_ref, xnext_ref, mod_ref, modnext_ref, ln_ref, wup_ref, wdn_ref, fn_ref,
                fromperm_ref, lmod_ref, lln_ref, wdt_ref, hp_ref, cum_ref, perm_ref,
                *out_and_scratch, tm, tf, final):
    if final:
        o_ref, ua_ref, ub_ref = out_and_scratch
    else:
        o_ref, unext_ref, parts_ref, ua_ref, ub_ref = out_and_scratch
    ln = ln_ref[...]

    def norm_rows(src_ref, src0, mod, u_ref, r0):
        w = ln * (1.0 + mod[0, 4:5, :])
        xf = src_ref[src0 + r0:src0 + r0 + MLP_NORM_ROWS, :]
        y = xf * lax.rsqrt(jnp.mean(xf * xf, axis=-1, keepdims=True) + EPS)
        u_ref[r0:r0 + MLP_NORM_ROWS, :] = (y * w + mod[0, 3:4, :]).astype(u_ref.dtype)

    @pl.when(pl.program_id(0) == 0)
    def _():
        for r0 in range(0, tm, MLP_NORM_ROWS):
            norm_rows(x_ref, 0, mod_ref, ua_ref, r0)

    def mlp_tile(u_ref, row0, side_work):
        acc = None
        for j in range(D_FF // tf):
            h = jnp.maximum(_dot(u_ref[...], wup_ref[0, :, j * tf:(j + 1) * tf]), 0.0)
            side_work()
            d = _dot((h * h).astype(jnp.bfloat16), wdn_ref[0, j * tf:(j + 1) * tf, :])
            side_work()
            acc = d if acc is None else acc + d
        xn = x_ref[row0:row0 + tm, :] + mod_ref[0, 5:6, :] * acc
        if final:
            ms = jnp.mean(xn * xn, axis=-1, keepdims=True)
            xf = xn * lax.rsqrt(ms + EPS) * fn_ref[...]
            onehot = fromperm_ref[...]
            for c0 in range(0, tm, SCAN_CHUNK):
                hi, mid, lo = _split3(xf[c0:c0 + SCAN_CHUNK, :])
                o_ref[row0 + c0:row0 + c0 + SCAN_CHUNK, :] = (
                    _dot(onehot, hi) + _dot(onehot, mid) + _dot(onehot, lo))
        else:
            o_ref[row0:row0 + tm, :] = xn
            w = lln_ref[...] * (1.0 + lmod_ref[0, 1:2, :])
            y = xn * lax.rsqrt(jnp.mean(xn * xn, axis=-1, keepdims=True) + EPS)
            u_rows = (y * w + lmod_ref[0, 0:1, :]).astype(unext_ref.dtype)
            unext_ref[row0:row0 + tm, :] = u_rows
            _dt_parts(u_rows, wdt_ref, hp_ref, cum_ref, perm_ref, parts_ref, row0)

    def norm_in_pieces(src_ref, src0, mod, u_ref):
        todo = list(range(0, tm, MLP_NORM_ROWS))
        per_call = -(-len(todo) // (2 * (D_FF // tf)))

        def side_work():
            for _ in range(per_call):
                if todo:
                    norm_rows(src_ref, src0, mod, u_ref, todo.pop(0))
        return side_work

    mlp_tile(ua_ref, 0, norm_in_pieces(x_ref, tm, mod_ref, ub_ref))
    mlp_tile(ub_ref, tm, norm_in_pieces(xnext_ref, 0, modnext_ref, ua_ref))


def _mlp_call(x2, mod_l, ln, w_up, w_dn, layer, fnorm, from_perm, lead, seq, final,
              tm=512, tf=2048):
    lmod, lln, lwdt, lhp, cum, perm = lead
    t_total = x2.shape[0]
    n_blocks = t_total // (2 * tm)
    blocks_per_batch = seq // (2 * tm)
    nxt = lambda i: jnp.minimum(i + 1, n_blocks - 1)
    resident = dict(pipeline_mode=pl.Buffered(1))
    rows = pl.BlockSpec((2 * tm, D_MODEL), lambda i: (i, 0))
    out_specs = [rows]
    out_shape = [jax.ShapeDtypeStruct((t_total, D_MODEL), jnp.float32)]
    if not final:
        out_specs += [rows, pl.BlockSpec((GROUPS * PART_ROWS, 2 * tm), lambda i: (0, i))]
        out_shape += [jax.ShapeDtypeStruct((t_total, D_MODEL), jnp.bfloat16),
                      jax.ShapeDtypeStruct((GROUPS * PART_ROWS, t_total), jnp.bfloat16)]
    return pl.pallas_call(
        functools.partial(_mlp_kernel, tm=tm, tf=tf, final=final),
        grid=(n_blocks,),
        in_specs=[
            pl.BlockSpec((2 * tm, D_MODEL), lambda i: (i, 0)),
            pl.BlockSpec((tm, D_MODEL), lambda i: (2 * nxt(i), 0)),
            pl.BlockSpec((1, MOD_ROWS, D_MODEL), lambda i: (i // blocks_per_batch, 0, 0)),
            pl.BlockSpec((1, MOD_ROWS, D_MODEL), lambda i: (nxt(i) // blocks_per_batch, 0, 0)),
            pl.BlockSpec((1, D_MODEL), lambda i: (0, 0)),
            pl.BlockSpec((1, D_MODEL, D_FF), lambda i: (layer, 0, 0), **resident),
            pl.BlockSpec((1, D_FF, D_MODEL), lambda i: (layer, 0, 0), **resident),
            pl.BlockSpec((1, D_MODEL), lambda i: (0, 0)),
            pl.BlockSpec((SCAN_CHUNK, SCAN_CHUNK), lambda i: (0, 0)),
            pl.BlockSpec((1, MOD_ROWS, D_MODEL), lambda i: (i // blocks_per_batch, 0, 0)),
            pl.BlockSpec((1, D_MODEL), lambda i: (0, 0)),
            pl.BlockSpec((LANES, D_MODEL), lambda i: (0, 0)),
            pl.BlockSpec((2, HEADS, LANES), lambda i: (0, 0, 0)),
            pl.BlockSpec((SCAN_CHUNK, SCAN_CHUNK), lambda i: (0, 0)),
            pl.BlockSpec((GROUPS * PART_ROWS, 2 * N_SPLIT * HEADS), lambda i: (0, 0)),
        ],
        out_specs=out_specs,
        out_shape=out_shape,
        scratch_shapes=[pltpu.VMEM((tm, D_MODEL), jnp.bfloat16),
                        pltpu.VMEM((tm, D_MODEL), jnp.bfloat16)],
        compiler_params=_cparams(1, VMEM_LIMIT_LARGE),
        name="mlp",
    )(x2, x2, mod_l, mod_l, ln, w_up, w_dn, fnorm, from_perm, lmod, lln, lwdt, lhp, cum, perm)


def _head_constants():
    e_t = np.zeros((PART_ROWS, HEADS_PER_GROUP, LANES), np.float32)
    perm = np.zeros((GROUPS, PART_ROWS, 2 * N_SPLIT, HEADS), np.float32)
    for k in range(2 * N_SPLIT):
        for r in range(HEADS_PER_GROUP):
            if k < N_SPLIT:
                e_t[k * HEADS_PER_GROUP + r, r, :] = 1.0
            for g in range(GROUPS):
                perm[g, k * HEADS_PER_GROUP + r, k, g * HEADS_PER_GROUP + r] = 1.0
    e_t = e_t.reshape(PART_ROWS, HEADS_PER_GROUP * LANES)
    perm = perm.reshape(GROUPS * PART_ROWS, 2 * N_SPLIT * HEADS)
    tm = _chunk_time(np.arange(SCAN_CHUNK))
    cum = (tm[:, None] <= tm[None, :]).astype(np.float32)
    to_perm = (np.arange(SCAN_CHUNK)[None, :] == tm[:, None]).astype(np.float32)
    bf = jnp.bfloat16
    return (jnp.asarray(e_t, bf), jnp.asarray(perm, bf), jnp.asarray(cum, bf),
            jnp.asarray(to_perm, bf), jnp.asarray(to_perm.T, bf))


def _regroup_kernel(xs_ref, bs_ref, cs_ref, z_ref, cb_ref, cc_ref, cx_ref, gate_ref, dt_ref,
                    o_ref, odt_ref):
    for ref, lo, hi in ((xs_ref, G_XBC, G_XBC + GROUP_X), (bs_ref, G_XBC + GROUP_X, G_Z - STATE),
                        (cs_ref, G_Z - STATE, G_Z), (z_ref, G_Z, G_CB), (cb_ref, G_CB, G_CC),
                        (cc_ref, G_CC, G_CX), (cx_ref, G_CX, G_GATE), (gate_ref, G_GATE, GROUP_COLS)):
        o_ref[0, :, lo:hi] = ref[0].T.astype(o_ref.dtype)

    @pl.when(pl.program_id(1) == 0)
    def _():
        row = lax.broadcasted_iota(jnp.int32, (LANES, D_MODEL), 0)
        odt_ref[0] = jnp.where(row < HEADS, dt_ref[0], 0.0).astype(odt_ref.dtype)


def _regroup_w_in(w_t):
    def rows(height, off, paired):
        base = off // height
        if paired:
            return pl.BlockSpec((1, height, D_MODEL),
                                lambda l, g: (l, base + (g % 2) * PAIRS + g // 2, 0))
        return pl.BlockSpec((1, height, D_MODEL), lambda l, g: (l, base + g, 0))

    return pl.pallas_call(
        _regroup_kernel,
        grid=(DEPTH, GROUPS),
        in_specs=[
            rows(GROUP_X, OFF_XS, False), rows(STATE, OFF_BS, False), rows(STATE, OFF_CS, False),
            rows(GROUP_X, OFF_Z, False),
            rows(SHORT_PER_GROUP, OFF_CB, True), rows(SHORT_PER_GROUP, OFF_CC, True),
            rows(SHORT_PER_GROUP, OFF_CX, True), rows(GATE_PER_GROUP, 0, True),
            pl.BlockSpec((1, LANES, D_MODEL), lambda l, g: (l, OFF_DT // LANES, 0)),
        ],
        out_specs=[
            pl.BlockSpec((1, D_MODEL, GROUP_COLS), lambda l, g: (l, 0, g)),
            pl.BlockSpec((1, LANES, D_MODEL), lambda l, g: (l, 0, 0)),
        ],
        out_shape=[
            jax.ShapeDtypeStruct((DEPTH, D_MODEL, GROUPS * GROUP_COLS), jnp.bfloat16),
            jax.ShapeDtypeStruct((DEPTH, LANES, D_MODEL), jnp.bfloat16),
        ],
        compiler_params=_cparams(2),
        name="w_in_regroup",
    )(*([w_t] * 9))


def _group_conv_params(p):
    k = p.shape[1]
    xs = p[..., :D_SSM].reshape(DEPTH, k, GROUPS, GROUP_X)
    bs = p[..., D_SSM:D_SSM + GROUPS * STATE].reshape(DEPTH, k, GROUPS, STATE)
    cs = p[..., D_SSM + GROUPS * STATE:].reshape(DEPTH, k, GROUPS, STATE)
    return jnp.swapaxes(jnp.concatenate([xs, bs, cs], axis=-1), 1, 2)


def kernel(x, c, w_ada, b_ada, ln1, ln2, w_in, conv_w, ssm_conv_w, ssm_conv_b, dt_bias, a_log,
           d_skip, ssm_norm_w, w_conv_out, w_ssm_out, w_o, w_up, w_down, final_norm):
    bsz, seq, _ = x.shape
    bf = jnp.bfloat16
    f32 = jnp.float32

    c_pad = jnp.zeros((SUBLANES, D_MODEL), f32).at[:bsz].set(c)
    mod_all = _ada_call(c_pad, w_ada, b_ada)
    mod_all = mod_all[:, :bsz].reshape(DEPTH, bsz, N_MOD, D_MODEL)
    mod_all = jnp.pad(mod_all, ((0, 0), (0, 0), (0, MOD_ROWS - N_MOD), (0, 0)))

    w_pairs, w_dt = _regroup_w_in(jnp.swapaxes(w_in, 1, 2))
    hp = jnp.broadcast_to(jnp.stack([dt_bias, a_log], axis=1)[..., None],
                          (DEPTH, 2, HEADS, LANES))
    cw = _group_conv_params(ssm_conv_w)
    cbias = _group_conv_params(ssm_conv_b[:, None, :])
    scw = conv_w.reshape(DEPTH, CONV_WIDTH, 2, PAIRS, SHORT_PER_GROUP)
    scw = jnp.transpose(scw, (0, 3, 2, 1, 4)).reshape(DEPTH, GROUPS, CONV_WIDTH, SHORT_PER_GROUP)
    dskip_e = jnp.repeat(d_skip, HEAD_DIM, axis=-1).reshape(DEPTH, GROUPS, 1, GROUP_X)
    nw = ssm_norm_w.reshape(DEPTH, GROUPS, 1, GROUP_X)
    ws = w_ssm_out.astype(bf)
    wc = w_conv_out.astype(bf)
    wo = w_o.astype(bf)
    wup = w_up.astype(bf)
    wdn = w_down.astype(bf)

    e_t, perm, cum, to_perm, from_perm = _head_constants()
    fnorm = final_norm.reshape(1, D_MODEL)
    x2 = x.reshape(bsz * seq, D_MODEL)

    x2, u, parts = _pre_call(x2, mod_all[0], ln1[0].reshape(1, D_MODEL), w_dt[0], hp[0], cum,
                             perm, seq, to_perm=to_perm)
    for l in range(DEPTH):
        mod_l = mod_all[l]
        final = l == DEPTH - 1
        mixed = _mixer_call(u, w_pairs, l, parts, cw[l], cbias[l], scw[l], dskip_e[l], nw[l], e_t,
                            bsz, seq)
        x2 = _out_call(mixed, x2, mod_l, wc, ws, wo, l, seq)
        n = l if final else l + 1
        lead = (mod_all[n], ln1[n].reshape(1, D_MODEL), w_dt[n], hp[n], cum, perm)
        res = _mlp_call(x2, mod_l, ln2[l].reshape(1, D_MODEL), wup, wdn, l, fnorm, from_perm, lead,
                        seq, final=final)
        if final:
            x2 = res[0]
        else:
            x2, u, parts = res
    return x2.reshape(bsz, seq, D_MODEL)
```
